```python
import jax
import jax.numpy as jnp
from jax import lax
import numpy as np

D_MODEL = 1024
BATCH = 32
SEQ = 256
DEPTH = 2
DEC_BATCH = 8
DEC_SEQ = 4096
PAST_LEN = 512

GRID_W = 64
N_MIXERS = 2
N_MLSTM_LAYERS = (DEPTH + 1) // 2
N_ATTN_LAYERS = DEPTH // 2
D_FF = 2816
FFN_RES = 0.5
NH_M = 8
DK_M = 64
DV_M = 128
MLSTM_CHUNK = 128
NH_A = 16
NKV_A = 4
G_A = NH_A // NKV_A
HD_A = 64
WINDOW = 128
ATTN_BLOCK = 128
ROPE_THETA = 10000.0
EPS = 1e-6
NEG_INF = -1e30
N_MOD = 9
M_IN = 2 * NH_M * DK_M + 2 * NH_M * DV_M + 4 * NH_M
A_IN = (NH_A + 2 * NKV_A) * HD_A

kernel_name = "hybrid_mlstm_swa_macaron_dit_step"


def _rmsnorm(x, g):
    xf = x.astype(jnp.float32)
    y = xf * lax.rsqrt(jnp.mean(xf * xf, axis=-1, keepdims=True) + EPS)
    return (y * g.astype(jnp.float32)).astype(x.dtype)


def _adaln(cond, w_mod, b_mod):
    return (jax.nn.silu(cond) @ w_mod + b_mod).reshape(cond.shape[0], N_MOD, D_MODEL)


def _modulated_norm(x, g, shift, scale):
    return _rmsnorm(x, g) * (1 + scale[:, None, :]) + shift[:, None, :]


def _swiglu(h, w_gu, w_down):
    gate, up = jnp.split(h @ w_gu, 2, axis=-1)
    return (jax.nn.silu(gate) * up) @ w_down


def _ffn_sub(x, mod, j, g, w_gu, w_down):
    shift, scale, gate = mod[:, 3 * j], mod[:, 3 * j + 1], mod[:, 3 * j + 2]
    return x + FFN_RES * gate[:, None, :] * _swiglu(_modulated_norm(x, g, shift, scale), w_gu, w_down)


def _mlstm_chunkwise(q, k, v, log_i, log_f, c0, n0, m0):
    bsz, n_tok, nh, _ = q.shape
    dv = v.shape[-1]
    nc = n_tok // MLSTM_CHUNK

    def chunks(a):
        a = a.reshape((bsz, nc, MLSTM_CHUNK) + a.shape[2:])
        return jnp.moveaxis(jnp.moveaxis(a, 1, 0), 3, 2)

    tril = jnp.tril(jnp.ones((MLSTM_CHUNK, MLSTM_CHUNK), bool))

    def step(carry, xs):
        c, n, m = carry
        qc, kc, vc, li, lf = xs
        b = jnp.cumsum(lf, axis=-1)
        d = jnp.where(tril, b[..., :, None] - b[..., None, :] + li[..., None, :], -jnp.inf)
        m_inter = b + m[..., None]
        m_t = jnp.maximum(jnp.max(d, axis=-1), m_inter)
        w = jnp.einsum('bhtd,bhsd->bhts', qc, kc) * jnp.exp(d - m_t[..., None])
        inter = jnp.exp(m_inter - m_t)
        num = jnp.einsum('bhts,bhsv->bhtv', w, vc) + inter[..., None] * jnp.einsum('bhtd,bhdv->bhtv', qc, c)
        den = jnp.sum(w, axis=-1) + inter * jnp.einsum('bhtd,bhd->bht', qc, n)
        h = num / jnp.maximum(jnp.abs(den), jnp.exp(-m_t))[..., None]
        b_end = b[..., -1]
        to_end = b_end[..., None] - b + li
        m_new = jnp.maximum(b_end + m, jnp.max(to_end, axis=-1))
        w_end = jnp.exp(to_end - m_new[..., None])
        keep = jnp.exp(b_end + m - m_new)
        c_new = keep[..., None, None] * c + jnp.einsum('bhs,bhsd,bhsv->bhdv', w_end, kc, vc)
        n_new = keep[..., None] * n + jnp.einsum('bhs,bhsd->bhd', w_end, kc)
        return (c_new, n_new, m_new), h

    xs = (chunks(q), chunks(k), chunks(v), chunks(log_i), chunks(log_f))
    (c_f, n_f, m_f), h = lax.scan(step, (c0, n0, m0), xs)
    h = jnp.swapaxes(jnp.moveaxis(h, 0, 1), 2, 3).reshape(bsz, n_tok, nh, dv)
    return h, c_f, n_f, m_f


def _mlstm_mixer(h, w_in, b_gate, g_head, w_out, c0, n0, m0):
    f32 = jnp.float32
    bsz, n_tok, _ = h.shape
    qk_w, v_w = NH_M * DK_M, NH_M * DV_M
    q, k, v, o, gates = jnp.split(h @ w_in, [qk_w, 2 * qk_w, 2 * qk_w + v_w, 2 * qk_w + 2 * v_w], axis=-1)
    q = q.reshape(bsz, n_tok, NH_M, DK_M).astype(f32) * DK_M ** -0.5
    k = k.reshape(bsz, n_tok, NH_M, DK_M).astype(f32)
    v = v.reshape(bsz, n_tok, NH_M, DV_M).astype(f32)
    gates = (gates.astype(f32) + b_gate.astype(f32)).reshape(bsz, n_tok, 2, 2, NH_M)
    log_i = gates[:, :, :, 0]
    log_f = jax.nn.log_sigmoid(gates[:, :, :, 1])
    c0, n0, m0 = c0.astype(f32), n0.astype(f32), m0.astype(f32)
    h_f, c_f, n_f, m_f = _mlstm_chunkwise(q, k, v, log_i[:, :, 0], log_f[:, :, 0], c0[:, 0], n0[:, 0], m0[:, 0])
    rev = lambda a: jnp.flip(a, axis=1)
    h_b, c_b, n_b, m_b = _mlstm_chunkwise(rev(q), rev(k), rev(v), rev(log_i[:, :, 1]), rev(log_f[:, :, 1]),
                                          c0[:, 1], n0[:, 1], m0[:, 1])
    hs = h_f + rev(h_b)
    hs = hs * lax.rsqrt(jnp.mean(hs * hs, axis=-1, keepdims=True) + EPS) * g_head.astype(f32).reshape(NH_M, DV_M)
    out = (hs.reshape(bsz, n_tok, v_w).astype(h.dtype) * jax.nn.sigmoid(o)) @ w_out
    dt = h.dtype
    return (out, jnp.stack([c_f, c_b], axis=1).astype(dt), jnp.stack([n_f, n_b], axis=1).astype(dt),
            jnp.stack([m_f, m_b], axis=1).astype(dt))


def _attn_project(h, w_in):
    bsz, n_tok, _ = h.shape
    q, k, v = jnp.split(h @ w_in, [NH_A * HD_A, (NH_A + NKV_A) * HD_A], axis=-1)
    return (q.reshape(bsz, n_tok, NKV_A, G_A, HD_A), k.reshape(bsz, n_tok, NKV_A, HD_A),
            v.reshape(bsz, n_tok, NKV_A, HD_A))


def _axial_rope_tables(n_rows):
    quarter = HD_A // 4
    freqs = ROPE_THETA ** (-jnp.arange(quarter, dtype=jnp.float32) / quarter)
    row = jnp.repeat(jnp.arange(n_rows, dtype=jnp.float32), GRID_W)
    col = jnp.tile(jnp.arange(GRID_W, dtype=jnp.float32), n_rows)
    ang_r, ang_c = row[:, None] * freqs, col[:, None] * freqs
    ang = jnp.concatenate([ang_r, ang_r, ang_c, ang_c], axis=-1)
    return jnp.cos(ang), jnp.sin(ang)


def _axial_rope(x, cos, sin):
    shp = (cos.shape[0],) + (1,) * (x.ndim - 3) + (HD_A,)
    cs, sn = cos.reshape(shp).astype(x.dtype), sin.reshape(shp).astype(x.dtype)
    xr = x.reshape(x.shape[:-1] + (2, 2, HD_A // 4))
    rot = jnp.stack([-xr[..., 1, :], xr[..., 0, :]], axis=-2).reshape(x.shape)
    return x * cs + rot * sn


def _sink_softmax_attend(qb, k, v, sink, mask):
    s = jnp.einsum('blkgd,bskd->bkgls', qb, k).astype(jnp.float32) * HD_A ** -0.5
    if mask is not None:
        s = jnp.where(mask, s, NEG_INF)
    sink_col = jnp.broadcast_to(sink.astype(jnp.float32).reshape(1, NKV_A, G_A, 1, 1), s.shape[:-1] + (1,))
    p = jax.nn.softmax(jnp.concatenate([s, sink_col], axis=-1), axis=-1)[..., :-1]
    return jnp.einsum('bkgls,bskd->blkgd', p.astype(v.dtype), v)


def _attn_context(q, k, v, sink):
    bsz, n_ctx = q.shape[:2]

    def block(j):
        qb = lax.dynamic_slice_in_dim(q, j * ATTN_BLOCK, ATTN_BLOCK, axis=1)
        return _sink_softmax_attend(qb, k, v, sink, None)

    out = lax.map(block, jnp.arange(n_ctx // ATTN_BLOCK))
    return jnp.moveaxis(out, 0, 1).reshape(bsz, n_ctx, NH_A * HD_A)


def _attn_latent(q, k, v, k_ctx, v_ctx, sink):
    bsz, n_tok = q.shape[:2]
    n_ctx = k_ctx.shape[1]
    pad = ((0, 0), (ATTN_BLOCK, ATTN_BLOCK), (0, 0), (0, 0))
    k_pad, v_pad = jnp.pad(k, pad), jnp.pad(v, pad)
    ctx_mask = jnp.ones((ATTN_BLOCK, n_ctx), bool)

    def block(j):
        start = j * ATTN_BLOCK
        qb = lax.dynamic_slice_in_dim(q, start, ATTN_BLOCK, axis=1)
        kb = lax.dynamic_slice_in_dim(k_pad, start, 3 * ATTN_BLOCK, axis=1)
        vb = lax.dynamic_slice_in_dim(v_pad, start, 3 * ATTN_BLOCK, axis=1)
        t_pos = start + jnp.arange(ATTN_BLOCK)
        s_pos = start - ATTN_BLOCK + jnp.arange(3 * ATTN_BLOCK)
        local = ((jnp.abs(t_pos[:, None] - s_pos[None, :]) <= WINDOW)
                 & (s_pos >= 0)[None, :] & (s_pos < n_tok)[None, :])
        mask = jnp.concatenate([local, ctx_mask], axis=1)
        return _sink_softmax_attend(qb, jnp.concatenate([kb, k_ctx], axis=1),
                                    jnp.concatenate([vb, v_ctx], axis=1), sink, mask)

    out = lax.map(block, jnp.arange(n_tok // ATTN_BLOCK))
    return jnp.moveaxis(out, 0, 1).reshape(bsz, n_tok, NH_A * HD_A)


def setup_inputs(seed: int = 0) -> dict:
    key = jax.random.key(seed)
    ks = jax.random.split(key, 24)
    f32 = jnp.float32
    nrm = lambda kk, shp: jax.random.normal(kk, shp, f32)
    D = D_MODEL
    gate_base = jnp.tile(jnp.concatenate([jnp.full((NH_M,), -1.0, f32), jnp.full((NH_M,), 3.0, f32)]), 2)
    return {
        "x_prompt": nrm(ks[0], (BATCH, SEQ, D)),
        "x_sample": nrm(ks[1], (DEC_BATCH, DEC_SEQ, D)),
        "state_c": 0.5 * nrm(ks[2], (DEC_BATCH, N_MLSTM_LAYERS, 2, NH_M, DK_M, DV_M)),
        "state_n": 0.5 * nrm(ks[3], (DEC_BATCH, N_MLSTM_LAYERS, 2, NH_M, DK_M)),
        "state_m": nrm(ks[4], (DEC_BATCH, N_MLSTM_LAYERS, 2, NH_M)),
        "cache_k": nrm(ks[5], (DEC_BATCH, N_ATTN_LAYERS, PAST_LEN, NKV_A, HD_A)),
        "cache_v": nrm(ks[6], (DEC_BATCH, N_ATTN_LAYERS, PAST_LEN, NKV_A, HD_A)),
        "c": nrm(ks[7], (DEC_BATCH, D)),
        "c_ctx": nrm(ks[8], (D,)),
        "w_mod": 0.5 * D ** -0.5 * nrm(ks[9], (DEPTH, D, N_MOD * D)),
        "b_mod": 0.02 * nrm(ks[10], (DEPTH, N_MOD * D)),
        "norm_g": 1.0 + 0.02 * nrm(ks[11], (DEPTH, 3, D)),
        "ffn1_w_gu": D ** -0.5 * nrm(ks[12], (DEPTH, D, 2 * D_FF)),
        "ffn1_w_down": D_FF ** -0.5 * nrm(ks[13], (DEPTH, D_FF, D)),
        "ffn2_w_gu": D ** -0.5 * nrm(ks[14], (DEPTH, D, 2 * D_FF)),
        "ffn2_w_down": D_FF ** -0.5 * nrm(ks[15], (DEPTH, D_FF, D)),
        "mlstm_w_in": D ** -0.5 * nrm(ks[16], (N_MLSTM_LAYERS, D, M_IN)),
        "mlstm_b_gate": gate_base[None, :] + 0.1 * nrm(ks[17], (N_MLSTM_LAYERS, 4 * NH_M)),
        "mlstm_g_head": 1.0 + 0.02 * nrm(ks[18], (N_MLSTM_LAYERS, NH_M * DV_M)),
        "mlstm_w_out": (NH_M * DV_M) ** -0.5 * nrm(ks[19], (N_MLSTM_LAYERS, NH_M * DV_M, D)),
        "attn_w_in": D ** -0.5 * nrm(ks[20], (N_ATTN_LAYERS, D, A_IN)),
        "attn_sink": 0.5 * nrm(ks[21], (N_ATTN_LAYERS, NH_A)),
        "attn_w_out": (NH_A * HD_A) ** -0.5 * nrm(ks[22], (N_ATTN_LAYERS, NH_A * HD_A, D)),
        "final_g": 1.0 + 0.02 * nrm(ks[23], (D,)),
    }


def reference(x_prompt, x_sample, state_c, state_n, state_m, cache_k, cache_v, c, c_ctx, w_mod, b_mod, norm_g,
              ffn1_w_gu, ffn1_w_down, ffn2_w_gu, ffn2_w_down, mlstm_w_in, mlstm_b_gate, mlstm_g_head, mlstm_w_out,
              attn_w_in, attn_sink, attn_w_out, final_g):
    n_rows = x_sample.shape[1] // GRID_W
    cos, sin = _axial_rope_tables(n_rows)
    bp = x_prompt.shape[0]
    xp, xs = x_prompt, x_sample
    new_c, new_n, new_m, new_k, new_v = [], [], [], [], []
    for l in range(DEPTH):
        mod_p = _adaln(c_ctx[None, :], w_mod[l], b_mod[l])
        mod_s = _adaln(c, w_mod[l], b_mod[l])
        xp = _ffn_sub(xp, mod_p, 0, norm_g[l, 0], ffn1_w_gu[l], ffn1_w_down[l])
        xs = _ffn_sub(xs, mod_s, 0, norm_g[l, 0], ffn1_w_gu[l], ffn1_w_down[l])
        hp = _modulated_norm(xp, norm_g[l, 1], mod_p[:, 3], mod_p[:, 4])
        hs = _modulated_norm(xs, norm_g[l, 1], mod_s[:, 3], mod_s[:, 4])
        i = l // N_MIXERS
        if l % N_MIXERS == 0:
            zc = jnp.zeros((bp, 2, NH_M, DK_M, DV_M), xp.dtype)
            zn = jnp.zeros((bp, 2, NH_M, DK_M), xp.dtype)
            zm = jnp.zeros((bp, 2, NH_M), xp.dtype)
            out_p, sc, sn, sm = _mlstm_mixer(hp, mlstm_w_in[i], mlstm_b_gate[i], mlstm_g_head[i], mlstm_w_out[i],
                                             zc, zn, zm)
            out_s, _, _, _ = _mlstm_mixer(hs, mlstm_w_in[i], mlstm_b_gate[i], mlstm_g_head[i], mlstm_w_out[i],
                                          state_c[:, i], state_n[:, i], state_m[:, i])
            new_c.append(sc)
            new_n.append(sn)
            new_m.append(sm)
        else:
            qp, kp, vp = _attn_project(hp, attn_w_in[i])
            out_p = _attn_context(qp, kp, vp, attn_sink[i]) @ attn_w_out[i]
            qs, ks_, vs = _attn_project(hs, attn_w_in[i])
            qs, ks_ = _axial_rope(qs, cos, sin), _axial_rope(ks_, cos, sin)
            out_s = _attn_latent(qs, ks_, vs, cache_k[:, i], cache_v[:, i], attn_sink[i]) @ attn_w_out[i]
            new_k.append(kp)
            new_v.append(vp)
        xp = xp + mod_p[:, 5][:, None, :] * out_p
        xs = xs + mod_s[:, 5][:, None, :] * out_s
        xp = _ffn_sub(xp, mod_p, 2, norm_g[l, 2], ffn2_w_gu[l], ffn2_w_down[l])
        xs = _ffn_sub(xs, mod_s, 2, norm_g[l, 2], ffn2_w_gu[l], ffn2_w_down[l])
    y_prompt = _rmsnorm(xp, final_g)
    y_sample = _rmsnorm(xs, final_g)
    new_state_c = jnp.stack(new_c, axis=1)
    new_state_n = jnp.stack(new_n, axis=1)
    new_state_m = jnp.stack(new_m, axis=1)
    new_cache_k = jnp.stack(new_k, axis=1)
    new_cache_v = jnp.stack(new_v, axis=1)
    return (y_prompt, y_sample, new_state_c, new_state_n, new_state_m, new_cache_k, new_cache_v)
```

```python
import functools

import jax
import jax.numpy as jnp
from jax import lax
from jax.experimental import pallas as pl
from jax.experimental.pallas import tpu as pltpu

D_MODEL = 1024
DEPTH = 2
GRID_W = 64
D_FF = 2816
FFN_RES = 0.5
NH_M = 8
DK_M = 64
DV_M = 128
NH_A = 16
NKV_A = 4
G_A = NH_A // NKV_A
HD_A = 64
WINDOW = 128
ATTN_BLOCK = 128
ROPE_THETA = 10000.0
EPS = 1e-6
NEG_INF = -1e30
N_MOD = 9

QK_W = NH_M * DK_M
V_W = NH_M * DV_M
KV_W = NKV_A * HD_A
Q_W = NH_A * HD_A

LANES = 128
MLSTM_L = 128
FFN_CHUNK = 256
COND_ROWS = 16
VMEM_CAP = 60000 * 1024

F32 = jnp.float32
BF16 = jnp.bfloat16


def _cparams(sem, vmem_bytes):
    return pltpu.CompilerParams(dimension_semantics=sem, vmem_limit_bytes=min(int(vmem_bytes), VMEM_CAP))


def _dot(a, b):
    return jnp.dot(a, b, preferred_element_type=F32)


def _dot_nt(a, b):
    return lax.dot_general(a, b, (((1,), (1,)), ((), ())), preferred_element_type=F32)


def _rms(x):
    return x * lax.rsqrt(jnp.mean(x * x, axis=-1, keepdims=True) + EPS)


def _sigmoid(x):
    return 1.0 / (1.0 + jnp.exp(-x))


def _mod_norm(x, g, shift, scale):
    return (_rms(x) * g) * (1.0 + scale) + shift


def _mod_kernel(c_ref, w_ref, b_ref, o_ref):
    c = c_ref[...]
    s = (c * _sigmoid(c)).astype(BF16)
    o_ref[0] = _dot(s, w_ref[0].astype(BF16)) + b_ref[0]


def _mod_call(cond, w_mod, b_mod):
    tn = D_MODEL
    n_out = N_MOD * D_MODEL
    return pl.pallas_call(
        _mod_kernel,
        grid=(DEPTH, n_out // tn),
        in_specs=[
            pl.BlockSpec((COND_ROWS, D_MODEL), lambda l, n: (0, 0)),
            pl.BlockSpec((1, D_MODEL, tn), lambda l, n: (l, 0, n)),
            pl.BlockSpec((1, 1, tn), lambda l, n: (l, 0, n)),
        ],
        out_specs=pl.BlockSpec((1, COND_ROWS, tn), lambda l, n: (l, 0, n)),
        out_shape=jax.ShapeDtypeStruct((DEPTH, COND_ROWS, n_out), F32),
        compiler_params=_cparams(("parallel", "parallel"), 4 * (2 * D_MODEL * tn * 4)),
        name="adaln_mod",
    )(cond, w_mod, b_mod.reshape(DEPTH, 1, n_out))


def _mod_spec(mod, tiles_per_seq):
    if mod.shape[0] == 1:
        return pl.BlockSpec((1, N_MOD, D_MODEL), lambda i: (0, 0, 0))
    return pl.BlockSpec((1, N_MOD, D_MODEL), lambda i: (i // tiles_per_seq, 0, 0))


def _row_spec(width):
    return pl.BlockSpec((1, width), lambda i: (0, 0))


def _resident(shape):
    return pl.BlockSpec(shape, lambda i: (0,) * len(shape), pipeline_mode=pl.Buffered(1))


def _ffn_kernel(x_ref, mod_ref, g_ref, wgu_ref, wd_ref, fg_ref, o_ref, *, j, final):
    x = x_ref[...]
    mod = mod_ref[0]
    shift, scale, gate = mod[3 * j:3 * j + 1], mod[3 * j + 1:3 * j + 2], mod[3 * j + 2:3 * j + 3]
    hb = _mod_norm(x, g_ref[...], shift, scale).astype(BF16)
    acc = jnp.zeros(x.shape, F32)
    for c in range(D_FF // FFN_CHUNK):
        lo = c * FFN_CHUNK
        gg = _dot(hb, wgu_ref[:, lo:lo + FFN_CHUNK])
        uu = _dot(hb, wgu_ref[:, D_FF + lo:D_FF + lo + FFN_CHUNK])
        act = (gg * _sigmoid(gg) * uu).astype(BF16)
        acc = acc + _dot(act, wd_ref[lo:lo + FFN_CHUNK, :])
    y = x + (FFN_RES * gate) * acc
    if final:
        y = _rms(y) * fg_ref[...]
    o_ref[...] = y


def _ffn_call(x, mod, g, wgu, wd, j, seq_len, tm, final_g=None):
    t = x.shape[0]
    final = final_g is not None
    fg = final_g if final else g
    vmem = 2 * (D_MODEL * 2 * D_FF + D_FF * D_MODEL) + 12 * tm * D_MODEL * 4 + (8 << 20)
    return pl.pallas_call(
        functools.partial(_ffn_kernel, j=j, final=final),
        grid=(t // tm,),
        in_specs=[
            pl.BlockSpec((tm, D_MODEL), lambda i: (i, 0)),
            _mod_spec(mod, seq_len // tm),
            _row_spec(D_MODEL),
            _resident((D_MODEL, 2 * D_FF)),
            _resident((D_FF, D_MODEL)),
            _row_spec(D_MODEL),
        ],
        out_specs=pl.BlockSpec((tm, D_MODEL), lambda i: (i, 0)),
        out_shape=jax.ShapeDtypeStruct((t, D_MODEL), F32),
        compiler_params=_cparams(("parallel",), vmem),
        name=f"ffn{j // 2 + 1}",
    )(x, mod, g, wgu, wd, fg)


def _mlstm_in_kernel(x_ref, mod_ref, g_ref, wq_ref, wkt_ref, wvo_ref, wg_ref, bg_ref,
                     q_ref, kt_ref, v_ref, o_ref, gt_ref):
    mod = mod_ref[0]
    hb = _mod_norm(x_ref[...], g_ref[...], mod[3:4], mod[4:5]).astype(BF16)
    q_ref[...] = (_dot(hb, wq_ref[...]) * DK_M ** -0.5).astype(BF16)
    kt_ref[...] = _dot_nt(wkt_ref[...], hb).astype(BF16)
    vo = _dot(hb, wvo_ref[...])
    v_ref[...] = vo[:, :V_W].astype(BF16)
    o_ref[...] = vo[:, V_W:].astype(BF16)
    gt_ref[...] = _dot(hb, wg_ref[...]) + bg_ref[...]


def _mlstm_in_call(x, mod, g, wq, wkt, wvo, wg, bg, seq_len, tm):
    t = x.shape[0]
    w_bytes = 2 * (D_MODEL * (2 * QK_W + 2 * V_W + 2 * LANES))
    vmem = w_bytes + 2 * tm * (D_MODEL * 4 + (2 * QK_W + 2 * V_W) * 2 + 2 * LANES * 4) + 6 * tm * D_MODEL * 4 + (4 << 20)
    return pl.pallas_call(
        _mlstm_in_kernel,
        grid=(t // tm,),
        in_specs=[
            pl.BlockSpec((tm, D_MODEL), lambda i: (i, 0)),
            _mod_spec(mod, seq_len // tm),
            _row_spec(D_MODEL),
            _resident((D_MODEL, QK_W)),
            _resident((QK_W, D_MODEL)),
            _resident((D_MODEL, 2 * V_W)),
            _resident((D_MODEL, 2 * LANES)),
            _row_spec(2 * LANES),
        ],
        out_specs=[
            pl.BlockSpec((tm, QK_W), lambda i: (i, 0)),
            pl.BlockSpec((QK_W, tm), lambda i: (0, i)),
            pl.BlockSpec((tm, V_W), lambda i: (i, 0)),
            pl.BlockSpec((tm, V_W), lambda i: (i, 0)),
            pl.BlockSpec((tm, 2 * LANES), lambda i: (i, 0)),
        ],
        out_shape=[
            jax.ShapeDtypeStruct((t, QK_W), BF16),
            jax.ShapeDtypeStruct((QK_W, t), BF16),
            jax.ShapeDtypeStruct((t, V_W), BF16),
            jax.ShapeDtypeStruct((t, V_W), BF16),
            jax.ShapeDtypeStruct((t, 2 * LANES), F32),
        ],
        compiler_params=_cparams(("parallel",), vmem),
        name="mlstm_in",
    )(x, mod, g, wq, wkt, wvo, wg, bg)


def _scan_rows(x, op, fill, reverse):
    n = x.shape[0]
    idx = lax.broadcasted_iota(jnp.int32, x.shape, 0)
    s = 1
    while s < n:
        if reverse:
            x = op(x, jnp.where(idx < n - s, pltpu.roll(x, n - s, axis=0), fill))
        else:
            x = op(x, jnp.where(idx >= s, pltpu.roll(x, s, axis=0), fill))
        s *= 2
    return x


def _mlstm_dir(d, q_ref, kt_ref, v_ref, gt_ref, h_ref, c_sc, m_sc):
    n = MLSTM_L
    gt = gt_ref[...]
    li = gt[:, :LANES]
    fx = gt[:, LANES:]
    lf = jnp.minimum(fx, 0.0) - jnp.log(1.0 + jnp.exp(-jnp.abs(fx)))
    rev = d == 1
    b = _scan_rows(lf, jnp.add, 0.0, rev)
    a = li - b
    cm = _scan_rows(a, jnp.maximum, -jnp.inf, rev)
    m = m_sc[...]
    mt = jnp.maximum(cm, m)
    inter = jnp.exp(m - mt)
    clamp = jnp.exp(-(b + mt))
    end = 0 if rev else n - 1
    mx = jnp.maximum(m, cm[end:end + 1, :])
    keep = jnp.exp(m - mx)
    m_new = b[end:end + 1, :] + mx
    a_t = a.T
    wend_t = jnp.exp(a - mx).T
    row = lax.broadcasted_iota(jnp.int32, (n, n), 0)
    col = lax.broadcasted_iota(jnp.int32, (n, n), 1)
    mask = (col >= row) if rev else (col <= row)
    ones = jnp.ones((n, LANES), BF16)
    for h in range(NH_M):
        c = d * NH_M + h
        qh = q_ref[:, h * DK_M:(h + 1) * DK_M]
        kth = kt_ref[h * DK_M:(h + 1) * DK_M, :]
        vext = jnp.concatenate([v_ref[:, h * DV_M:(h + 1) * DV_M], ones], axis=1)
        s = _dot(qh, kth)
        e = jnp.exp(a_t[c:c + 1, :] - mt[:, c:c + 1])
        w = jnp.where(mask, s * e, 0.0).astype(BF16)
        cext = c_sc[d, h]
        r = _dot(w, vext) + inter[:, c:c + 1] * _dot(qh, cext.astype(BF16))
        den = jnp.maximum(jnp.abs(r[:, DV_M:]), clamp[:, c:c + 1])
        h_ref[:, h * DV_M:(h + 1) * DV_M] = (r[:, :DV_M] / den).astype(BF16)
        kts = (kth.astype(F32) * wend_t[c:c + 1, :]).astype(BF16)
        c_sc[d, h] = keep[:, c:c + 1] * cext + _dot(kts, vext)
    lane = lax.broadcasted_iota(jnp.int32, m.shape, 1)
    m_sc[...] = jnp.where((lane >= d * NH_M) & (lane < (d + 1) * NH_M), m_new, m)


def _mlstm_kernel(qf_ref, ktf_ref, vf_ref, gtf_ref, qb_ref, ktb_ref, vb_ref, gtb_ref, c0_ref, m0_ref,
                  hf_ref, hb_ref, cout_ref, mout_ref, c_sc, m_sc):
    i = pl.program_id(1)

    @pl.when(i == 0)
    def _():
        c_sc[...] = c0_ref[0]
        m_sc[...] = m0_ref[0]

    _mlstm_dir(0, qf_ref, ktf_ref, vf_ref, gtf_ref, hf_ref, c_sc, m_sc)
    _mlstm_dir(1, qb_ref, ktb_ref, vb_ref, gtb_ref, hb_ref, c_sc, m_sc)

    @pl.when(i == pl.num_programs(1) - 1)
    def _():
        cout_ref[0] = c_sc[...]
        mout_ref[0] = m_sc[...]


def _mlstm_core_call(q, kt, v, gt, c0, m0, bsz, seq_len):
    t = q.shape[0]
    nc = seq_len // MLSTM_L
    fwd = lambda b, i: (b * nc + i, 0)
    bwd = lambda b, i: (b * nc + nc - 1 - i, 0)
    fwd_t = lambda b, i: (0, b * nc + i)
    bwd_t = lambda b, i: (0, b * nc + nc - 1 - i)

    def specs(row_map, col_map):
        return [
            pl.BlockSpec((MLSTM_L, QK_W), row_map),
            pl.BlockSpec((QK_W, MLSTM_L), col_map),
            pl.BlockSpec((MLSTM_L, V_W), row_map),
            pl.BlockSpec((MLSTM_L, 2 * LANES), row_map),
        ]

    state_shape = (2, NH_M, DK_M, 2 * DV_M)
    state_bytes = 2 * NH_M * DK_M * 2 * DV_M * 4
    vmem = 5 * state_bytes + 8 * MLSTM_L * (2 * QK_W + 2 * V_W) * 2 + (16 << 20)
    return pl.pallas_call(
        _mlstm_kernel,
        grid=(bsz, nc),
        in_specs=specs(fwd, fwd_t) + specs(bwd, bwd_t) + [
            pl.BlockSpec((1,) + state_shape, lambda b, i: (b, 0, 0, 0, 0)),
            pl.BlockSpec((1, 1, LANES), lambda b, i: (b, 0, 0)),
        ],
        out_specs=[
            pl.BlockSpec((MLSTM_L, V_W), fwd),
            pl.BlockSpec((MLSTM_L, V_W), bwd),
            pl.BlockSpec((1,) + state_shape, lambda b, i: (b, 0, 0, 0, 0)),
            pl.BlockSpec((1, 1, LANES), lambda b, i: (b, 0, 0)),
        ],
        out_shape=[
            jax.ShapeDtypeStruct((t, V_W), BF16),
            jax.ShapeDtypeStruct((t, V_W), BF16),
            jax.ShapeDtypeStruct((bsz,) + state_shape, F32),
            jax.ShapeDtypeStruct((bsz, 1, LANES), F32),
        ],
        scratch_shapes=[pltpu.VMEM(state_shape, F32), pltpu.VMEM((1, LANES), F32)],
        compiler_params=_cparams(("parallel", "arbitrary"), vmem),
        name="mlstm_core",
    )(q, kt, v, gt, q, kt, v, gt, c0, m0)


def _mlstm_out_kernel(x_ref, mod_ref, hf_ref, hb_ref, o_ref, gh_ref, w_ref, y_ref):
    hs = hf_ref[...].astype(F32) + hb_ref[...].astype(F32)
    parts = [_rms(hs[:, h * DV_M:(h + 1) * DV_M]) for h in range(NH_M)]
    hn = jnp.concatenate(parts, axis=1) * gh_ref[...]
    z = (hn * _sigmoid(o_ref[...].astype(F32))).astype(BF16)
    y_ref[...] = x_ref[...] + mod_ref[0][5:6] * _dot(z, w_ref[...])


def _mlstm_out_call(x, mod, hf, hb, o, gh, w, seq_len, tm):
    t = x.shape[0]
    vmem = 2 * V_W * D_MODEL + 2 * tm * (2 * D_MODEL * 4 + 3 * V_W * 2) + 6 * tm * D_MODEL * 4 + (4 << 20)
    tok = lambda width: pl.BlockSpec((tm, width), lambda i: (i, 0))
    return pl.pallas_call(
        _mlstm_out_kernel,
        grid=(t // tm,),
        in_specs=[tok(D_MODEL), _mod_spec(mod, seq_len // tm), tok(V_W), tok(V_W), tok(V_W),
                  _row_spec(V_W), _resident((V_W, D_MODEL))],
        out_specs=tok(D_MODEL),
        out_shape=jax.ShapeDtypeStruct((t, D_MODEL), F32),
        compiler_params=_cparams(("parallel",), vmem),
        name="mlstm_out",
    )(x, mod, hf, hb, o, gh, w)


def _rope_rows_swap(x):
    q = HD_A // 4
    parts = []
    for base in range(0, x.shape[0], 2 * q):
        parts += [x[base + q:base + 2 * q], x[base:base + q]]
    return jnp.concatenate(parts, axis=0)


def _attn_in_kernel(*refs, rope, emit_f32):
    x_ref, mod_ref, g_ref, wq_ref, wkt_ref, wv_ref = refs[:6]
    pos = 6
    if rope:
        cos_ref, sina_ref, sinb_ref, cost_ref, sint_ref = refs[pos:pos + 5]
        pos += 5
    if emit_f32:
        wk_ref = refs[pos]
        pos += 1
    q_ref, kt_ref, v_ref = refs[pos:pos + 3]
    pos += 3
    mod = mod_ref[0]
    hb = _mod_norm(x_ref[...], g_ref[...], mod[3:4], mod[4:5]).astype(BF16)
    q = _dot(hb, wq_ref[...])
    kt = _dot_nt(wkt_ref[...], hb)
    v = _dot(hb, wv_ref[...])
    if rope:
        cos, sina, sinb = cos_ref[...], sina_ref[...], sinb_ref[...]
        for gi in range(Q_W // LANES):
            qg = q[:, gi * LANES:(gi + 1) * LANES]
            qg = qg * cos + pltpu.roll(qg, LANES - HD_A // 4, axis=1) * sina + pltpu.roll(qg, HD_A // 4, axis=1) * sinb
            q_ref[:, gi * LANES:(gi + 1) * LANES] = (qg * HD_A ** -0.5).astype(BF16)
        kt = kt * cost_ref[...] + _rope_rows_swap(kt) * sint_ref[...]
    else:
        q_ref[...] = (q * HD_A ** -0.5).astype(BF16)
    kt_ref[...] = kt.astype(BF16)
    v_ref[...] = v.astype(BF16)
    if emit_f32:
        kf_ref, vf_ref = refs[pos:pos + 2]
        kf_ref[...] = _dot(hb, wk_ref[...])
        vf_ref[...] = v


def _attn_in_call(x, mod, g, wq, wkt, wv, seq_len, tm, rope_tabs=None, wk=None):
    t = x.shape[0]
    rope = rope_tabs is not None
    emit_f32 = wk is not None
    tiles = seq_len // tm
    tok = lambda width: pl.BlockSpec((tm, width), lambda i: (i, 0))
    in_specs = [tok(D_MODEL), _mod_spec(mod, tiles), _row_spec(D_MODEL),
                _resident((D_MODEL, Q_W)), _resident((KV_W, D_MODEL)), _resident((D_MODEL, KV_W))]
    args = [x, mod, g, wq, wkt, wv]
    if rope:
        pos_row = lambda i: (i % tiles, 0)
        pos_col = lambda i: (0, i % tiles)
        in_specs += [pl.BlockSpec((tm, LANES), pos_row)] * 3 + [pl.BlockSpec((KV_W, tm), pos_col)] * 2
        args += list(rope_tabs)
    if emit_f32:
        in_specs.append(_resident((D_MODEL, KV_W)))
        args.append(wk)
    out_specs = [tok(Q_W), pl.BlockSpec((KV_W, tm), lambda i: (0, i)), tok(KV_W)]
    out_shape = [jax.ShapeDtypeStruct((t, Q_W), BF16), jax.ShapeDtypeStruct((KV_W, t), BF16),
                 jax.ShapeDtypeStruct((t, KV_W), BF16)]
    if emit_f32:
        out_specs += [tok(KV_W), tok(KV_W)]
        out_shape += [jax.ShapeDtypeStruct((t, KV_W), F32)] * 2
    vmem = 2 * D_MODEL * (Q_W + 3 * KV_W) + 2 * tm * (D_MODEL * 4 + Q_W * 2 + 8 * KV_W + 5 * LANES * 4) \
        + 8 * tm * D_MODEL * 4 + (4 << 20)
    return pl.pallas_call(
        functools.partial(_attn_in_kernel, rope=rope, emit_f32=emit_f32),
        grid=(t // tm,),
        in_specs=in_specs,
        out_specs=out_specs,
        out_shape=out_shape,
        compiler_params=_cparams(("parallel",), vmem),
        name="attn_in",
    )(*args)


def _softmax_pv(parts, sink):
    m = sink
    for s, _ in parts:
        m = jnp.maximum(m, jnp.max(s, axis=-1, keepdims=True))
    l = jnp.exp(sink - m)
    o = None
    for s, v in parts:
        e = jnp.exp(s - m)
        l = l + jnp.sum(e, axis=-1, keepdims=True)
        pv = _dot(e.astype(BF16), v)
        o = pv if o is None else o + pv
    return o / l


def _attn_latent_kernel(sink_ref, q_ref, ktp_ref, ktc_ref, ktn_ref, vp_ref, vc_ref, vn_ref, ktx_ref, vx_ref,
                        o_ref, *, n_tok):
    j = pl.program_id(1)
    blk = ATTN_BLOCK
    t_pos = j * blk + lax.broadcasted_iota(jnp.int32, (blk, 3 * blk), 0)
    s_pos = (j - 1) * blk + lax.broadcasted_iota(jnp.int32, (blk, 3 * blk), 1)
    local = (jnp.abs(t_pos - s_pos) <= WINDOW) & (s_pos >= 0) & (s_pos < n_tok)
    for kv in range(NKV_A):
        rows = slice(kv * HD_A, (kv + 1) * HD_A)
        kt_loc = jnp.concatenate([ktp_ref[rows, :], ktc_ref[rows, :], ktn_ref[rows, :]], axis=1)
        v_loc = jnp.concatenate([vp_ref[:, rows], vc_ref[:, rows], vn_ref[:, rows]], axis=0)
        kt_ctx = ktx_ref[0, rows, :]
        v_ctx = vx_ref[0, :, rows]
        for g in range(G_A):
            h = kv * G_A + g
            qh = q_ref[:, h * HD_A:(h + 1) * HD_A]
            s_loc = jnp.where(local, _dot(qh, kt_loc), NEG_INF)
            s_ctx = _dot(qh, kt_ctx)
            o = _softmax_pv([(s_loc, v_loc), (s_ctx, v_ctx)], sink_ref[0, h])
            o_ref[:, h * HD_A:(h + 1) * HD_A] = o.astype(BF16)


def _attn_latent_call(q, kt, v, ktx, vx, sink, bsz, seq_len):
    t = q.shape[0]
    nb = seq_len // ATTN_BLOCK
    n_ctx = ktx.shape[2]
    prev = lambda j: jnp.maximum(j - 1, 0)
    nxt = lambda j: jnp.minimum(j + 1, nb - 1)
    kt_spec = lambda f: pl.BlockSpec((KV_W, ATTN_BLOCK), lambda b, j: (0, b * nb + f(j)))
    v_spec = lambda f: pl.BlockSpec((ATTN_BLOCK, KV_W), lambda b, j: (b * nb + f(j), 0))
    same = lambda j: j
    return pl.pallas_call(
        functools.partial(_attn_latent_kernel, n_tok=seq_len),
        grid=(bsz, nb),
        in_specs=[
            pl.BlockSpec(memory_space=pltpu.SMEM),
            pl.BlockSpec((ATTN_BLOCK, Q_W), lambda b, j: (b * nb + j, 0)),
            kt_spec(prev), kt_spec(same), kt_spec(nxt),
            v_spec(prev), v_spec(same), v_spec(nxt),
            pl.BlockSpec((1, KV_W, n_ctx), lambda b, j: (b, 0, 0)),
            pl.BlockSpec((1, n_ctx, KV_W), lambda b, j: (b, 0, 0)),
        ],
        out_specs=pl.BlockSpec((ATTN_BLOCK, Q_W), lambda b, j: (b * nb + j, 0)),
        out_shape=jax.ShapeDtypeStruct((t, Q_W), BF16),
        compiler_params=_cparams(("parallel", "parallel"), 32 << 20),
        name="attn_latent",
    )(sink, q, kt, kt, kt, v, v, v, ktx, vx)


def _attn_context_kernel(sink_ref, q_ref, kt_ref, v_ref, o_ref):
    for kv in range(NKV_A):
        rows = slice(kv * HD_A, (kv + 1) * HD_A)
        kt_kv = kt_ref[rows, :]
        v_kv = v_ref[:, rows]
        for g in range(G_A):
            h = kv * G_A + g
            qh = q_ref[:, h * HD_A:(h + 1) * HD_A]
            o = _softmax_pv([(_dot(qh, kt_kv), v_kv)], sink_ref[0, h])
            o_ref[:, h * HD_A:(h + 1) * HD_A] = o.astype(BF16)


def _attn_context_call(q, kt, v, sink, bsz, seq_len):
    t = q.shape[0]
    nb = seq_len // ATTN_BLOCK
    return pl.pallas_call(
        _attn_context_kernel,
        grid=(bsz, nb),
        in_specs=[
            pl.BlockSpec(memory_space=pltpu.SMEM),
            pl.BlockSpec((ATTN_BLOCK, Q_W), lambda b, j: (b * nb + j, 0)),
            pl.BlockSpec((KV_W, seq_len), lambda b, j: (0, b)),
            pl.BlockSpec((seq_len, KV_W), lambda b, j: (b, 0)),
        ],
        out_specs=pl.BlockSpec((ATTN_BLOCK, Q_W), lambda b, j: (b * nb + j, 0)),
        out_shape=jax.ShapeDtypeStruct((t, Q_W), BF16),
        compiler_params=_cparams(("parallel", "parallel"), 32 << 20),
        name="attn_context",
    )(sink, q, kt, v)


def _attn_out_kernel(x_ref, mod_ref, a_ref, w_ref, y_ref):
    y_ref[...] = x_ref[...] + mod_ref[0][5:6] * _dot(a_ref[...], w_ref[...])


def _attn_out_call(x, mod, a, w, seq_len, tm):
    t = x.shape[0]
    tok = lambda width: pl.BlockSpec((tm, width), lambda i: (i, 0))
    vmem = 2 * Q_W * D_MODEL + 2 * tm * (2 * D_MODEL * 4 + Q_W * 2) + 4 * tm * D_MODEL * 4 + (4 << 20)
    return pl.pallas_call(
        _attn_out_kernel,
        grid=(t // tm,),
        in_specs=[tok(D_MODEL), _mod_spec(mod, seq_len // tm), tok(Q_W), _resident((Q_W, D_MODEL))],
        out_specs=tok(D_MODEL),
        out_shape=jax.ShapeDtypeStruct((t, D_MODEL), F32),
        compiler_params=_cparams(("parallel",), vmem),
        name="attn_out",
    )(x, mod, a, w)


def _rope_tables(n_tok):
    quarter = HD_A // 4
    freqs = ROPE_THETA ** (-jnp.arange(quarter, dtype=F32) / quarter)
    pos = jnp.arange(n_tok)
    row = (pos // GRID_W).astype(F32)
    col = (pos % GRID_W).astype(F32)
    ang_r, ang_c = row[:, None] * freqs, col[:, None] * freqs
    ang = jnp.concatenate([ang_r, ang_r, ang_c, ang_c], axis=-1)
    cos, sin = jnp.cos(ang), jnp.sin(ang)
    first = (jnp.arange(HD_A) % (2 * quarter)) < quarter
    sin_a = jnp.where(first, -sin, 0.0)
    sin_b = jnp.where(first, 0.0, sin)
    sin_t = jnp.where(first, -sin, sin)
    lane_tile = lambda a: jnp.tile(a, (1, LANES // HD_A))
    row_tile = lambda a: jnp.tile(a.T, (NKV_A, 1))
    return lane_tile(cos), lane_tile(sin_a), lane_tile(sin_b), row_tile(cos), row_tile(sin_t)


def _gate_layout(a):
    a4 = a.reshape(a.shape[:-1] + (2, 2, NH_M))
    gi = a4[..., :, 0, :].reshape(a.shape[:-1] + (2 * NH_M,))
    gf = a4[..., :, 1, :].reshape(a.shape[:-1] + (2 * NH_M,))
    pad = jnp.zeros(a.shape[:-1] + (LANES - 2 * NH_M,), a.dtype)
    return jnp.concatenate([gi, pad, gf, pad], axis=-1)


def _state_ext(state_c, state_n):
    n_rep = jnp.broadcast_to(state_n[..., None], state_n.shape + (DV_M,))
    return jnp.concatenate([state_c, n_rep], axis=-1).astype(F32)


def _m_row(state_m):
    m = state_m.reshape(state_m.shape[0], 1, 2 * NH_M).astype(F32)
    return jnp.pad(m, ((0, 0), (0, 0), (0, LANES - 2 * NH_M)))


def kernel(x_prompt, x_sample, state_c, state_n, state_m, cache_k, cache_v, c, c_ctx, w_mod, b_mod, norm_g,
           ffn1_w_gu, ffn1_w_down, ffn2_w_gu, ffn2_w_down, mlstm_w_in, mlstm_b_gate, mlstm_g_head, mlstm_w_out,
           attn_w_in, attn_sink, attn_w_out, final_g):
    bp, n_p, _ = x_prompt.shape
    bs, n_s, _ = x_sample.shape
    xp = x_prompt.reshape(bp * n_p, D_MODEL)
    xs = x_sample.reshape(bs * n_s, D_MODEL)
    tm_p, tm_s = 256, 512

    cond = jnp.concatenate([c_ctx[None, :], c], axis=0)
    cond = jnp.pad(cond, ((0, COND_ROWS - cond.shape[0]), (0, 0)))
    mod_all = _mod_call(cond, w_mod, b_mod).reshape(DEPTH, COND_ROWS, N_MOD, D_MODEL)

    outs = {}
    for l in range(DEPTH):
        mod_p = mod_all[l, 0:1]
        mod_s = mod_all[l, 1:1 + bs]
        g = norm_g[l]
        wgu1, wd1 = ffn1_w_gu[l].astype(BF16), ffn1_w_down[l].astype(BF16)
        wgu2, wd2 = ffn2_w_gu[l].astype(BF16), ffn2_w_down[l].astype(BF16)
        xp = _ffn_call(xp, mod_p, g[0:1], wgu1, wd1, 0, n_p, tm_p)
        xs = _ffn_call(xs, mod_s, g[0:1], wgu1, wd1, 0, n_s, tm_s)
        i = l // 2
        if l % 2 == 0:
            w_in = mlstm_w_in[i]
            wq = w_in[:, :QK_W].astype(BF16)
            wkt = w_in[:, QK_W:2 * QK_W].T.astype(BF16)
            wvo = w_in[:, 2 * QK_W:2 * QK_W + 2 * V_W].astype(BF16)
            wg = _gate_layout(w_in[:, 2 * QK_W + 2 * V_W:]).astype(BF16)
            bg = _gate_layout(mlstm_b_gate[i].astype(F32))[None, :]
            gh = mlstm_g_head[i].astype(F32)[None, :]
            w_out = mlstm_w_out[i].astype(BF16)
            zc = jnp.zeros((bp, 2, NH_M, DK_M, DV_M), F32)
            zn = jnp.zeros((bp, 2, NH_M, DK_M), F32)
            zm = jnp.zeros((bp, 2, NH_M), F32)
            streams = [("p", xp, mod_p, n_p, tm_p, bp, _state_ext(zc, zn), _m_row(zm)),
                       ("s", xs, mod_s, n_s, tm_s, bs, _state_ext(state_c[:, i], state_n[:, i]), _m_row(state_m[:, i]))]
            res = {}
            for tag, x, mod, n_tok, tm, bsz, c0, m0 in streams:
                q, kt, v, o, gt = _mlstm_in_call(x, mod, g[1:2], wq, wkt, wvo, wg, bg, n_tok, tm)
                hf, hb, c_fin, m_fin = _mlstm_core_call(q, kt, v, gt, c0, m0, bsz, n_tok)
                res[tag] = (_mlstm_out_call(x, mod, hf, hb, o, gh, w_out, n_tok, tm), c_fin, m_fin)
            xp, c_fin, m_fin = res["p"]
            xs = res["s"][0]
            dt = x_prompt.dtype
            outs.setdefault("c", []).append(c_fin[..., :DV_M].astype(dt))
            outs.setdefault("n", []).append(c_fin[..., DV_M].astype(dt))
            outs.setdefault("m", []).append(m_fin[:, 0, :2 * NH_M].reshape(bp, 2, NH_M).astype(dt))
        else:
            w_in = attn_w_in[i]
            wq = w_in[:, :Q_W].astype(BF16)
            wk = w_in[:, Q_W:Q_W + KV_W].astype(BF16)
            wkt = w_in[:, Q_W:Q_W + KV_W].T.astype(BF16)
            wv = w_in[:, Q_W + KV_W:].astype(BF16)
            w_out = attn_w_out[i].astype(BF16)
            sink = attn_sink[i].astype(F32)[None, :]
            q, kt, v, kf, vf = _attn_in_call(xp, mod_p, g[1:2], wq, wkt, wv, n_p, tm_p, wk=wk)
            ap = _attn_context_call(q, kt, v, sink, bp, n_p)
            xp = _attn_out_call(xp, mod_p, ap, w_out, n_p, tm_p)
            outs.setdefault("k", []).append(kf.reshape(bp, n_p, NKV_A, HD_A))
            outs.setdefault("v", []).append(vf.reshape(bp, n_p, NKV_A, HD_A))
            q, kt, v = _attn_in_call(xs, mod_s, g[1:2], wq, wkt, wv, n_s, tm_s, rope_tabs=_rope_tables(n_s))
            n_ctx = cache_k.shape[2]
            ktx = jnp.swapaxes(cache_k[:, i].reshape(bs, n_ctx, KV_W), 1, 2).astype(BF16)
            vx = cache_v[:, i].reshape(bs, n_ctx, KV_W).astype(BF16)
            a_s = _attn_latent_call(q, kt, v, ktx, vx, sink, bs, n_s)
            xs = _attn_out_call(xs, mod_s, a_s, w_out, n_s, tm_s)
        fg = final_g[None, :] if l == DEPTH - 1 else None
        xp = _ffn_call(xp, mod_p, g[2:3], wgu2, wd2, 2, n_p, tm_p, final_g=fg)
        xs = _ffn_call(xs, mod_s, g[2:3], wgu2, wd2, 2, n_s, tm_s, final_g=fg)

    return (xp.reshape(bp, n_p, D_MODEL), xs.reshape(bs, n_s, D_MODEL),
            jnp.stack(outs["c"], axis=1), jnp.stack(outs["n"], axis=1), jnp.stack(outs["m"], axis=1),
            jnp.stack(outs["k"], axis=1), jnp.stack(outs["v"], axis=1))
```

```python
import functools

import jax
import jax.numpy as jnp
from jax import lax
from jax.experimental import pallas as pl
from jax.experimental.pallas import tpu as pltpu

D_MODEL = 1024
DEPTH = 2
GRID_W = 64
D_FF = 2816
FFN_RES = 0.5
NH_M = 8
DK_M = 64
DV_M = 128
NH_A = 16
NKV_A = 4
G_A = NH_A // NKV_A
HD_A = 64
WINDOW = 128
ATTN_BLOCK = 128
ROPE_THETA = 10000.0
EPS = 1e-6
NEG_INF = -1e30
N_MOD = 9

QK_W = NH_M * DK_M
V_W = NH_M * DV_M
KV_W = NKV_A * HD_A
Q_W = NH_A * HD_A

LANES = 128
MLSTM_L = 128
FFN_CHUNK = 256
COND_ROWS = 16
VMEM_CAP = 60000 * 1024

F32 = jnp.float32
BF16 = jnp.bfloat16


def _cparams(sem, vmem_bytes):
    return pltpu.CompilerParams(dimension_semantics=sem, vmem_limit_bytes=min(int(vmem_bytes), VMEM_CAP))


def _dot(a, b):
    return jnp.dot(a, b, preferred_element_type=F32)


def _dot_nt(a, b):
    return lax.dot_general(a, b, (((1,), (1,)), ((), ())), preferred_element_type=F32)


def _rms(x):
    return x * lax.rsqrt(jnp.mean(x * x, axis=-1, keepdims=True) + EPS)


def _sigmoid(x):
    return 1.0 / (1.0 + jnp.exp(-x))


def _mod_norm(x, g, shift, scale):
    return (_rms(x) * g) * (1.0 + scale) + shift


def _mod_kernel(c_ref, w_ref, b_ref, o_ref):
    c = c_ref[...]
    s = (c * _sigmoid(c)).astype(BF16)
    o_ref[0] = _dot(s, w_ref[0].astype(BF16)) + b_ref[0]


def _mod_call(cond, w_mod, b_mod):
    tn = D_MODEL
    n_out = N_MOD * D_MODEL
    return pl.pallas_call(
        _mod_kernel,
        grid=(DEPTH, n_out // tn),
        in_specs=[
            pl.BlockSpec((COND_ROWS, D_MODEL), lambda l, n: (0, 0)),
            pl.BlockSpec((1, D_MODEL, tn), lambda l, n: (l, 0, n)),
            pl.BlockSpec((1, 1, tn), lambda l, n: (l, 0, n)),
        ],
        out_specs=pl.BlockSpec((1, COND_ROWS, tn), lambda l, n: (l, 0, n)),
        out_shape=jax.ShapeDtypeStruct((DEPTH, COND_ROWS, n_out), F32),
        compiler_params=_cparams(("parallel", "parallel"), 4 * (2 * D_MODEL * tn * 4)),
        name="adaln_mod",
    )(cond, w_mod, b_mod.reshape(DEPTH, 1, n_out))


def _mod_spec(mod, tiles_per_seq):
    if mod.shape[0] == 1:
        return pl.BlockSpec((1, N_MOD, D_MODEL), lambda i: (0, 0, 0))
    return pl.BlockSpec((1, N_MOD, D_MODEL), lambda i: (i // tiles_per_seq, 0, 0))


def _row_spec(width):
    return pl.BlockSpec((1, width), lambda i: (0, 0))


def _resident(shape):
    return pl.BlockSpec(shape, lambda i: (0,) * len(shape), pipeline_mode=pl.Buffered(1))


def _mlstm_gated_heads(hf_ref, hb_ref, o_ref, gh_ref):
    hs = hf_ref[...].astype(F32) + hb_ref[...].astype(F32)
    parts = [_rms(hs[:, h * DV_M:(h + 1) * DV_M]) for h in range(NH_M)]
    hn = jnp.concatenate(parts, axis=1) * gh_ref[...]
    return (hn * _sigmoid(o_ref[...].astype(F32))).astype(BF16)


def _ffn_kernel(*refs, j, final, mixer):
    x_ref, mod_ref, g_ref, wgu_ref, wd_ref, fg_ref = refs[:6]
    o_ref = refs[-1]
    x = x_ref[...]
    mod = mod_ref[0]
    if mixer == "mlstm":
        hf_ref, hb_ref, og_ref, gh_ref, wo_ref = refs[6:11]
        x = x + mod[5:6] * _dot(_mlstm_gated_heads(hf_ref, hb_ref, og_ref, gh_ref), wo_ref[...])
    elif mixer == "attn":
        a_ref, wo_ref = refs[6:8]
        x = x + mod[5:6] * _dot(a_ref[...], wo_ref[...])
    shift, scale, gate = mod[3 * j:3 * j + 1], mod[3 * j + 1:3 * j + 2], mod[3 * j + 2:3 * j + 3]
    hb = _mod_norm(x, g_ref[...], shift, scale).astype(BF16)
    acc = jnp.zeros(x.shape, F32)
    for c in range(D_FF // FFN_CHUNK):
        lo = c * FFN_CHUNK
        gg = _dot(hb, wgu_ref[:, lo:lo + FFN_CHUNK])
        uu = _dot(hb, wgu_ref[:, D_FF + lo:D_FF + lo + FFN_CHUNK])
        act = (gg * _sigmoid(gg) * uu).astype(BF16)
        acc = acc + _dot(act, wd_ref[lo:lo + FFN_CHUNK, :])
    y = x + (FFN_RES * gate) * acc
    if final:
        y = _rms(y) * fg_ref[...]
    o_ref[...] = y


def _ffn_call(x, mod, g, wgu, wd, j, seq_len, tm, final_g=None, mlstm=None, attn=None):
    t = x.shape[0]
    final = final_g is not None
    fg = final_g if final else g
    tok = lambda width: pl.BlockSpec((tm, width), lambda i: (i, 0))
    in_specs = [tok(D_MODEL), _mod_spec(mod, seq_len // tm), _row_spec(D_MODEL),
                _resident((D_MODEL, 2 * D_FF)), _resident((D_FF, D_MODEL)), _row_spec(D_MODEL)]
    args = [x, mod, g, wgu, wd, fg]
    mixer, extra = None, 0
    if mlstm is not None:
        mixer = "mlstm"
        in_specs += [tok(V_W), tok(V_W), tok(V_W), _row_spec(V_W), _resident((V_W, D_MODEL))]
        args += list(mlstm)
        extra = 2 * V_W * D_MODEL + 2 * tm * 3 * V_W * 2 + 3 * tm * V_W * 4
    elif attn is not None:
        mixer = "attn"
        in_specs += [tok(Q_W), _resident((Q_W, D_MODEL))]
        args += list(attn)
        extra = 2 * Q_W * D_MODEL + 2 * tm * Q_W * 2 + tm * Q_W * 4
    vmem = 2 * (D_MODEL * 2 * D_FF + D_FF * D_MODEL) + 12 * tm * D_MODEL * 4 + extra + (8 << 20)
    return pl.pallas_call(
        functools.partial(_ffn_kernel, j=j, final=final, mixer=mixer),
        grid=(t // tm,),
        in_specs=in_specs,
        out_specs=tok(D_MODEL),
        out_shape=jax.ShapeDtypeStruct((t, D_MODEL), F32),
        compiler_params=_cparams(("parallel",), vmem),
        name=f"ffn{j // 2 + 1}" + (f"_{mixer}" if mixer else ""),
    )(*args)


def _mlstm_in_kernel(x_ref, mod_ref, g_ref, wq_ref, wkt_ref, wvo_ref, wg_ref, bg_ref,
                     q_ref, kt_ref, v_ref, o_ref, gt_ref):
    mod = mod_ref[0]
    hb = _mod_norm(x_ref[...], g_ref[...], mod[3:4], mod[4:5]).astype(BF16)
    q_ref[...] = (_dot(hb, wq_ref[...]) * DK_M ** -0.5).astype(BF16)
    kt_ref[...] = _dot_nt(wkt_ref[...], hb).astype(BF16)
    vo = _dot(hb, wvo_ref[...])
    v_ref[...] = vo[:, :V_W].astype(BF16)
    o_ref[...] = vo[:, V_W:].astype(BF16)
    gt_ref[...] = _dot(hb, wg_ref[...]) + bg_ref[...]


def _mlstm_in_call(x, mod, g, wq, wkt, wvo, wg, bg, seq_len, tm):
    t = x.shape[0]
    w_bytes = 2 * (D_MODEL * (2 * QK_W + 2 * V_W + 2 * LANES))
    vmem = w_bytes + 2 * tm * (D_MODEL * 4 + (2 * QK_W + 2 * V_W) * 2 + 2 * LANES * 4) + 6 * tm * D_MODEL * 4 + (4 << 20)
    return pl.pallas_call(
        _mlstm_in_kernel,
        grid=(t // tm,),
        in_specs=[
            pl.BlockSpec((tm, D_MODEL), lambda i: (i, 0)),
            _mod_spec(mod, seq_len // tm),
            _row_spec(D_MODEL),
            _resident((D_MODEL, QK_W)),
            _resident((QK_W, D_MODEL)),
            _resident((D_MODEL, 2 * V_W)),
            _resident((D_MODEL, 2 * LANES)),
            _row_spec(2 * LANES),
        ],
        out_specs=[
            pl.BlockSpec((tm, QK_W), lambda i: (i, 0)),
            pl.BlockSpec((QK_W, tm), lambda i: (0, i)),
            pl.BlockSpec((tm, V_W), lambda i: (i, 0)),
            pl.BlockSpec((tm, V_W), lambda i: (i, 0)),
            pl.BlockSpec((tm, 2 * LANES), lambda i: (i, 0)),
        ],
        out_shape=[
            jax.ShapeDtypeStruct((t, QK_W), BF16),
            jax.ShapeDtypeStruct((QK_W, t), BF16),
            jax.ShapeDtypeStruct((t, V_W), BF16),
            jax.ShapeDtypeStruct((t, V_W), BF16),
            jax.ShapeDtypeStruct((t, 2 * LANES), F32),
        ],
        compiler_params=_cparams(("parallel",), vmem),
        name="mlstm_in",
    )(x, mod, g, wq, wkt, wvo, wg, bg)


def _scan_rows(x, op, fill, reverse):
    n = x.shape[0]
    idx = lax.broadcasted_iota(jnp.int32, x.shape, 0)
    s = 1
    while s < n:
        if reverse:
            x = op(x, jnp.where(idx < n - s, pltpu.roll(x, n - s, axis=0), fill))
        else:
            x = op(x, jnp.where(idx >= s, pltpu.roll(x, s, axis=0), fill))
        s *= 2
    return x


def _mlstm_dir(d, q_ref, kt_ref, v_ref, gt_ref, h_ref, c_sc, m_sc):
    n = MLSTM_L
    gt = gt_ref[...]
    li = gt[:, :LANES]
    fx = gt[:, LANES:]
    lf = jnp.minimum(fx, 0.0) - jnp.log(1.0 + jnp.exp(-jnp.abs(fx)))
    rev = d == 1
    b = _scan_rows(lf, jnp.add, 0.0, rev)
    a = li - b
    cm = _scan_rows(a, jnp.maximum, -jnp.inf, rev)
    m = m_sc[...]
    mt = jnp.maximum(cm, m)
    inter = jnp.exp(m - mt)
    clamp = jnp.exp(-(b + mt))
    end = 0 if rev else n - 1
    mx = jnp.maximum(m, cm[end:end + 1, :])
    keep = jnp.exp(m - mx)
    m_new = b[end:end + 1, :] + mx
    a_t = a.T
    wend_t = jnp.exp(a - mx).T
    row = lax.broadcasted_iota(jnp.int32, (n, n), 0)
    col = lax.broadcasted_iota(jnp.int32, (n, n), 1)
    mask = (col >= row) if rev else (col <= row)
    ones = jnp.ones((n, LANES), BF16)
    for h in range(NH_M):
        c = d * NH_M + h
        qh = q_ref[:, h * DK_M:(h + 1) * DK_M]
        kth = kt_ref[h * DK_M:(h + 1) * DK_M, :]
        vext = jnp.concatenate([v_ref[:, h * DV_M:(h + 1) * DV_M], ones], axis=1)
        s = _dot(qh, kth)
        e = jnp.exp(a_t[c:c + 1, :] - mt[:, c:c + 1])
        w = jnp.where(mask, s * e, 0.0).astype(BF16)
        cext = c_sc[d, h]
        qi = (inter[:, c:c + 1] * qh.astype(F32)).astype(BF16)
        r = _dot(jnp.concatenate([w, qi], axis=1), jnp.concatenate([vext, cext.astype(BF16)], axis=0))
        den = jnp.maximum(jnp.abs(r[:, DV_M:]), clamp[:, c:c + 1])
        h_ref[:, h * DV_M:(h + 1) * DV_M] = (r[:, :DV_M] / den).astype(BF16)
        kts = (kth.astype(F32) * wend_t[c:c + 1, :]).astype(BF16)
        c_sc[d, h] = keep[:, c:c + 1] * cext + _dot(kts, vext)
    lane = lax.broadcasted_iota(jnp.int32, m.shape, 1)
    m_sc[...] = jnp.where((lane >= d * NH_M) & (lane < (d + 1) * NH_M), m_new, m)


def _mlstm_kernel(qf_ref, ktf_ref, vf_ref, gtf_ref, qb_ref, ktb_ref, vb_ref, gtb_ref, c0_ref, m0_ref,
                  hf_ref, hb_ref, cout_ref, mout_ref, c_sc, m_sc):
    i = pl.program_id(1)

    @pl.when(i == 0)
    def _():
        c_sc[...] = c0_ref[0]
        m_sc[...] = m0_ref[0]

    _mlstm_dir(0, qf_ref, ktf_ref, vf_ref, gtf_ref, hf_ref, c_sc, m_sc)
    _mlstm_dir(1, qb_ref, ktb_ref, vb_ref, gtb_ref, hb_ref, c_sc, m_sc)

    @pl.when(i == pl.num_programs(1) - 1)
    def _():
        cout_ref[0] = c_sc[...]
        mout_ref[0] = m_sc[...]


def _mlstm_core_call(q, kt, v, gt, c0, m0, bsz, seq_len):
    t = q.shape[0]
    nc = seq_len // MLSTM_L
    fwd = lambda b, i: (b * nc + i, 0)
    bwd = lambda b, i: (b * nc + nc - 1 - i, 0)
    fwd_t = lambda b, i: (0, b * nc + i)
    bwd_t = lambda b, i: (0, b * nc + nc - 1 - i)

    def specs(row_map, col_map):
        return [
            pl.BlockSpec((MLSTM_L, QK_W), row_map),
            pl.BlockSpec((QK_W, MLSTM_L), col_map),
            pl.BlockSpec((MLSTM_L, V_W), row_map),
            pl.BlockSpec((MLSTM_L, 2 * LANES), row_map),
        ]

    state_shape = (2, NH_M, DK_M, 2 * DV_M)
    state_bytes = 2 * NH_M * DK_M * 2 * DV_M * 4
    vmem = 5 * state_bytes + 8 * MLSTM_L * (2 * QK_W + 2 * V_W) * 2 + (16 << 20)
    return pl.pallas_call(
        _mlstm_kernel,
        grid=(bsz, nc),
        in_specs=specs(fwd, fwd_t) + specs(bwd, bwd_t) + [
            pl.BlockSpec((1,) + state_shape, lambda b, i: (b, 0, 0, 0, 0)),
            pl.BlockSpec((1, 1, LANES), lambda b, i: (b, 0, 0)),
        ],
        out_specs=[
            pl.BlockSpec((MLSTM_L, V_W), fwd),
            pl.BlockSpec((MLSTM_L, V_W), bwd),
            pl.BlockSpec((1,) + state_shape, lambda b, i: (b, 0, 0, 0, 0)),
            pl.BlockSpec((1, 1, LANES), lambda b, i: (b, 0, 0)),
        ],
        out_shape=[
            jax.ShapeDtypeStruct((t, V_W), BF16),
            jax.ShapeDtypeStruct((t, V_W), BF16),
            jax.ShapeDtypeStruct((bsz,) + state_shape, F32),
            jax.ShapeDtypeStruct((bsz, 1, LANES), F32),
        ],
        scratch_shapes=[pltpu.VMEM(state_shape, F32), pltpu.VMEM((1, LANES), F32)],
        compiler_params=_cparams(("parallel", "arbitrary"), vmem),
        name="mlstm_core",
    )(q, kt, v, gt, q, kt, v, gt, c0, m0)


def _rope_rows_swap(x):
    q = HD_A // 4
    parts = []
    for base in range(0, x.shape[0], 2 * q):
        parts += [x[base + q:base + 2 * q], x[base:base + q]]
    return jnp.concatenate(parts, axis=0)


def _attn_in_kernel(*refs, rope, emit_f32):
    x_ref, mod_ref, g_ref, wq_ref, wkt_ref, wv_ref = refs[:6]
    pos = 6
    if rope:
        cos_ref, sina_ref, sinb_ref, cost_ref, sint_ref = refs[pos:pos + 5]
        pos += 5
    if emit_f32:
        wk_ref = refs[pos]
        pos += 1
    q_ref, kt_ref, v_ref = refs[pos:pos + 3]
    pos += 3
    mod = mod_ref[0]
    hb = _mod_norm(x_ref[...], g_ref[...], mod[3:4], mod[4:5]).astype(BF16)
    q = _dot(hb, wq_ref[...])
    kt = _dot_nt(wkt_ref[...], hb)
    v = _dot(hb, wv_ref[...])
    if rope:
        cos, sina, sinb = cos_ref[...], sina_ref[...], sinb_ref[...]
        kt = kt * cost_ref[...] + _rope_rows_swap(kt) * sint_ref[...]
    blk = ATTN_BLOCK
    for gi in range(Q_W // LANES):
        qg = q[:, gi * LANES:(gi + 1) * LANES]
        if rope:
            qg = qg * cos + pltpu.roll(qg, LANES - HD_A // 4, axis=1) * sina + pltpu.roll(qg, HD_A // 4, axis=1) * sinb
        qg = (qg * HD_A ** -0.5).astype(BF16)
        p, g = gi // G_A, gi % G_A
        for tb in range(q.shape[0] // blk):
            q_ref[tb, p, g * blk:(g + 1) * blk, :] = qg[tb * blk:(tb + 1) * blk]
    kt_ref[...] = kt.astype(BF16)
    v_ref[...] = v.astype(BF16)
    if emit_f32:
        kf_ref, vf_ref = refs[pos:pos + 2]
        kf_ref[...] = _dot(hb, wk_ref[...])
        vf_ref[...] = v


def _attn_in_call(x, mod, g, wq, wkt, wv, seq_len, tm, rope_tabs=None, wk=None):
    t = x.shape[0]
    rope = rope_tabs is not None
    emit_f32 = wk is not None
    tiles = seq_len // tm
    tok = lambda width: pl.BlockSpec((tm, width), lambda i: (i, 0))
    in_specs = [tok(D_MODEL), _mod_spec(mod, tiles), _row_spec(D_MODEL),
                _resident((D_MODEL, Q_W)), _resident((KV_W, D_MODEL)), _resident((D_MODEL, KV_W))]
    args = [x, mod, g, wq, wkt, wv]
    if rope:
        pos_row = lambda i: (i % tiles, 0)
        pos_col = lambda i: (0, i % tiles)
        in_specs += [pl.BlockSpec((tm, LANES), pos_row)] * 3 + [pl.BlockSpec((KV_W, tm), pos_col)] * 2
        args += list(rope_tabs)
    if emit_f32:
        in_specs.append(_resident((D_MODEL, KV_W)))
        args.append(wk)
    q_blocks = (tm // ATTN_BLOCK, NKV_A // 2, G_A * ATTN_BLOCK, LANES)
    out_specs = [pl.BlockSpec(q_blocks, lambda i: (i, 0, 0, 0)), pl.BlockSpec((KV_W, tm), lambda i: (0, i)), tok(KV_W)]
    out_shape = [jax.ShapeDtypeStruct((t // ATTN_BLOCK,) + q_blocks[1:], BF16), jax.ShapeDtypeStruct((KV_W, t), BF16),
                 jax.ShapeDtypeStruct((t, KV_W), BF16)]
    if emit_f32:
        out_specs += [tok(KV_W), tok(KV_W)]
        out_shape += [jax.ShapeDtypeStruct((t, KV_W), F32)] * 2
    vmem = 2 * D_MODEL * (Q_W + 3 * KV_W) + 2 * tm * (D_MODEL * 4 + Q_W * 2 + 8 * KV_W + 5 * LANES * 4) \
        + 8 * tm * D_MODEL * 4 + (4 << 20)
    return pl.pallas_call(
        functools.partial(_attn_in_kernel, rope=rope, emit_f32=emit_f32),
        grid=(t // tm,),
        in_specs=in_specs,
        out_specs=out_specs,
        out_shape=out_shape,
        compiler_params=_cparams(("parallel",), vmem),
        name="attn_in",
    )(*args)


def _softmax_pv(parts, sink):
    m = sink
    for s, _ in parts:
        m = jnp.maximum(m, jnp.max(s, axis=-1, keepdims=True))
    l = jnp.exp(sink - m)
    o = None
    for s, v in parts:
        e = jnp.exp(s - m)
        l = l + jnp.sum(e, axis=-1, keepdims=True)
        pv = _dot(e.astype(BF16), v)
        o = pv if o is None else o + pv
    return o / l


def _pair_attention(sink_ref, q_ref, o_ref, p, keys, mask_of):
    blk = ATTN_BLOCK
    qp = q_ref[0, p]
    lane = lax.broadcasted_iota(jnp.int32, qp.shape, 1)
    row_g = lax.broadcasted_iota(jnp.int32, (G_A * blk, 1), 0) // blk
    outs = []
    for e in range(2):
        kv = 2 * p + e
        qm = jnp.where((lane >= e * HD_A) & (lane < (e + 1) * HD_A), qp, jnp.zeros_like(qp))
        sink = jnp.full((G_A * blk, 1), sink_ref[0, kv * G_A], F32)
        for g in range(1, G_A):
            sink = jnp.where(row_g == g, sink_ref[0, kv * G_A + g], sink)
        parts = []
        for i, (kt, v) in enumerate(keys):
            s = _dot(qm, kt)
            msk = mask_of(i)
            parts.append((s if msk is None else jnp.where(msk, s, NEG_INF), v))
        outs.append(_softmax_pv(parts, sink))
    lane_o = lax.broadcasted_iota(jnp.int32, outs[0].shape, 1)
    a = jnp.where(lane_o < HD_A, outs[0], outs[1]).astype(BF16)
    for g in range(G_A):
        c0 = (p * G_A + g) * LANES
        o_ref[:, c0:c0 + LANES] = a[g * blk:(g + 1) * blk]


def _attn_latent_kernel(sink_ref, q_ref, ktp_ref, ktc_ref, ktn_ref, vp_ref, vc_ref, vn_ref, ktx_ref, vx_ref,
                        o_ref, *, n_tok):
    j = pl.program_id(1)
    blk = ATTN_BLOCK
    shape = (G_A * blk, 3 * blk)
    t_pos = j * blk + lax.broadcasted_iota(jnp.int32, shape, 0) % blk
    s_pos = (j - 1) * blk + lax.broadcasted_iota(jnp.int32, shape, 1)
    local = (jnp.abs(t_pos - s_pos) <= WINDOW) & (s_pos >= 0) & (s_pos < n_tok)
    for p in range(NKV_A // 2):
        rows = slice(p * LANES, (p + 1) * LANES)
        kt_loc = jnp.concatenate([ktp_ref[rows, :], ktc_ref[rows, :], ktn_ref[rows, :]], axis=1)
        v_loc = jnp.concatenate([vp_ref[:, rows], vc_ref[:, rows], vn_ref[:, rows]], axis=0)
        keys = [(kt_loc, v_loc), (ktx_ref[0, rows, :], vx_ref[0, :, rows])]
        _pair_attention(sink_ref, q_ref, o_ref, p, keys, lambda i: local if i == 0 else None)


def _q_pair_spec(nb):
    return pl.BlockSpec((1, NKV_A // 2, G_A * ATTN_BLOCK, LANES), lambda b, j: (b * nb + j, 0, 0, 0))


def _attn_latent_call(q, kt, v, ktx, vx, sink, bsz, seq_len):
    t = v.shape[0]
    nb = seq_len // ATTN_BLOCK
    n_ctx = ktx.shape[2]
    prev = lambda j: jnp.maximum(j - 1, 0)
    nxt = lambda j: jnp.minimum(j + 1, nb - 1)
    kt_spec = lambda f: pl.BlockSpec((KV_W, ATTN_BLOCK), lambda b, j: (0, b * nb + f(j)))
    v_spec = lambda f: pl.BlockSpec((ATTN_BLOCK, KV_W), lambda b, j: (b * nb + f(j), 0))
    same = lambda j: j
    return pl.pallas_call(
        functools.partial(_attn_latent_kernel, n_tok=seq_len),
        grid=(bsz, nb),
        in_specs=[
            pl.BlockSpec(memory_space=pltpu.SMEM),
            _q_pair_spec(nb),
            kt_spec(prev), kt_spec(same), kt_spec(nxt),
            v_spec(prev), v_spec(same), v_spec(nxt),
            pl.BlockSpec((1, KV_W, n_ctx), lambda b, j: (b, 0, 0)),
            pl.BlockSpec((1, n_ctx, KV_W), lambda b, j: (b, 0, 0)),
        ],
        out_specs=pl.BlockSpec((ATTN_BLOCK, Q_W), lambda b, j: (b * nb + j, 0)),
        out_shape=jax.ShapeDtypeStruct((t, Q_W), BF16),
        compiler_params=_cparams(("parallel", "parallel"), 32 << 20),
        name="attn_latent",
    )(sink, q, kt, kt, kt, v, v, v, ktx, vx)


def _attn_context_kernel(sink_ref, q_ref, kt_ref, v_ref, o_ref):
    for p in range(NKV_A // 2):
        rows = slice(p * LANES, (p + 1) * LANES)
        _pair_attention(sink_ref, q_ref, o_ref, p, [(kt_ref[rows, :], v_ref[:, rows])], lambda i: None)


def _attn_context_call(q, kt, v, sink, bsz, seq_len):
    t = v.shape[0]
    nb = seq_len // ATTN_BLOCK
    return pl.pallas_call(
        _attn_context_kernel,
        grid=(bsz, nb),
        in_specs=[
            pl.BlockSpec(memory_space=pltpu.SMEM),
            _q_pair_spec(nb),
            pl.BlockSpec((KV_W, seq_len), lambda b, j: (0, b)),
            pl.BlockSpec((seq_len, KV_W), lambda b, j: (b, 0)),
        ],
        out_specs=pl.BlockSpec((ATTN_BLOCK, Q_W), lambda b, j: (b * nb + j, 0)),
        out_shape=jax.ShapeDtypeStruct((t, Q_W), BF16),
        compiler_params=_cparams(("parallel", "parallel"), 32 << 20),
        name="attn_context",
    )(sink, q, kt, v)


def _rope_tables(n_tok):
    quarter = HD_A // 4
    freqs = ROPE_THETA ** (-jnp.arange(quarter, dtype=F32) / quarter)
    pos = jnp.arange(n_tok)
    row = (pos // GRID_W).astype(F32)
    col = (pos % GRID_W).astype(F32)
    ang_r, ang_c = row[:, None] * freqs, col[:, None] * freqs
    ang = jnp.concatenate([ang_r, ang_r, ang_c, ang_c], axis=-1)
    cos, sin = jnp.cos(ang), jnp.sin(ang)
    first = (jnp.arange(HD_A) % (2 * quarter)) < quarter
    sin_a = jnp.where(first, -sin, 0.0)
    sin_b = jnp.where(first, 0.0, sin)
    sin_t = jnp.where(first, -sin, sin)
    lane_tile = lambda a: jnp.tile(a, (1, LANES // HD_A))
    row_tile = lambda a: jnp.tile(a.T, (NKV_A, 1))
    return lane_tile(cos), lane_tile(sin_a), lane_tile(sin_b), row_tile(cos), row_tile(sin_t)


def _gate_layout(a):
    a4 = a.reshape(a.shape[:-1] + (2, 2, NH_M))
    gi = a4[..., :, 0, :].reshape(a.shape[:-1] + (2 * NH_M,))
    gf = a4[..., :, 1, :].reshape(a.shape[:-1] + (2 * NH_M,))
    pad = jnp.zeros(a.shape[:-1] + (LANES - 2 * NH_M,), a.dtype)
    return jnp.concatenate([gi, pad, gf, pad], axis=-1)


def _pair_layout_cols(w):
    d_in = w.shape[0]
    w5 = w.reshape(d_in, NKV_A // 2, 2, G_A, HD_A)
    return jnp.transpose(w5, (0, 1, 3, 2, 4)).reshape(d_in, Q_W)


def _state_ext(state_c, state_n):
    n_rep = jnp.broadcast_to(state_n[..., None], state_n.shape + (DV_M,))
    return jnp.concatenate([state_c, n_rep], axis=-1).astype(F32)


def _m_row(state_m):
    m = state_m.reshape(state_m.shape[0], 1, 2 * NH_M).astype(F32)
    return jnp.pad(m, ((0, 0), (0, 0), (0, LANES - 2 * NH_M)))


def kernel(x_prompt, x_sample, state_c, state_n, state_m, cache_k, cache_v, c, c_ctx, w_mod, b_mod, norm_g,
           ffn1_w_gu, ffn1_w_down, ffn2_w_gu, ffn2_w_down, mlstm_w_in, mlstm_b_gate, mlstm_g_head, mlstm_w_out,
           attn_w_in, attn_sink, attn_w_out, final_g):
    bp, n_p, _ = x_prompt.shape
    bs, n_s, _ = x_sample.shape
    xp = x_prompt.reshape(bp * n_p, D_MODEL)
    xs = x_sample.reshape(bs * n_s, D_MODEL)
    tm_p, tm_s = 512, 512

    cond = jnp.concatenate([c_ctx[None, :], c], axis=0)
    cond = jnp.pad(cond, ((0, COND_ROWS - cond.shape[0]), (0, 0)))
    mod_all = _mod_call(cond, w_mod, b_mod).reshape(DEPTH, COND_ROWS, N_MOD, D_MODEL)

    outs = {}
    for l in range(DEPTH):
        mod_p = mod_all[l, 0:1]
        mod_s = mod_all[l, 1:1 + bs]
        g = norm_g[l]
        wgu1, wd1 = ffn1_w_gu[l].astype(BF16), ffn1_w_down[l].astype(BF16)
        wgu2, wd2 = ffn2_w_gu[l].astype(BF16), ffn2_w_down[l].astype(BF16)
        xp = _ffn_call(xp, mod_p, g[0:1], wgu1, wd1, 0, n_p, tm_p)
        xs = _ffn_call(xs, mod_s, g[0:1], wgu1, wd1, 0, n_s, tm_s)
        i = l // 2
        if l % 2 == 0:
            w_in = mlstm_w_in[i]
            wq = w_in[:, :QK_W].astype(BF16)
            wkt = w_in[:, QK_W:2 * QK_W].T.astype(BF16)
            wvo = w_in[:, 2 * QK_W:2 * QK_W + 2 * V_W].astype(BF16)
            wg = _gate_layout(w_in[:, 2 * QK_W + 2 * V_W:]).astype(BF16)
            bg = _gate_layout(mlstm_b_gate[i].astype(F32))[None, :]
            gh = mlstm_g_head[i].astype(F32)[None, :]
            w_out = mlstm_w_out[i].astype(BF16)
            zc = jnp.zeros((bp, 2, NH_M, DK_M, DV_M), F32)
            zn = jnp.zeros((bp, 2, NH_M, DK_M), F32)
            zm = jnp.zeros((bp, 2, NH_M), F32)
            streams = [("p", xp, mod_p, n_p, tm_p, bp, _state_ext(zc, zn), _m_row(zm)),
                       ("s", xs, mod_s, n_s, tm_s, bs, _state_ext(state_c[:, i], state_n[:, i]), _m_row(state_m[:, i]))]
            res = {}
            for tag, x, mod, n_tok, tm, bsz, c0, m0 in streams:
                q, kt, v, o, gt = _mlstm_in_call(x, mod, g[1:2], wq, wkt, wvo, wg, bg, n_tok, tm)
                hf, hb, c_fin, m_fin = _mlstm_core_call(q, kt, v, gt, c0, m0, bsz, n_tok)
                res[tag] = ((hf, hb, o, gh, w_out), c_fin, m_fin)
            mix_p, c_fin, m_fin = res["p"]
            mix_s = res["s"][0]
            mix_key = "mlstm"
            dt = x_prompt.dtype
            outs.setdefault("c", []).append(c_fin[..., :DV_M].astype(dt))
            outs.setdefault("n", []).append(c_fin[..., DV_M].astype(dt))
            outs.setdefault("m", []).append(m_fin[:, 0, :2 * NH_M].reshape(bp, 2, NH_M).astype(dt))
        else:
            w_in = attn_w_in[i]
            wq = _pair_layout_cols(w_in[:, :Q_W]).astype(BF16)
            wk = w_in[:, Q_W:Q_W + KV_W].astype(BF16)
            wkt = w_in[:, Q_W:Q_W + KV_W].T.astype(BF16)
            wv = w_in[:, Q_W + KV_W:].astype(BF16)
            w_out = _pair_layout_cols(attn_w_out[i].T).T.astype(BF16)
            sink = attn_sink[i].astype(F32)[None, :]
            q, kt, v, kf, vf = _attn_in_call(xp, mod_p, g[1:2], wq, wkt, wv, n_p, tm_p, wk=wk)
            mix_p = (_attn_context_call(q, kt, v, sink, bp, n_p), w_out)
            mix_key = "attn"
            outs.setdefault("k", []).append(kf.reshape(bp, n_p, NKV_A, HD_A))
            outs.setdefault("v", []).append(vf.reshape(bp, n_p, NKV_A, HD_A))
            q, kt, v = _attn_in_call(xs, mod_s, g[1:2], wq, wkt, wv, n_s, tm_s, rope_tabs=_rope_tables(n_s))
            n_ctx = cache_k.shape[2]
            ktx = jnp.swapaxes(cache_k[:, i].reshape(bs, n_ctx, KV_W), 1, 2).astype(BF16)
            vx = cache_v[:, i].reshape(bs, n_ctx, KV_W).astype(BF16)
            mix_s = (_attn_latent_call(q, kt, v, ktx, vx, sink, bs, n_s), w_out)
        fg = final_g[None, :] if l == DEPTH - 1 else None
        xp = _ffn_call(xp, mod_p, g[2:3], wgu2, wd2, 2, n_p, tm_p, final_g=fg, **{mix_key: mix_p})
        xs = _ffn_call(xs, mod_s, g[2:3], wgu2, wd2, 2, n_s, tm_s, final_g=fg, **{mix_key: mix_s})

    return (xp.reshape(bp, n_p, D_MODEL), xs.reshape(bs, n_s, D_MODEL),
            jnp.stack(outs["c"], axis=1), jnp.stack(outs["n"], axis=1), jnp.stack(outs["m"], axis=1),
            jnp.stack(outs["k"], axis=1), jnp.stack(outs["v"], axis=1))
```

```python
import functools

import jax
import jax.numpy as jnp
from jax import lax
from jax.experimental import pallas as pl
from jax.experimental.pallas import tpu as pltpu

D_MODEL = 1024
DEPTH = 2
GRID_W = 64
D_FF = 2816
FFN_RES = 0.5
NH_M = 8
DK_M = 64
DV_M = 128
NH_A = 16
NKV_A = 4
G_A = NH_A // NKV_A
HD_A = 64
WINDOW = 128
ATTN_BLOCK = 128
ROPE_THETA = 10000.0
EPS = 1e-6
NEG_INF = -1e30
N_MOD = 9

QK_W = NH_M * DK_M
V_W = NH_M * DV_M
KV_W = NKV_A * HD_A
Q_W = NH_A * HD_A

LANES = 128
MLSTM_L = 128
FFN_CHUNK = 256
LOG2_E = 1.4426950408889634
GATE_COPIES = 5
GATE_DIR_LANES = GATE_COPIES * NH_M
COND_ROWS = 16
VMEM_CAP = 60000 * 1024

F32 = jnp.float32
BF16 = jnp.bfloat16


def _cparams(sem, vmem_bytes):
    return pltpu.CompilerParams(dimension_semantics=sem, vmem_limit_bytes=min(int(vmem_bytes), VMEM_CAP))


def _dot(a, b):
    return jnp.dot(a, b, preferred_element_type=F32)


def _dot_nt(a, b):
    return lax.dot_general(a, b, (((1,), (1,)), ((), ())), preferred_element_type=F32)


def _rms(x):
    return x * lax.rsqrt(jnp.mean(x * x, axis=-1, keepdims=True) + EPS)


def _sigmoid(x):
    return 1.0 / (1.0 + jnp.exp(-x))


def _mod_norm(x, g, shift, scale):
    return (_rms(x) * g) * (1.0 + scale) + shift


def _mod_kernel(c_ref, w_ref, b_ref, o_ref):
    c = c_ref[...]
    s = (c * _sigmoid(c)).astype(BF16)
    o_ref[0] = _dot(s, w_ref[0].astype(BF16)) + b_ref[0]


def _mod_call(cond, w_mod, b_mod):
    tn = D_MODEL
    n_out = N_MOD * D_MODEL
    return pl.pallas_call(
        _mod_kernel,
        grid=(DEPTH, n_out // tn),
        in_specs=[
            pl.BlockSpec((COND_ROWS, D_MODEL), lambda l, n: (0, 0)),
            pl.BlockSpec((1, D_MODEL, tn), lambda l, n: (l, 0, n)),
            pl.BlockSpec((1, 1, tn), lambda l, n: (l, 0, n)),
        ],
        out_specs=pl.BlockSpec((1, COND_ROWS, tn), lambda l, n: (l, 0, n)),
        out_shape=jax.ShapeDtypeStruct((DEPTH, COND_ROWS, n_out), F32),
        compiler_params=_cparams(("parallel", "parallel"), 4 * (2 * D_MODEL * tn * 4)),
        name="adaln_mod",
    )(cond, w_mod, b_mod.reshape(DEPTH, 1, n_out))


def _mod_spec(mod, tiles_per_seq):
    if mod.shape[0] == 1:
        return pl.BlockSpec((1, N_MOD, D_MODEL), lambda i: (0, 0, 0))
    return pl.BlockSpec((1, N_MOD, D_MODEL), lambda i: (i // tiles_per_seq, 0, 0))


def _row_spec(width):
    return pl.BlockSpec((1, width), lambda i: (0, 0))


def _resident(shape):
    return pl.BlockSpec(shape, lambda i: (0,) * len(shape), pipeline_mode=pl.Buffered(1))


def _mlstm_gated_heads(hf_ref, hb_ref, o_ref, gh_ref):
    hs = hf_ref[...].astype(F32) + hb_ref[...].astype(F32)
    parts = [_rms(hs[:, h * DV_M:(h + 1) * DV_M]) for h in range(NH_M)]
    hn = jnp.concatenate(parts, axis=1) * gh_ref[...]
    return (hn * _sigmoid(o_ref[...].astype(F32))).astype(BF16)


def _ffn_kernel(*refs, j, final, mixer):
    x_ref, mod_ref, g_ref, wgu_ref, wd_ref, fg_ref = refs[:6]
    o_ref = refs[-1]
    x = x_ref[...]
    mod = mod_ref[0]
    if mixer == "mlstm":
        hf_ref, hb_ref, og_ref, gh_ref, wo_ref = refs[6:11]
        x = x + mod[5:6] * _dot(_mlstm_gated_heads(hf_ref, hb_ref, og_ref, gh_ref), wo_ref[...])
    elif mixer == "attn":
        a_ref, wo_ref = refs[6:8]
        x = x + mod[5:6] * _dot(a_ref[...], wo_ref[...])
    shift, scale, gate = mod[3 * j:3 * j + 1], mod[3 * j + 1:3 * j + 2], mod[3 * j + 2:3 * j + 3]
    hb = _mod_norm(x, g_ref[...], shift, scale).astype(BF16)
    acc = jnp.zeros(x.shape, F32)
    for c in range(D_FF // FFN_CHUNK):
        lo = c * FFN_CHUNK
        gg = _dot(hb, wgu_ref[:, lo:lo + FFN_CHUNK])
        uu = _dot(hb, wgu_ref[:, D_FF + lo:D_FF + lo + FFN_CHUNK])
        act = (gg * _sigmoid(gg) * uu).astype(BF16)
        acc = acc + _dot(act, wd_ref[lo:lo + FFN_CHUNK, :])
    y = x + (FFN_RES * gate) * acc
    if final:
        y = _rms(y) * fg_ref[...]
    o_ref[...] = y


def _ffn_call(x, mod, g, wgu, wd, j, seq_len, tm, final_g=None, mlstm=None, attn=None):
    t = x.shape[0]
    final = final_g is not None
    fg = final_g if final else g
    tok = lambda width: pl.BlockSpec((tm, width), lambda i: (i, 0))
    in_specs = [tok(D_MODEL), _mod_spec(mod, seq_len // tm), _row_spec(D_MODEL),
                _resident((D_MODEL, 2 * D_FF)), _resident((D_FF, D_MODEL)), _row_spec(D_MODEL)]
    args = [x, mod, g, wgu, wd, fg]
    mixer, extra = None, 0
    if mlstm is not None:
        mixer = "mlstm"
        in_specs += [tok(V_W), tok(V_W), tok(V_W), _row_spec(V_W), _resident((V_W, D_MODEL))]
        args += list(mlstm)
        extra = 2 * V_W * D_MODEL + 2 * tm * 3 * V_W * 2 + 3 * tm * V_W * 4
    elif attn is not None:
        mixer = "attn"
        in_specs += [tok(Q_W), _resident((Q_W, D_MODEL))]
        args += list(attn)
        extra = 2 * Q_W * D_MODEL + 2 * tm * Q_W * 2 + tm * Q_W * 4
    vmem = 2 * (D_MODEL * 2 * D_FF + D_FF * D_MODEL) + 12 * tm * D_MODEL * 4 + extra + (8 << 20)
    return pl.pallas_call(
        functools.partial(_ffn_kernel, j=j, final=final, mixer=mixer),
        grid=(t // tm,),
        in_specs=in_specs,
        out_specs=tok(D_MODEL),
        out_shape=jax.ShapeDtypeStruct((t, D_MODEL), F32),
        compiler_params=_cparams(("parallel",), vmem),
        name=f"ffn{j // 2 + 1}" + (f"_{mixer}" if mixer else ""),
    )(*args)


def _mlstm_in_kernel(x_ref, mod_ref, g_ref, wq_ref, wkt_ref, wvo_ref, wg_ref, bg_ref,
                     q_ref, kt_ref, v_ref, o_ref, gt_ref):
    mod = mod_ref[0]
    hb = _mod_norm(x_ref[...], g_ref[...], mod[3:4], mod[4:5]).astype(BF16)
    q_ref[...] = (_dot(hb, wq_ref[...]) * DK_M ** -0.5).astype(BF16)
    kt_ref[...] = _dot_nt(wkt_ref[...], hb).astype(BF16)
    vo = _dot(hb, wvo_ref[...])
    v_ref[...] = vo[:, :V_W].astype(BF16)
    o_ref[...] = vo[:, V_W:].astype(BF16)
    gt_ref[...] = _dot(hb, wg_ref[...]) + bg_ref[...]


def _mlstm_in_call(x, mod, g, wq, wkt, wvo, wg, bg, seq_len, tm):
    t = x.shape[0]
    w_bytes = 2 * (D_MODEL * (2 * QK_W + 2 * V_W + 2 * LANES))
    vmem = w_bytes + 2 * tm * (D_MODEL * 4 + (2 * QK_W + 2 * V_W) * 2 + 2 * LANES * 4) + 6 * tm * D_MODEL * 4 + (4 << 20)
    return pl.pallas_call(
        _mlstm_in_kernel,
        grid=(t // tm,),
        in_specs=[
            pl.BlockSpec((tm, D_MODEL), lambda i: (i, 0)),
            _mod_spec(mod, seq_len // tm),
            _row_spec(D_MODEL),
            _resident((D_MODEL, QK_W)),
            _resident((QK_W, D_MODEL)),
            _resident((D_MODEL, 2 * V_W)),
            _resident((D_MODEL, 2 * LANES)),
            _row_spec(2 * LANES),
        ],
        out_specs=[
            pl.BlockSpec((tm, QK_W), lambda i: (i, 0)),
            pl.BlockSpec((QK_W, tm), lambda i: (0, i)),
            pl.BlockSpec((tm, V_W), lambda i: (i, 0)),
            pl.BlockSpec((tm, V_W), lambda i: (i, 0)),
            pl.BlockSpec((tm, 2 * LANES), lambda i: (i, 0)),
        ],
        out_shape=[
            jax.ShapeDtypeStruct((t, QK_W), BF16),
            jax.ShapeDtypeStruct((QK_W, t), BF16),
            jax.ShapeDtypeStruct((t, V_W), BF16),
            jax.ShapeDtypeStruct((t, V_W), BF16),
            jax.ShapeDtypeStruct((t, 2 * LANES), F32),
        ],
        compiler_params=_cparams(("parallel",), vmem),
        name="mlstm_in",
    )(x, mod, g, wq, wkt, wvo, wg, bg)


def _scan_rows(x, op, fill, reverse):
    n = x.shape[0]
    idx = lax.broadcasted_iota(jnp.int32, x.shape, 0)
    s = 1
    while s < n:
        if reverse:
            x = op(x, jnp.where(idx < n - s, pltpu.roll(x, n - s, axis=0), fill))
        else:
            x = op(x, jnp.where(idx >= s, pltpu.roll(x, s, axis=0), fill))
        s *= 2
    return x


def _bf16_part(x):
    return x.astype(BF16).astype(F32)


def _mlstm_gates(d, gt_ref, m):
    n = MLSTM_L
    gt = gt_ref[...]
    li = gt[:, :LANES]
    fx = gt[:, LANES:]
    lf = jnp.minimum(fx, 0.0) - jnp.log(1.0 + jnp.exp(-jnp.abs(fx)))
    rev = d == 1
    b = _scan_rows(lf, jnp.add, 0.0, rev)
    a = li - b
    cm = _scan_rows(a, jnp.maximum, -jnp.inf, rev)
    mt = jnp.maximum(cm, m) * LOG2_E
    u_hi = _bf16_part(mt)
    u_lo = _bf16_part(mt - u_hi)
    z = -(b * LOG2_E + (u_hi + u_lo))
    z_hi = _bf16_part(z)
    z_mid = _bf16_part(z - z_hi)
    z_lo = _bf16_part(z - z_hi - z_mid)
    lane = lax.broadcasted_iota(jnp.int32, mt.shape, 1)
    k = (lane - d * GATE_DIR_LANES) // NH_M
    terms = jnp.where(k == 0, u_hi, jnp.where(k == 1, u_lo, jnp.where(k == 2, z_hi, jnp.where(k == 3, z_mid, z_lo))))
    end = 0 if rev else n - 1
    mx = jnp.maximum(m, cm[end:end + 1, :])
    keep = jnp.exp(m - mx)
    m_new = b[end:end + 1, :] + mx
    return terms, (a * LOG2_E).T, jnp.exp(a - mx).T, keep, m_new


def _mlstm_heads(d, q_ref, kt_ref, v_ref, h_ref, c_sc, spread, m2, a_t, wend_t, keep):
    n = MLSTM_L
    row = lax.broadcasted_iota(jnp.int32, (n, n), 0)
    col = lax.broadcasted_iota(jnp.int32, (n, n), 1)
    mask = (col >= row) if d == 1 else (col <= row)
    ones = jnp.ones((n, LANES), BF16)
    for h in range(NH_M):
        c = d * GATE_DIR_LANES + h
        base = (d * NH_M + h) * 2 * LANES
        u = spread[:, base:base + LANES]
        zz = spread[:, base + LANES:base + 2 * LANES]
        qh = q_ref[:, h * DK_M:(h + 1) * DK_M]
        kth = kt_ref[h * DK_M:(h + 1) * DK_M, :]
        vext = jnp.concatenate([v_ref[:, h * DV_M:(h + 1) * DV_M], ones], axis=1)
        e = jnp.exp2(a_t[c:c + 1, :] - u)
        w = jnp.where(mask, _dot(qh, kth) * e, 0.0).astype(BF16)
        inter = jnp.exp2(m2[:, c:c + 1] - u)
        qi = (inter[:, :DK_M] * qh.astype(F32)).astype(BF16)
        cext = c_sc[d, h]
        r = _dot(jnp.concatenate([w, qi], axis=1), jnp.concatenate([vext, cext.astype(BF16)], axis=0))
        den = jnp.maximum(jnp.abs(r[:, DV_M:]), jnp.exp2(zz))
        h_ref[:, h * DV_M:(h + 1) * DV_M] = (r[:, :DV_M] / den).astype(BF16)
        kts = (kth.astype(F32) * wend_t[c:c + 1, :]).astype(BF16)
        c_sc[d, h] = keep[:, c:c + 1] * cext + _dot(kts, vext)


def _mlstm_kernel(*refs, zero_init):
    qf_ref, ktf_ref, vf_ref, gtf_ref, qb_ref, ktb_ref, vb_ref, gtb_ref, sel_ref = refs[:9]
    hf_ref, hb_ref, cout_ref, nout_ref, mout_ref, c_sc, m_sc = refs[-7:]
    i = pl.program_id(1)

    @pl.when(i == 0)
    def _():
        if zero_init:
            c_sc[...] = jnp.zeros(c_sc.shape, F32)
            m_sc[...] = jnp.zeros(m_sc.shape, F32)
        else:
            c0_ref, n0_ref, m0_ref = refs[9:12]
            c_sc[:, :, :, :DV_M] = c0_ref[0]
            c_sc[:, :, :, DV_M:] = n0_ref[0]
            m_sc[...] = m0_ref[0]

    m = m_sc[...]
    terms_f, at_f, wend_f, keep_f, mnew_f = _mlstm_gates(0, gtf_ref, m)
    terms_b, at_b, wend_b, keep_b, mnew_b = _mlstm_gates(1, gtb_ref, m)
    lane = lax.broadcasted_iota(jnp.int32, terms_f.shape, 1)
    packed = jnp.where(lane < GATE_DIR_LANES, terms_f, jnp.where(lane < 2 * GATE_DIR_LANES, terms_b, 0.0))
    spread = _dot(packed.astype(BF16), sel_ref[...])
    m2 = m * LOG2_E
    _mlstm_heads(0, qf_ref, ktf_ref, vf_ref, hf_ref, c_sc, spread, m2, at_f, wend_f, keep_f)
    _mlstm_heads(1, qb_ref, ktb_ref, vb_ref, hb_ref, c_sc, spread, m2, at_b, wend_b, keep_b)
    lane_m = lax.broadcasted_iota(jnp.int32, m.shape, 1)
    m_sc[...] = jnp.where(lane_m < GATE_DIR_LANES, mnew_f, jnp.where(lane_m < 2 * GATE_DIR_LANES, mnew_b, m))

    @pl.when(i == pl.num_programs(1) - 1)
    def _():
        cout_ref[0] = c_sc[:, :, :, :DV_M]
        nout_ref[0] = c_sc[:, :, :, DV_M:]
        mout_ref[0] = m_sc[...]


def _mlstm_core_call(q, kt, v, gt, sel, init, bsz, seq_len):
    t = q.shape[0]
    nc = seq_len // MLSTM_L
    fwd = lambda b, i: (b * nc + i, 0)
    bwd = lambda b, i: (b * nc + nc - 1 - i, 0)
    fwd_t = lambda b, i: (0, b * nc + i)
    bwd_t = lambda b, i: (0, b * nc + nc - 1 - i)

    def specs(row_map, col_map):
        return [
            pl.BlockSpec((MLSTM_L, QK_W), row_map),
            pl.BlockSpec((QK_W, MLSTM_L), col_map),
            pl.BlockSpec((MLSTM_L, V_W), row_map),
            pl.BlockSpec((MLSTM_L, 2 * LANES), row_map),
        ]

    half_shape = (2, NH_M, DK_M, DV_M)
    half_spec = pl.BlockSpec((1,) + half_shape, lambda b, i: (b, 0, 0, 0, 0))
    m_spec = pl.BlockSpec((1, 1, LANES), lambda b, i: (b, 0, 0))
    state_bytes = 2 * NH_M * DK_M * 2 * DV_M * 4
    sel_shape = (LANES, 2 * NH_M * 2 * LANES)
    in_specs = specs(fwd, fwd_t) + specs(bwd, bwd_t) + [pl.BlockSpec(sel_shape, lambda b, i: (0, 0))]
    args = [q, kt, v, gt, q, kt, v, gt, sel]
    if init is not None:
        in_specs += [half_spec, half_spec, m_spec]
        args += list(init)
    vmem = 6 * state_bytes + 8 * MLSTM_L * (2 * QK_W + 2 * V_W) * 2 + 4 * sel_shape[0] * sel_shape[1] + (16 << 20)
    return pl.pallas_call(
        functools.partial(_mlstm_kernel, zero_init=init is None),
        grid=(bsz, nc),
        in_specs=in_specs,
        out_specs=[pl.BlockSpec((MLSTM_L, V_W), fwd), pl.BlockSpec((MLSTM_L, V_W), bwd), half_spec, half_spec, m_spec],
        out_shape=[
            jax.ShapeDtypeStruct((t, V_W), BF16),
            jax.ShapeDtypeStruct((t, V_W), BF16),
            jax.ShapeDtypeStruct((bsz,) + half_shape, F32),
            jax.ShapeDtypeStruct((bsz,) + half_shape, F32),
            jax.ShapeDtypeStruct((bsz, 1, LANES), F32),
        ],
        scratch_shapes=[pltpu.VMEM((2, NH_M, DK_M, 2 * DV_M), F32), pltpu.VMEM((1, LANES), F32)],
        compiler_params=_cparams(("parallel", "arbitrary"), vmem),
        name="mlstm_core",
    )(*args)


def _rope_rows_swap(x):
    q = HD_A // 4
    parts = []
    for base in range(0, x.shape[0], 2 * q):
        parts += [x[base + q:base + 2 * q], x[base:base + q]]
    return jnp.concatenate(parts, axis=0)


def _attn_in_kernel(*refs, rope, emit_f32):
    x_ref, mod_ref, g_ref, wq_ref, wkt_ref, wv_ref = refs[:6]
    pos = 6
    if rope:
        cos_ref, sina_ref, sinb_ref, cost_ref, sint_ref = refs[pos:pos + 5]
        pos += 5
    if emit_f32:
        wk_ref = refs[pos]
        pos += 1
    q_ref, kt_ref, v_ref = refs[pos:pos + 3]
    pos += 3
    mod = mod_ref[0]
    hb = _mod_norm(x_ref[...], g_ref[...], mod[3:4], mod[4:5]).astype(BF16)
    q = _dot(hb, wq_ref[...])
    kt = _dot_nt(wkt_ref[...], hb)
    v = _dot(hb, wv_ref[...])
    if rope:
        cos, sina, sinb = cos_ref[...], sina_ref[...], sinb_ref[...]
        kt = kt * cost_ref[...] + _rope_rows_swap(kt) * sint_ref[...]
    blk = ATTN_BLOCK
    for gi in range(Q_W // LANES):
        qg = q[:, gi * LANES:(gi + 1) * LANES]
        if rope:
            qg = qg * cos + pltpu.roll(qg, LANES - HD_A // 4, axis=1) * sina + pltpu.roll(qg, HD_A // 4, axis=1) * sinb
        qg = (qg * (HD_A ** -0.5 * LOG2_E)).astype(BF16)
        p, g = gi // G_A, gi % G_A
        for tb in range(q.shape[0] // blk):
            q_ref[tb, p, g * blk:(g + 1) * blk, :] = qg[tb * blk:(tb + 1) * blk]
    kt_ref[...] = kt.astype(BF16)
    v_ref[...] = v.astype(BF16)
    if emit_f32:
        kf_ref, vf_ref = refs[pos:pos + 2]
        kf_ref[...] = _dot(hb, wk_ref[...])
        vf_ref[...] = v


def _attn_in_call(x, mod, g, wq, wkt, wv, seq_len, tm, rope_tabs=None, wk=None):
    t = x.shape[0]
    rope = rope_tabs is not None
    emit_f32 = wk is not None
    tiles = seq_len // tm
    tok = lambda width: pl.BlockSpec((tm, width), lambda i: (i, 0))
    in_specs = [tok(D_MODEL), _mod_spec(mod, tiles), _row_spec(D_MODEL),
                _resident((D_MODEL, Q_W)), _resident((KV_W, D_MODEL)), _resident((D_MODEL, KV_W))]
    args = [x, mod, g, wq, wkt, wv]
    if rope:
        pos_row = lambda i: (i % tiles, 0)
        pos_col = lambda i: (0, i % tiles)
        in_specs += [pl.BlockSpec((tm, LANES), pos_row)] * 3 + [pl.BlockSpec((KV_W, tm), pos_col)] * 2
        args += list(rope_tabs)
    if emit_f32:
        in_specs.append(_resident((D_MODEL, KV_W)))
        args.append(wk)
    q_blocks = (tm // ATTN_BLOCK, NKV_A // 2, G_A * ATTN_BLOCK, LANES)
    out_specs = [pl.BlockSpec(q_blocks, lambda i: (i, 0, 0, 0)), pl.BlockSpec((KV_W, tm), lambda i: (0, i)), tok(KV_W)]
    out_shape = [jax.ShapeDtypeStruct((t // ATTN_BLOCK,) + q_blocks[1:], BF16), jax.ShapeDtypeStruct((KV_W, t), BF16),
                 jax.ShapeDtypeStruct((t, KV_W), BF16)]
    if emit_f32:
        out_specs += [tok(KV_W), tok(KV_W)]
        out_shape += [jax.ShapeDtypeStruct((t, KV_W), F32)] * 2
    vmem = 2 * D_MODEL * (Q_W + 3 * KV_W) + 2 * tm * (D_MODEL * 4 + Q_W * 2 + 8 * KV_W + 5 * LANES * 4) \
        + 8 * tm * D_MODEL * 4 + (4 << 20)
    return pl.pallas_call(
        functools.partial(_attn_in_kernel, rope=rope, emit_f32=emit_f32),
        grid=(t // tm,),
        in_specs=in_specs,
        out_specs=out_specs,
        out_shape=out_shape,
        compiler_params=_cparams(("parallel",), vmem),
        name="attn_in",
    )(*args)


def _pair_attention(sink_ref, q_ref, o_ref, p, keys, mask_of):
    blk = ATTN_BLOCK
    rows_all = G_A * blk
    qp = q_ref[0, p]
    lane = lax.broadcasted_iota(jnp.int32, qp.shape, 1)
    row_g = lax.broadcasted_iota(jnp.int32, (rows_all, 1), 0) // blk
    ones = [jnp.ones(v.shape, BF16) for _, v in keys]
    v_ext = [jnp.concatenate([v, o1], axis=1) for (_, v), o1 in zip(keys, ones)]
    outs = []
    for e in range(2):
        kv = 2 * p + e
        qm = jnp.where((lane >= e * HD_A) & (lane < (e + 1) * HD_A), qp, jnp.zeros_like(qp))
        sink = jnp.full((rows_all, 1), sink_ref[0, kv * G_A], F32)
        for g in range(1, G_A):
            sink = jnp.where(row_g == g, sink_ref[0, kv * G_A + g], sink)
        sink = sink * LOG2_E
        scores = []
        m = sink
        for i, (kt, _) in enumerate(keys):
            s = _dot(qm, kt)
            msk = mask_of(i)
            s = s if msk is None else jnp.where(msk, s, NEG_INF)
            m = jnp.maximum(m, jnp.max(s, axis=-1, keepdims=True))
            scores.append(s)
        acc = None
        for s, vx in zip(scores, v_ext):
            pv = _dot(jnp.exp2(s - m).astype(BF16), vx)
            acc = pv if acc is None else acc + pv
        outs.append(acc[:, :LANES] / (acc[:, LANES:] + jnp.exp2(sink - m)))
    lane_o = lax.broadcasted_iota(jnp.int32, outs[0].shape, 1)
    a = jnp.where(lane_o < HD_A, outs[0], outs[1]).astype(BF16)
    for g in range(G_A):
        c0 = (p * G_A + g) * LANES
        o_ref[:, c0:c0 + LANES] = a[g * blk:(g + 1) * blk]


def _attn_latent_kernel(sink_ref, q_ref, ktp_ref, ktc_ref, ktn_ref, vp_ref, vc_ref, vn_ref, ktx_ref, vx_ref,
                        o_ref, *, n_tok):
    j = pl.program_id(1)
    blk = ATTN_BLOCK
    shape = (G_A * blk, 3 * blk)
    t_pos = j * blk + lax.broadcasted_iota(jnp.int32, shape, 0) % blk
    s_pos = (j - 1) * blk + lax.broadcasted_iota(jnp.int32, shape, 1)
    local = (jnp.abs(t_pos - s_pos) <= WINDOW) & (s_pos >= 0) & (s_pos < n_tok)
    for p in range(NKV_A // 2):
        rows = slice(p * LANES, (p + 1) * LANES)
        kt_loc = jnp.concatenate([ktp_ref[rows, :], ktc_ref[rows, :], ktn_ref[rows, :]], axis=1)
        v_loc = jnp.concatenate([vp_ref[:, rows], vc_ref[:, rows], vn_ref[:, rows]], axis=0)
        keys = [(kt_loc, v_loc), (ktx_ref[0, rows, :], vx_ref[0, :, rows])]
        _pair_attention(sink_ref, q_ref, o_ref, p, keys, lambda i: local if i == 0 else None)


def _q_pair_spec(nb):
    return pl.BlockSpec((1, NKV_A // 2, G_A * ATTN_BLOCK, LANES), lambda b, j: (b * nb + j, 0, 0, 0))


def _attn_latent_call(q, kt, v, ktx, vx, sink, bsz, seq_len):
    t = v.shape[0]
    nb = seq_len // ATTN_BLOCK
    n_ctx = ktx.shape[2]
    prev = lambda j: jnp.maximum(j - 1, 0)
    nxt = lambda j: jnp.minimum(j + 1, nb - 1)
    kt_spec = lambda f: pl.BlockSpec((KV_W, ATTN_BLOCK), lambda b, j: (0, b * nb + f(j)))
    v_spec = lambda f: pl.BlockSpec((ATTN_BLOCK, KV_W), lambda b, j: (b * nb + f(j), 0))
    same = lambda j: j
    return pl.pallas_call(
        functools.partial(_attn_latent_kernel, n_tok=seq_len),
        grid=(bsz, nb),
        in_specs=[
            pl.BlockSpec(memory_space=pltpu.SMEM),
            _q_pair_spec(nb),
            kt_spec(prev), kt_spec(same), kt_spec(nxt),
            v_spec(prev), v_spec(same), v_spec(nxt),
            pl.BlockSpec((1, KV_W, n_ctx), lambda b, j: (b, 0, 0)),
            pl.BlockSpec((1, n_ctx, KV_W), lambda b, j: (b, 0, 0)),
        ],
        out_specs=pl.BlockSpec((ATTN_BLOCK, Q_W), lambda b, j: (b * nb + j, 0)),
        out_shape=jax.ShapeDtypeStruct((t, Q_W), BF16),
        compiler_params=_cparams(("parallel", "parallel"), 32 << 20),
        name="attn_latent",
    )(sink, q, kt, kt, kt, v, v, v, ktx, vx)


def _attn_context_kernel(sink_ref, q_ref, kt_ref, v_ref, o_ref):
    for p in range(NKV_A // 2):
        rows = slice(p * LANES, (p + 1) * LANES)
        _pair_attention(sink_ref, q_ref, o_ref, p, [(kt_ref[rows, :], v_ref[:, rows])], lambda i: None)


def _attn_context_call(q, kt, v, sink, bsz, seq_len):
    t = v.shape[0]
    nb = seq_len // ATTN_BLOCK
    return pl.pallas_call(
        _attn_context_kernel,
        grid=(bsz, nb),
        in_specs=[
            pl.BlockSpec(memory_space=pltpu.SMEM),
            _q_pair_spec(nb),
            pl.BlockSpec((KV_W, seq_len), lambda b, j: (0, b)),
            pl.BlockSpec((seq_len, KV_W), lambda b, j: (b, 0)),
        ],
        out_specs=pl.BlockSpec((ATTN_BLOCK, Q_W), lambda b, j: (b * nb + j, 0)),
        out_shape=jax.ShapeDtypeStruct((t, Q_W), BF16),
        compiler_params=_cparams(("parallel", "parallel"), 32 << 20),
        name="attn_context",
    )(sink, q, kt, v)


def _rope_tables(n_tok):
    quarter = HD_A // 4
    freqs = ROPE_THETA ** (-jnp.arange(quarter, dtype=F32) / quarter)
    pos = jnp.arange(n_tok)
    row = (pos // GRID_W).astype(F32)
    col = (pos % GRID_W).astype(F32)
    ang_r, ang_c = row[:, None] * freqs, col[:, None] * freqs
    ang = jnp.concatenate([ang_r, ang_r, ang_c, ang_c], axis=-1)
    cos, sin = jnp.cos(ang), jnp.sin(ang)
    first = (jnp.arange(HD_A) % (2 * quarter)) < quarter
    sin_a = jnp.where(first, -sin, 0.0)
    sin_b = jnp.where(first, 0.0, sin)
    sin_t = jnp.where(first, -sin, sin)
    lane_tile = lambda a: jnp.tile(a, (1, LANES // HD_A))
    row_tile = lambda a: jnp.tile(a.T, (NKV_A, 1))
    return lane_tile(cos), lane_tile(sin_a), lane_tile(sin_b), row_tile(cos), row_tile(sin_t)


def _gate_lanes(a):
    lead = a.shape[:-2]
    rep = jnp.broadcast_to(a[..., :, None, :], lead + (2, GATE_COPIES, NH_M)).reshape(lead + (2 * GATE_DIR_LANES,))
    return jnp.concatenate([rep, jnp.zeros(lead + (LANES - 2 * GATE_DIR_LANES,), a.dtype)], axis=-1)


def _gate_layout(a):
    a4 = a.reshape(a.shape[:-1] + (2, 2, NH_M))
    return jnp.concatenate([_gate_lanes(a4[..., :, 0, :]), _gate_lanes(a4[..., :, 1, :])], axis=-1)


def _gate_selector():
    r = jnp.arange(LANES)[:, None]
    c = jnp.arange(2 * NH_M * 2 * LANES)[None, :]
    hd, j = c // (2 * LANES), c % (2 * LANES)
    rd, rk, rh = r // GATE_DIR_LANES, (r % GATE_DIR_LANES) // NH_M, r % NH_M
    hit = (r < 2 * GATE_DIR_LANES) & (rd == hd // NH_M) & (rh == hd % NH_M) & ((j < LANES) == (rk < 2))
    return hit.astype(BF16)


def _pair_layout_cols(w):
    d_in = w.shape[0]
    w5 = w.reshape(d_in, NKV_A // 2, 2, G_A, HD_A)
    return jnp.transpose(w5, (0, 1, 3, 2, 4)).reshape(d_in, Q_W)


def _mlstm_init(state_c, state_n, state_m):
    n_rep = jnp.broadcast_to(state_n[..., None].astype(F32), state_n.shape + (DV_M,))
    return state_c.astype(F32), n_rep, _gate_lanes(state_m.astype(F32))[:, None, :]


def kernel(x_prompt, x_sample, state_c, state_n, state_m, cache_k, cache_v, c, c_ctx, w_mod, b_mod, norm_g,
           ffn1_w_gu, ffn1_w_down, ffn2_w_gu, ffn2_w_down, mlstm_w_in, mlstm_b_gate, mlstm_g_head, mlstm_w_out,
           attn_w_in, attn_sink, attn_w_out, final_g):
    bp, n_p, _ = x_prompt.shape
    bs, n_s, _ = x_sample.shape
    xp = x_prompt.reshape(bp * n_p, D_MODEL)
    xs = x_sample.reshape(bs * n_s, D_MODEL)
    tm_p, tm_s = 512, 512

    cond = jnp.concatenate([c_ctx[None, :], c], axis=0)
    cond = jnp.pad(cond, ((0, COND_ROWS - cond.shape[0]), (0, 0)))
    mod_all = _mod_call(cond, w_mod, b_mod).reshape(DEPTH, COND_ROWS, N_MOD, D_MODEL)

    outs = {}
    for l in range(DEPTH):
        mod_p = mod_all[l, 0:1]
        mod_s = mod_all[l, 1:1 + bs]
        g = norm_g[l]
        wgu1, wd1 = ffn1_w_gu[l].astype(BF16), ffn1_w_down[l].astype(BF16)
        wgu2, wd2 = ffn2_w_gu[l].astype(BF16), ffn2_w_down[l].astype(BF16)
        xp = _ffn_call(xp, mod_p, g[0:1], wgu1, wd1, 0, n_p, tm_p)
        xs = _ffn_call(xs, mod_s, g[0:1], wgu1, wd1, 0, n_s, tm_s)
        i = l // 2
        if l % 2 == 0:
            w_in = mlstm_w_in[i]
            wq = w_in[:, :QK_W].astype(BF16)
            wkt = w_in[:, QK_W:2 * QK_W].T.astype(BF16)
            wvo = w_in[:, 2 * QK_W:2 * QK_W + 2 * V_W].astype(BF16)
            wg = _gate_layout(w_in[:, 2 * QK_W + 2 * V_W:]).astype(BF16)
            bg = _gate_layout(mlstm_b_gate[i].astype(F32))[None, :]
            gh = mlstm_g_head[i].astype(F32)[None, :]
            w_out = mlstm_w_out[i].astype(BF16)
            sel = _gate_selector()
            streams = [("p", xp, mod_p, n_p, tm_p, bp, None),
                       ("s", xs, mod_s, n_s, tm_s, bs, _mlstm_init(state_c[:, i], state_n[:, i], state_m[:, i]))]
            res = {}
            for tag, x, mod, n_tok, tm, bsz, init in streams:
                q, kt, v, o, gt = _mlstm_in_call(x, mod, g[1:2], wq, wkt, wvo, wg, bg, n_tok, tm)
                hf, hb, c_fin, n_fin, m_fin = _mlstm_core_call(q, kt, v, gt, sel, init, bsz, n_tok)
                res[tag] = ((hf, hb, o, gh, w_out), c_fin, n_fin, m_fin)
            mix_p, c_fin, n_fin, m_fin = res["p"]
            mix_s = res["s"][0]
            mix_key = "mlstm"
            dt = x_prompt.dtype
            m_heads = m_fin[:, 0, :2 * GATE_DIR_LANES].reshape(bp, 2, GATE_COPIES, NH_M)[:, :, 0, :]
            outs.setdefault("c", []).append(c_fin.astype(dt))
            outs.setdefault("n", []).append(n_fin[..., 0].astype(dt))
            outs.setdefault("m", []).append(m_heads.astype(dt))
        else:
            w_in = attn_w_in[i]
            wq = _pair_layout_cols(w_in[:, :Q_W]).astype(BF16)
            wk = w_in[:, Q_W:Q_W + KV_W].astype(BF16)
            wkt = w_in[:, Q_W:Q_W + KV_W].T.astype(BF16)
            wv = w_in[:, Q_W + KV_W:].astype(BF16)
            w_out = _pair_layout_cols(attn_w_out[i].T).T.astype(BF16)
            sink = attn_sink[i].astype(F32)[None, :]
            q, kt, v, kf, vf = _attn_in_call(xp, mod_p, g[1:2], wq, wkt, wv, n_p, tm_p, wk=wk)
            mix_p = (_attn_context_call(q, kt, v, sink, bp, n_p), w_out)
            mix_key = "attn"
            outs.setdefault("k", []).append(kf.reshape(bp, n_p, NKV_A, HD_A))
            outs.setdefault("v", []).append(vf.reshape(bp, n_p, NKV_A, HD_A))
            q, kt, v = _attn_in_call(xs, mod_s, g[1:2], wq, wkt, wv, n_s, tm_s, rope_tabs=_rope_tables(n_s))
            n_ctx = cache_k.shape[2]
            ktx = jnp.swapaxes(cache_k[:, i].reshape(bs, n_ctx, KV_W), 1, 2).astype(BF16)
            vx = cache_v[:, i].reshape(bs, n_ctx, KV_W).astype(BF16)
            mix_s = (_attn_latent_call(q, kt, v, ktx, vx, sink, bs, n_s), w_out)
        fg = final_g[None, :] if l == DEPTH - 1 else None
        xp = _ffn_call(xp, mod_p, g[2:3], wgu2, wd2, 2, n_p, tm_p, final_g=fg, **{mix_key: mix_p})
        xs = _ffn_call(xs, mod_s, g[2:3], wgu2, wd2, 2, n_s, tm_s, final_g=fg, **{mix_key: mix_s})

    return (xp.reshape(bp, n_p, D_MODEL), xs.reshape(bs, n_s, D_MODEL),
            jnp.stack(outs["c"], axis=1), jnp.stack(outs["n"], axis=1), jnp.stack(outs["m"], axis=1),
            jnp.stack(outs["k"], axis=1), jnp.stack(outs["v"], axis=1))
```

```python
import functools

import jax
import jax.numpy as jnp
from jax import lax
from jax.experimental import pallas as pl
from jax.experimental.pallas import tpu as pltpu

D_MODEL = 1024
DEPTH = 2
GRID_W = 64
D_FF = 2816
FFN_RES = 0.5
NH_M = 8
DK_M = 64
DV_M = 128
NH_A = 16
NKV_A = 4
G_A = NH_A // NKV_A
HD_A = 64
WINDOW = 128
ATTN_BLOCK = 128
ROPE_THETA = 10000.0
EPS = 1e-6
NEG_INF = -1e30
N_MOD = 9

QK_W = NH_M * DK_M
V_W = NH_M * DV_M
KV_W = NKV_A * HD_A
Q_W = NH_A * HD_A

LANES = 128
MLSTM_L = 128
FFN_CHUNK = 256
FFN_W_STEPS = 11
FFN_WGU_COLS = 2 * D_FF // FFN_W_STEPS
FFN_WD_ROWS = D_FF // FFN_W_STEPS
ONES_ROWS = 16
LOG2_E = 1.4426950408889634
GATE_COPIES = 5
GATE_DIR_LANES = GATE_COPIES * NH_M
COND_ROWS = 16
VMEM_CAP = 60000 * 1024

F32 = jnp.float32
BF16 = jnp.bfloat16


def _cparams(sem, vmem_bytes):
    return pltpu.CompilerParams(dimension_semantics=sem, vmem_limit_bytes=min(int(vmem_bytes), VMEM_CAP))


def _dot(a, b):
    return jnp.dot(a, b, preferred_element_type=F32)


def _dot_nt(a, b):
    return lax.dot_general(a, b, (((1,), (1,)), ((), ())), preferred_element_type=F32)


def _rms(x):
    return x * lax.rsqrt(jnp.mean(x * x, axis=-1, keepdims=True) + EPS)


def _sigmoid(x):
    return 1.0 / (1.0 + jnp.exp(-x))


def _mod_norm(x, g, shift, scale):
    return (_rms(x) * g) * (1.0 + scale) + shift


def _mod_kernel(c_ref, w_ref, b_ref, o_ref):
    c = c_ref[...]
    s = (c * _sigmoid(c)).astype(BF16)
    o_ref[0] = _dot(s, w_ref[0].astype(BF16)) + b_ref[0]


def _mod_call(cond, w_mod, b_mod):
    tn = D_MODEL
    n_out = N_MOD * D_MODEL
    return pl.pallas_call(
        _mod_kernel,
        grid=(DEPTH, n_out // tn),
        in_specs=[
            pl.BlockSpec((COND_ROWS, D_MODEL), lambda l, n: (0, 0)),
            pl.BlockSpec((1, D_MODEL, tn), lambda l, n: (l, 0, n)),
            pl.BlockSpec((1, 1, tn), lambda l, n: (l, 0, n)),
        ],
        out_specs=pl.BlockSpec((1, COND_ROWS, tn), lambda l, n: (l, 0, n)),
        out_shape=jax.ShapeDtypeStruct((DEPTH, COND_ROWS, n_out), F32),
        compiler_params=_cparams(("parallel", "parallel"), 4 * (2 * D_MODEL * tn * 4)),
        name="adaln_mod",
    )(cond, w_mod, b_mod.reshape(DEPTH, 1, n_out))


def _mod_spec(mod, tiles_per_seq, tile=lambda i: i):
    if mod.shape[0] == 1:
        return pl.BlockSpec((1, N_MOD, D_MODEL), lambda i: (0, 0, 0))
    return pl.BlockSpec((1, N_MOD, D_MODEL), lambda i: (tile(i) // tiles_per_seq, 0, 0))


def _row_spec(width):
    return pl.BlockSpec((1, width), lambda i: (0, 0))


def _resident(shape):
    return pl.BlockSpec(shape, lambda i: (0,) * len(shape), pipeline_mode=pl.Buffered(1))


def _mlstm_gated_heads(hf_ref, hb_ref, o_ref, gh_ref):
    hs = hf_ref[...].astype(F32) + hb_ref[...].astype(F32)
    parts = [_rms(hs[:, h * DV_M:(h + 1) * DV_M]) for h in range(NH_M)]
    hn = jnp.concatenate(parts, axis=1) * gh_ref[...]
    return (hn * _sigmoid(o_ref[...].astype(F32))).astype(BF16)


def _ffn_kernel(*refs, j, final, mixer):
    x_ref, mod_ref, g_ref, wgu32_ref, wd32_ref, fg_ref = refs[:6]
    o_ref, wgu_ref, wd_ref = refs[-3:]
    step = pl.program_id(0)

    @pl.when(step < FFN_W_STEPS)
    def _():
        c0 = pl.multiple_of(step * FFN_WGU_COLS, FFN_WGU_COLS)
        r0 = pl.multiple_of(step * FFN_WD_ROWS, FFN_WD_ROWS)
        wgu_ref[:, pl.ds(c0, FFN_WGU_COLS)] = wgu32_ref[0].astype(BF16)
        wd_ref[pl.ds(r0, FFN_WD_ROWS), :] = wd32_ref[0].astype(BF16)

    @pl.when(step >= FFN_W_STEPS)
    def _():
        x = x_ref[...]
        mod = mod_ref[0]
        if mixer == "mlstm":
            hf_ref, hb_ref, og_ref, gh_ref, wo_ref = refs[6:11]
            x = x + mod[5:6] * _dot(_mlstm_gated_heads(hf_ref, hb_ref, og_ref, gh_ref), wo_ref[...])
        elif mixer == "attn":
            a_ref, wo_ref = refs[6:8]
            x = x + mod[5:6] * _dot(a_ref[...], wo_ref[...])
        shift, scale, gate = mod[3 * j:3 * j + 1], mod[3 * j + 1:3 * j + 2], mod[3 * j + 2:3 * j + 3]
        hb = _mod_norm(x, g_ref[...], shift, scale).astype(BF16)
        acc = jnp.zeros(x.shape, F32)
        for c in range(D_FF // FFN_CHUNK):
            lo = c * FFN_CHUNK
            gg = _dot(hb, wgu_ref[:, lo:lo + FFN_CHUNK])
            uu = _dot(hb, wgu_ref[:, D_FF + lo:D_FF + lo + FFN_CHUNK])
            act = (gg * _sigmoid(gg) * uu).astype(BF16)
            acc = acc + _dot(act, wd_ref[lo:lo + FFN_CHUNK, :])
        y = x + (FFN_RES * gate) * acc
        if final:
            y = _rms(y) * fg_ref[...]
        o_ref[...] = y


def _ffn_call(x, mod, g, wgu, wd, layer, j, seq_len, tm, final_g=None, mlstm=None, attn=None):
    t = x.shape[0]
    final = final_g is not None
    fg = final_g if final else g
    tile = lambda i: jnp.maximum(i - FFN_W_STEPS, 0)
    wstep = lambda i: jnp.minimum(i, FFN_W_STEPS - 1)
    tok = lambda width: pl.BlockSpec((tm, width), lambda i: (tile(i), 0))
    in_specs = [tok(D_MODEL), _mod_spec(mod, seq_len // tm, tile), _row_spec(D_MODEL),
                pl.BlockSpec((1, D_MODEL, FFN_WGU_COLS), lambda i: (layer, 0, wstep(i))),
                pl.BlockSpec((1, FFN_WD_ROWS, D_MODEL), lambda i: (layer, wstep(i), 0)), _row_spec(D_MODEL)]
    args = [x, mod, g, wgu, wd, fg]
    mixer, extra = None, 0
    if mlstm is not None:
        mixer = "mlstm"
        in_specs += [tok(V_W), tok(V_W), tok(V_W), _row_spec(V_W), _resident((V_W, D_MODEL))]
        args += list(mlstm)
        extra = 2 * V_W * D_MODEL + 2 * tm * 3 * V_W * 2 + 3 * tm * V_W * 4
    elif attn is not None:
        mixer = "attn"
        in_specs += [tok(Q_W), _resident((Q_W, D_MODEL))]
        args += list(attn)
        extra = 2 * Q_W * D_MODEL + 2 * tm * Q_W * 2 + tm * Q_W * 4
    w_stream = 2 * 4 * (D_MODEL * FFN_WGU_COLS + FFN_WD_ROWS * D_MODEL)
    vmem = 2 * (D_MODEL * 2 * D_FF + D_FF * D_MODEL) + w_stream + 12 * tm * D_MODEL * 4 + extra + (8 << 20)
    return pl.pallas_call(
        functools.partial(_ffn_kernel, j=j, final=final, mixer=mixer),
        grid=(FFN_W_STEPS + t // tm,),
        in_specs=in_specs,
        out_specs=tok(D_MODEL),
        out_shape=jax.ShapeDtypeStruct((t, D_MODEL), F32),
        scratch_shapes=[pltpu.VMEM((D_MODEL, 2 * D_FF), BF16), pltpu.VMEM((D_FF, D_MODEL), BF16)],
        compiler_params=_cparams(("arbitrary",), vmem),
        name=f"ffn{j // 2 + 1}" + (f"_{mixer}" if mixer else ""),
    )(*args)


def _mlstm_in_kernel(x_ref, mod_ref, g_ref, wq_ref, wkt_ref, wvo_ref, wg_ref, bg_ref,
                     q_ref, kt_ref, v_ref, o_ref, gt_ref):
    mod = mod_ref[0]
    hb = _mod_norm(x_ref[...], g_ref[...], mod[3:4], mod[4:5]).astype(BF16)
    q_ref[...] = (_dot(hb, wq_ref[...]) * DK_M ** -0.5).astype(BF16)
    kt_ref[...] = _dot_nt(wkt_ref[...], hb).astype(BF16)
    vo = _dot(hb, wvo_ref[...])
    v_ref[...] = vo[:, :V_W].astype(BF16)
    o_ref[...] = vo[:, V_W:].astype(BF16)
    gt_ref[...] = _dot(hb, wg_ref[...]) + bg_ref[...]


def _mlstm_in_call(x, mod, g, wq, wkt, wvo, wg, bg, seq_len, tm):
    t = x.shape[0]
    w_bytes = 2 * (D_MODEL * (2 * QK_W + 2 * V_W + 2 * LANES))
    vmem = w_bytes + 2 * tm * (D_MODEL * 4 + (2 * QK_W + 2 * V_W) * 2 + 2 * LANES * 4) + 6 * tm * D_MODEL * 4 + (4 << 20)
    return pl.pallas_call(
        _mlstm_in_kernel,
        grid=(t // tm,),
        in_specs=[
            pl.BlockSpec((tm, D_MODEL), lambda i: (i, 0)),
            _mod_spec(mod, seq_len // tm),
            _row_spec(D_MODEL),
            _resident((D_MODEL, QK_W)),
            _resident((QK_W, D_MODEL)),
            _resident((D_MODEL, 2 * V_W)),
            _resident((D_MODEL, 2 * LANES)),
            _row_spec(2 * LANES),
        ],
        out_specs=[
            pl.BlockSpec((tm, QK_W), lambda i: (i, 0)),
            pl.BlockSpec((QK_W, tm), lambda i: (0, i)),
            pl.BlockSpec((tm, V_W), lambda i: (i, 0)),
            pl.BlockSpec((tm, V_W), lambda i: (i, 0)),
            pl.BlockSpec((tm, 2 * LANES), lambda i: (i, 0)),
        ],
        out_shape=[
            jax.ShapeDtypeStruct((t, QK_W), BF16),
            jax.ShapeDtypeStruct((QK_W, t), BF16),
            jax.ShapeDtypeStruct((t, V_W), BF16),
            jax.ShapeDtypeStruct((t, V_W), BF16),
            jax.ShapeDtypeStruct((t, 2 * LANES), F32),
        ],
        compiler_params=_cparams(("parallel",), vmem),
        name="mlstm_in",
    )(x, mod, g, wq, wkt, wvo, wg, bg)


def _scan_rows(x, op, fill, reverse):
    n = x.shape[0]
    idx = lax.broadcasted_iota(jnp.int32, x.shape, 0)
    s = 1
    while s < n:
        if reverse:
            x = op(x, jnp.where(idx < n - s, pltpu.roll(x, n - s, axis=0), fill))
        else:
            x = op(x, jnp.where(idx >= s, pltpu.roll(x, s, axis=0), fill))
        s *= 2
    return x


def _bf16_part(x):
    return x.astype(BF16).astype(F32)


def _mlstm_gates(d, gt_ref, m):
    n = MLSTM_L
    gt = gt_ref[...]
    li = gt[:, :LANES]
    fx = gt[:, LANES:]
    lf = jnp.minimum(fx, 0.0) - jnp.log(1.0 + jnp.exp(-jnp.abs(fx)))
    rev = d == 1
    b = _scan_rows(lf, jnp.add, 0.0, rev)
    a = li - b
    cm = _scan_rows(a, jnp.maximum, -jnp.inf, rev)
    mt = jnp.maximum(cm, m) * LOG2_E
    u_hi = _bf16_part(mt)
    u_lo = _bf16_part(mt - u_hi)
    z = -(b * LOG2_E + (u_hi + u_lo))
    z_hi = _bf16_part(z)
    z_mid = _bf16_part(z - z_hi)
    z_lo = _bf16_part(z - z_hi - z_mid)
    lane = lax.broadcasted_iota(jnp.int32, mt.shape, 1)
    k = (lane - d * GATE_DIR_LANES) // NH_M
    terms = jnp.where(k == 0, u_hi, jnp.where(k == 1, u_lo, jnp.where(k == 2, z_hi, jnp.where(k == 3, z_mid, z_lo))))
    end = 0 if rev else n - 1
    mx = jnp.maximum(m, cm[end:end + 1, :])
    keep = jnp.exp(m - mx)
    m_new = b[end:end + 1, :] + mx
    return terms, (a * LOG2_E).T, jnp.exp(a - mx).T, keep, m_new


def _mlstm_heads(dirs, c_sc, spread, m2):
    n = MLSTM_L
    row = lax.broadcasted_iota(jnp.int32, (n, n), 0)
    col = lax.broadcasted_iota(jnp.int32, (n, n), 1)
    ones = jnp.ones((n, LANES), BF16)
    heads = [(dd, h) for dd in dirs for h in range(NH_M)]
    lhs = []
    for (d, q_ref, kt_ref, _, _, a_t, _, _), h in heads:
        c = d * GATE_DIR_LANES + h
        base = (d * NH_M + h) * 2 * LANES
        u = spread[:, base:base + LANES]
        qh = q_ref[:, h * DK_M:(h + 1) * DK_M]
        e = jnp.exp2(a_t[c:c + 1, :] - u)
        mask = (col >= row) if d == 1 else (col <= row)
        w = jnp.where(mask, _dot(qh, kt_ref[h * DK_M:(h + 1) * DK_M, :]) * e, 0.0).astype(BF16)
        inter = jnp.exp2(m2[:, c:c + 1] - u)
        qi = (inter[:, :DK_M] * qh.astype(F32)).astype(BF16)
        lhs.append(jnp.concatenate([w, qi], axis=1))
    for ((d, _, _, v_ref, h_ref, _, _, _), h), wq in zip(heads, lhs):
        base = (d * NH_M + h) * 2 * LANES
        zz = spread[:, base + LANES:base + 2 * LANES]
        vext = jnp.concatenate([v_ref[:, h * DV_M:(h + 1) * DV_M], ones], axis=1)
        r = _dot(wq, jnp.concatenate([vext, c_sc[d, h].astype(BF16)], axis=0))
        den = jnp.maximum(jnp.abs(r[:, DV_M:]), jnp.exp2(zz))
        h_ref[:, h * DV_M:(h + 1) * DV_M] = (r[:, :DV_M] / den).astype(BF16)
    for (d, _, kt_ref, v_ref, _, _, wend_t, keep), h in heads:
        c = d * GATE_DIR_LANES + h
        vext = jnp.concatenate([v_ref[:, h * DV_M:(h + 1) * DV_M], ones], axis=1)
        kts = (kt_ref[h * DK_M:(h + 1) * DK_M, :].astype(F32) * wend_t[c:c + 1, :]).astype(BF16)
        c_sc[d, h] = keep[:, c:c + 1] * c_sc[d, h] + _dot(kts, vext)


def _mlstm_kernel(*refs, zero_init):
    qf_ref, ktf_ref, vf_ref, gtf_ref, qb_ref, ktb_ref, vb_ref, gtb_ref, sel_ref = refs[:9]
    hf_ref, hb_ref, cout_ref, nout_ref, mout_ref, c_sc, m_sc = refs[-7:]
    i = pl.program_id(1)

    @pl.when(i == 0)
    def _():
        if zero_init:
            c_sc[...] = jnp.zeros(c_sc.shape, F32)
            m_sc[...] = jnp.zeros(m_sc.shape, F32)
        else:
            c0_ref, n0_ref, m0_ref = refs[9:12]
            c_sc[:, :, :, :DV_M] = c0_ref[0]
            c_sc[:, :, :, DV_M:] = n0_ref[0]
            m_sc[...] = m0_ref[0]

    m = m_sc[...]
    terms_f, at_f, wend_f, keep_f, mnew_f = _mlstm_gates(0, gtf_ref, m)
    terms_b, at_b, wend_b, keep_b, mnew_b = _mlstm_gates(1, gtb_ref, m)
    lane = lax.broadcasted_iota(jnp.int32, terms_f.shape, 1)
    packed = jnp.where(lane < GATE_DIR_LANES, terms_f, jnp.where(lane < 2 * GATE_DIR_LANES, terms_b, 0.0))
    spread = _dot(packed.astype(BF16), sel_ref[...])
    m2 = m * LOG2_E
    _mlstm_heads([(0, qf_ref, ktf_ref, vf_ref, hf_ref, at_f, wend_f, keep_f),
                  (1, qb_ref, ktb_ref, vb_ref, hb_ref, at_b, wend_b, keep_b)], c_sc, spread, m2)
    lane_m = lax.broadcasted_iota(jnp.int32, m.shape, 1)
    m_sc[...] = jnp.where(lane_m < GATE_DIR_LANES, mnew_f, jnp.where(lane_m < 2 * GATE_DIR_LANES, mnew_b, m))

    @pl.when(i == pl.num_programs(1) - 1)
    def _():
        cout_ref[0] = c_sc[:, :, :, :DV_M]
        nout_ref[0] = c_sc[:, :, :, DV_M:]
        mout_ref[0] = m_sc[...]


def _mlstm_core_call(q, kt, v, gt, sel, init, bsz, seq_len):
    t = q.shape[0]
    nc = seq_len // MLSTM_L
    fwd = lambda b, i: (b * nc + i, 0)
    bwd = lambda b, i: (b * nc + nc - 1 - i, 0)
    fwd_t = lambda b, i: (0, b * nc + i)
    bwd_t = lambda b, i: (0, b * nc + nc - 1 - i)

    def specs(row_map, col_map):
        return [
            pl.BlockSpec((MLSTM_L, QK_W), row_map),
            pl.BlockSpec((QK_W, MLSTM_L), col_map),
            pl.BlockSpec((MLSTM_L, V_W), row_map),
            pl.BlockSpec((MLSTM_L, 2 * LANES), row_map),
        ]

    half_shape = (2, NH_M, DK_M, DV_M)
    half_spec = pl.BlockSpec((1,) + half_shape, lambda b, i: (b, 0, 0, 0, 0))
    m_spec = pl.BlockSpec((1, 1, LANES), lambda b, i: (b, 0, 0))
    state_bytes = 2 * NH_M * DK_M * 2 * DV_M * 4
    sel_shape = (LANES, 2 * NH_M * 2 * LANES)
    in_specs = specs(fwd, fwd_t) + specs(bwd, bwd_t) + [pl.BlockSpec(sel_shape, lambda b, i: (0, 0))]
    args = [q, kt, v, gt, q, kt, v, gt, sel]
    if init is not None:
        in_specs += [half_spec, half_spec, m_spec]
        args += list(init)
    vmem = 6 * state_bytes + 8 * MLSTM_L * (2 * QK_W + 2 * V_W) * 2 + 4 * sel_shape[0] * sel_shape[1] + (16 << 20)
    return pl.pallas_call(
        functools.partial(_mlstm_kernel, zero_init=init is None),
        grid=(bsz, nc),
        in_specs=in_specs,
        out_specs=[pl.BlockSpec((MLSTM_L, V_W), fwd), pl.BlockSpec((MLSTM_L, V_W), bwd), half_spec, half_spec, m_spec],
        out_shape=[
            jax.ShapeDtypeStruct((t, V_W), BF16),
            jax.ShapeDtypeStruct((t, V_W), BF16),
            jax.ShapeDtypeStruct((bsz,) + half_shape, F32),
            jax.ShapeDtypeStruct((bsz,) + half_shape, F32),
            jax.ShapeDtypeStruct((bsz, 1, LANES), F32),
        ],
        scratch_shapes=[pltpu.VMEM((2, NH_M, DK_M, 2 * DV_M), F32), pltpu.VMEM((1, LANES), F32)],
        compiler_params=_cparams(("parallel", "arbitrary"), vmem),
        name="mlstm_core",
    )(*args)


def _rope_rows_swap(x):
    q = HD_A // 4
    parts = []
    for base in range(0, x.shape[0], 2 * q):
        parts += [x[base + q:base + 2 * q], x[base:base + q]]
    return jnp.concatenate(parts, axis=0)


def _attn_in_kernel(*refs, rope, emit_f32):
    x_ref, mod_ref, g_ref, wqt_ref, wk_ref, wvt_ref = refs[:6]
    pos = 6
    if rope:
        cos_ref, sina_ref, sinb_ref, cost_ref, sint_ref = refs[pos:pos + 5]
        pos += 5
    if emit_f32:
        wv_ref = refs[pos]
        pos += 1
    qt_ref, k_ref, vt_ref = refs[pos:pos + 3]
    pos += 3
    mod = mod_ref[0]
    hb = _mod_norm(x_ref[...], g_ref[...], mod[3:4], mod[4:5]).astype(BF16)
    qt = _dot_nt(wqt_ref[...], hb)
    k = _dot(hb, wk_ref[...])
    vt_ref[...] = _dot_nt(wvt_ref[...], hb).astype(BF16)
    if emit_f32:
        kf_ref, vf_ref = refs[pos:pos + 2]
        kf_ref[...] = k
        vf_ref[...] = _dot(hb, wv_ref[...])
    blk = ATTN_BLOCK
    for gi in range(Q_W // LANES):
        qg = qt[gi * LANES:(gi + 1) * LANES, :]
        if rope:
            cos_t = jnp.concatenate([cost_ref[...]] * (LANES // HD_A), axis=0)
            sin_t = jnp.concatenate([sint_ref[...]] * (LANES // HD_A), axis=0)
            qg = qg * cos_t + _rope_rows_swap(qg) * sin_t
        qg = (qg * (HD_A ** -0.5 * LOG2_E)).astype(BF16)
        p, g = gi // G_A, gi % G_A
        for tb in range(qt.shape[1] // blk):
            col = tb * G_A * blk + g * blk
            qt_ref[p, :, col:col + blk] = qg[:, tb * blk:(tb + 1) * blk]
    for gi in range(KV_W // LANES):
        kg = k[:, gi * LANES:(gi + 1) * LANES]
        if rope:
            kg = (kg * cos_ref[...] + pltpu.roll(kg, LANES - HD_A // 4, axis=1) * sina_ref[...]
                  + pltpu.roll(kg, HD_A // 4, axis=1) * sinb_ref[...])
        k_ref[:, gi * LANES:(gi + 1) * LANES] = kg.astype(BF16)


def _attn_in_call(x, mod, g, wqt, wk, wvt, seq_len, tm, rope_tabs=None, wv=None):
    t = x.shape[0]
    rope = rope_tabs is not None
    emit_f32 = wv is not None
    tiles = seq_len // tm
    tok = lambda width: pl.BlockSpec((tm, width), lambda i: (i, 0))
    in_specs = [tok(D_MODEL), _mod_spec(mod, tiles), _row_spec(D_MODEL),
                _resident((Q_W, D_MODEL)), _resident((D_MODEL, KV_W)), _resident((KV_W, D_MODEL))]
    args = [x, mod, g, wqt, wk, wvt]
    if rope:
        pos_row = lambda i: (i % tiles, 0)
        pos_col = lambda i: (0, i % tiles)
        in_specs += [pl.BlockSpec((tm, LANES), pos_row)] * 3 + [pl.BlockSpec((HD_A, tm), pos_col)] * 2
        args += list(rope_tabs)
    if emit_f32:
        in_specs.append(_resident((D_MODEL, KV_W)))
        args.append(wv)
    q_cols = G_A * tm
    out_specs = [pl.BlockSpec((NKV_A // 2, LANES, q_cols), lambda i: (0, 0, i)), tok(KV_W),
                 pl.BlockSpec((KV_W, tm), lambda i: (0, i))]
    out_shape = [jax.ShapeDtypeStruct((NKV_A // 2, LANES, G_A * t), BF16), jax.ShapeDtypeStruct((t, KV_W), BF16),
                 jax.ShapeDtypeStruct((KV_W, t), BF16)]
    if emit_f32:
        out_specs += [tok(KV_W), tok(KV_W)]
        out_shape += [jax.ShapeDtypeStruct((t, KV_W), F32)] * 2
    vmem = 2 * D_MODEL * (Q_W + 3 * KV_W) + 2 * tm * (D_MODEL * 4 + Q_W * 2 + 8 * KV_W + 5 * LANES * 4) \
        + 8 * tm * D_MODEL * 4 + (4 << 20)
    return pl.pallas_call(
        functools.partial(_attn_in_kernel, rope=rope, emit_f32=emit_f32),
        grid=(t // tm,),
        in_specs=in_specs,
        out_specs=out_specs,
        out_shape=out_shape,
        compiler_params=_cparams(("parallel",), vmem),
        name="attn_in",
    )(*args)


def _block_attention(sink_ref, qt_ref, o_ref, pair_keys, mask_of):
    blk = ATTN_BLOCK
    cols_all = G_A * blk
    col_g = lax.broadcasted_iota(jnp.int32, (1, cols_all), 1) // blk
    staged = []
    for p, keys in enumerate(pair_keys):
        qp = qt_ref[p]
        row = lax.broadcasted_iota(jnp.int32, qp.shape, 0)
        for e in range(2):
            kv = 2 * p + e
            qm = jnp.where((row >= e * HD_A) & (row < (e + 1) * HD_A), qp, jnp.zeros_like(qp))
            sink = jnp.full((1, cols_all), sink_ref[0, kv * G_A], F32)
            for g in range(1, G_A):
                sink = jnp.where(col_g == g, sink_ref[0, kv * G_A + g], sink)
            sink = sink * LOG2_E
            scores = []
            m = sink
            for i, (k, _) in enumerate(keys):
                s = _dot(k, qm)
                msk = mask_of(i)
                s = s if msk is None else jnp.where(msk, s, NEG_INF)
                m = jnp.maximum(m, jnp.max(s, axis=0, keepdims=True))
                scores.append(s)
            staged.append((scores, m, sink))
    outs = []
    for idx, (scores, m, sink) in enumerate(staged):
        acc = None
        for s, (_, vt) in zip(scores, pair_keys[idx // 2]):
            vx = jnp.concatenate([vt, jnp.ones((ONES_ROWS, vt.shape[1]), BF16)], axis=0)
            pv = _dot(vx, jnp.exp2(s - m).astype(BF16))
            acc = pv if acc is None else acc + pv
        outs.append(acc[:LANES] / (acc[LANES:LANES + 1] + jnp.exp2(sink - m)))
    row_o = lax.broadcasted_iota(jnp.int32, outs[0].shape, 0)
    for p in range(len(pair_keys)):
        a_t = jnp.where(row_o < HD_A, outs[2 * p], outs[2 * p + 1])
        for g in range(G_A):
            c0 = (p * G_A + g) * LANES
            o_ref[:, c0:c0 + LANES] = a_t[:, g * blk:(g + 1) * blk].T.astype(BF16)


def _attn_latent_kernel(sink_ref, qt_ref, kp_ref, kc_ref, kn_ref, vtp_ref, vtc_ref, vtn_ref, kx_ref, vtx_ref,
                        o_ref, *, n_tok):
    j = pl.program_id(1)
    blk = ATTN_BLOCK
    shape = (3 * blk, G_A * blk)
    s_pos = (j - 1) * blk + lax.broadcasted_iota(jnp.int32, shape, 0)
    t_pos = j * blk + lax.broadcasted_iota(jnp.int32, shape, 1) % blk
    local = (jnp.abs(t_pos - s_pos) <= WINDOW) & (s_pos >= 0) & (s_pos < n_tok)
    pair_keys = []
    for p in range(NKV_A // 2):
        pl_ = slice(p * LANES, (p + 1) * LANES)
        k_loc = jnp.concatenate([kp_ref[:, pl_], kc_ref[:, pl_], kn_ref[:, pl_]], axis=0)
        vt_loc = jnp.concatenate([vtp_ref[pl_, :], vtc_ref[pl_, :], vtn_ref[pl_, :]], axis=1)
        pair_keys.append([(k_loc, vt_loc), (kx_ref[0, :, pl_], vtx_ref[0, pl_, :])])
    _block_attention(sink_ref, qt_ref, o_ref, pair_keys, lambda i: local if i == 0 else None)


def _qt_pair_spec(nb):
    return pl.BlockSpec((NKV_A // 2, LANES, G_A * ATTN_BLOCK), lambda b, j: (0, 0, b * nb + j))


def _attn_latent_call(qt, k, vt, kx, vtx, sink, bsz, seq_len):
    t = k.shape[0]
    nb = seq_len // ATTN_BLOCK
    n_ctx = kx.shape[1]
    prev = lambda j: jnp.maximum(j - 1, 0)
    nxt = lambda j: jnp.minimum(j + 1, nb - 1)
    k_spec = lambda f: pl.BlockSpec((ATTN_BLOCK, KV_W), lambda b, j: (b * nb + f(j), 0))
    vt_spec = lambda f: pl.BlockSpec((KV_W, ATTN_BLOCK), lambda b, j: (0, b * nb + f(j)))
    same = lambda j: j
    return pl.pallas_call(
        functools.partial(_attn_latent_kernel, n_tok=seq_len),
        grid=(bsz, nb),
        in_specs=[
            pl.BlockSpec(memory_space=pltpu.SMEM),
            _qt_pair_spec(nb),
            k_spec(prev), k_spec(same), k_spec(nxt),
            vt_spec(prev), vt_spec(same), vt_spec(nxt),
            pl.BlockSpec((1, n_ctx, KV_W), lambda b, j: (b, 0, 0)),
            pl.BlockSpec((1, KV_W, n_ctx), lambda b, j: (b, 0, 0)),
        ],
        out_specs=pl.BlockSpec((ATTN_BLOCK, Q_W), lambda b, j: (b * nb + j, 0)),
        out_shape=jax.ShapeDtypeStruct((t, Q_W), BF16),
        compiler_params=_cparams(("parallel", "parallel"), 32 << 20),
        name="attn_latent",
    )(sink, qt, k, k, k, vt, vt, vt, kx, vtx)


def _attn_context_kernel(sink_ref, qt_ref, k_ref, vt_ref, o_ref):
    pair_keys = [[(k_ref[:, p * LANES:(p + 1) * LANES], vt_ref[p * LANES:(p + 1) * LANES, :])]
                 for p in range(NKV_A // 2)]
    _block_attention(sink_ref, qt_ref, o_ref, pair_keys, lambda i: None)


def _attn_context_call(qt, k, vt, sink, bsz, seq_len):
    t = k.shape[0]
    nb = seq_len // ATTN_BLOCK
    return pl.pallas_call(
        _attn_context_kernel,
        grid=(bsz, nb),
        in_specs=[
            pl.BlockSpec(memory_space=pltpu.SMEM),
            _qt_pair_spec(nb),
            pl.BlockSpec((seq_len, KV_W), lambda b, j: (b, 0)),
            pl.BlockSpec((KV_W, seq_len), lambda b, j: (0, b)),
        ],
        out_specs=pl.BlockSpec((ATTN_BLOCK, Q_W), lambda b, j: (b * nb + j, 0)),
        out_shape=jax.ShapeDtypeStruct((t, Q_W), BF16),
        compiler_params=_cparams(("parallel", "parallel"), 32 << 20),
        name="attn_context",
    )(sink, qt, k, vt)


def _rope_tables(n_tok):
    quarter = HD_A // 4
    freqs = ROPE_THETA ** (-jnp.arange(quarter, dtype=F32) / quarter)
    pos = jnp.arange(n_tok)
    row = (pos // GRID_W).astype(F32)
    col = (pos % GRID_W).astype(F32)
    ang_r, ang_c = row[:, None] * freqs, col[:, None] * freqs
    ang = jnp.concatenate([ang_r, ang_r, ang_c, ang_c], axis=-1)
    cos, sin = jnp.cos(ang), jnp.sin(ang)
    first = (jnp.arange(HD_A) % (2 * quarter)) < quarter
    sin_a = jnp.where(first, -sin, 0.0)
    sin_b = jnp.where(first, 0.0, sin)
    sin_t = jnp.where(first, -sin, sin)
    lane_tile = lambda a: jnp.tile(a, (1, LANES // HD_A))
    return lane_tile(cos), lane_tile(sin_a), lane_tile(sin_b), cos.T, sin_t.T


def _gate_lanes(a):
    lead = a.shape[:-2]
    rep = jnp.broadcast_to(a[..., :, None, :], lead + (2, GATE_COPIES, NH_M)).reshape(lead + (2 * GATE_DIR_LANES,))
    return jnp.concatenate([rep, jnp.zeros(lead + (LANES - 2 * GATE_DIR_LANES,), a.dtype)], axis=-1)


def _gate_layout(a):
    a4 = a.reshape(a.shape[:-1] + (2, 2, NH_M))
    return jnp.concatenate([_gate_lanes(a4[..., :, 0, :]), _gate_lanes(a4[..., :, 1, :])], axis=-1)


def _gate_selector():
    r = jnp.arange(LANES)[:, None]
    c = jnp.arange(2 * NH_M * 2 * LANES)[None, :]
    hd, j = c // (2 * LANES), c % (2 * LANES)
    rd, rk, rh = r // GATE_DIR_LANES, (r % GATE_DIR_LANES) // NH_M, r % NH_M
    hit = (r < 2 * GATE_DIR_LANES) & (rd == hd // NH_M) & (rh == hd % NH_M) & ((j < LANES) == (rk < 2))
    return hit.astype(BF16)


def _pair_layout_cols(w):
    d_in = w.shape[0]
    w5 = w.reshape(d_in, NKV_A // 2, 2, G_A, HD_A)
    return jnp.transpose(w5, (0, 1, 3, 2, 4)).reshape(d_in, Q_W)


def _mlstm_init(state_c, state_n, state_m):
    n_rep = jnp.broadcast_to(state_n[..., None].astype(F32), state_n.shape + (DV_M,))
    return state_c.astype(F32), n_rep, _gate_lanes(state_m.astype(F32))[:, None, :]


def kernel(x_prompt, x_sample, state_c, state_n, state_m, cache_k, cache_v, c, c_ctx, w_mod, b_mod, norm_g,
           ffn1_w_gu, ffn1_w_down, ffn2_w_gu, ffn2_w_down, mlstm_w_in, mlstm_b_gate, mlstm_g_head, mlstm_w_out,
           attn_w_in, attn_sink, attn_w_out, final_g):
    bp, n_p, _ = x_prompt.shape
    bs, n_s, _ = x_sample.shape
    xp = x_prompt.reshape(bp * n_p, D_MODEL)
    xs = x_sample.reshape(bs * n_s, D_MODEL)
    tm_p, tm_s = 512, 512

    cond = jnp.concatenate([c_ctx[None, :], c], axis=0)
    cond = jnp.pad(cond, ((0, COND_ROWS - cond.shape[0]), (0, 0)))
    mod_all = _mod_call(cond, w_mod, b_mod).reshape(DEPTH, COND_ROWS, N_MOD, D_MODEL)

    outs = {}
    for l in range(DEPTH):
        mod_p = mod_all[l, 0:1]
        mod_s = mod_all[l, 1:1 + bs]
        g = norm_g[l]
        xp = _ffn_call(xp, mod_p, g[0:1], ffn1_w_gu, ffn1_w_down, l, 0, n_p, tm_p)
        xs = _ffn_call(xs, mod_s, g[0:1], ffn1_w_gu, ffn1_w_down, l, 0, n_s, tm_s)
        i = l // 2
        if l % 2 == 0:
            w_in = mlstm_w_in[i]
            wq = w_in[:, :QK_W].astype(BF16)
            wkt = w_in[:, QK_W:2 * QK_W].T.astype(BF16)
            wvo = w_in[:, 2 * QK_W:2 * QK_W + 2 * V_W].astype(BF16)
            wg = _gate_layout(w_in[:, 2 * QK_W + 2 * V_W:]).astype(BF16)
            bg = _gate_layout(mlstm_b_gate[i].astype(F32))[None, :]
            gh = mlstm_g_head[i].astype(F32)[None, :]
            w_out = mlstm_w_out[i].astype(BF16)
            sel = _gate_selector()
            streams = [("p", xp, mod_p, n_p, tm_p, bp, None),
                       ("s", xs, mod_s, n_s, tm_s, bs, _mlstm_init(state_c[:, i], state_n[:, i], state_m[:, i]))]
            res = {}
            for tag, x, mod, n_tok, tm, bsz, init in streams:
                q, kt, v, o, gt = _mlstm_in_call(x, mod, g[1:2], wq, wkt, wvo, wg, bg, n_tok, tm)
                hf, hb, c_fin, n_fin, m_fin = _mlstm_core_call(q, kt, v, gt, sel, init, bsz, n_tok)
                res[tag] = ((hf, hb, o, gh, w_out), c_fin, n_fin, m_fin)
            mix_p, c_fin, n_fin, m_fin = res["p"]
            mix_s = res["s"][0]
            mix_key = "mlstm"
            dt = x_prompt.dtype
            m_heads = m_fin[:, 0, :2 * GATE_DIR_LANES].reshape(bp, 2, GATE_COPIES, NH_M)[:, :, 0, :]
            outs.setdefault("c", []).append(c_fin.astype(dt))
            outs.setdefault("n", []).append(n_fin[..., 0].astype(dt))
            outs.setdefault("m", []).append(m_heads.astype(dt))
        else:
            w_in = attn_w_in[i]
            wqt = _pair_layout_cols(w_in[:, :Q_W]).T.astype(BF16)
            wk = w_in[:, Q_W:Q_W + KV_W].astype(BF16)
            wv = w_in[:, Q_W + KV_W:].astype(BF16)
            wvt = w_in[:, Q_W + KV_W:].T.astype(BF16)
            w_out = _pair_layout_cols(attn_w_out[i].T).T.astype(BF16)
            sink = attn_sink[i].astype(F32)[None, :]
            qt, k, vt, kf, vf = _attn_in_call(xp, mod_p, g[1:2], wqt, wk, wvt, n_p, tm_p, wv=wv)
            mix_p = (_attn_context_call(qt, k, vt, sink, bp, n_p), w_out)
            mix_key = "attn"
            outs.setdefault("k", []).append(kf.reshape(bp, n_p, NKV_A, HD_A))
            outs.setdefault("v", []).append(vf.reshape(bp, n_p, NKV_A, HD_A))
            qt, k, vt = _attn_in_call(xs, mod_s, g[1:2], wqt, wk, wvt, n_s, tm_s, rope_tabs=_rope_tables(n_s))
            n_ctx = cache_k.shape[2]
            kx = cache_k[:, i].reshape(bs, n_ctx, KV_W).astype(BF16)
            vtx = jnp.swapaxes(cache_v[:, i].reshape(bs, n_ctx, KV_W), 1, 2).astype(BF16)
            mix_s = (_attn_latent_call(qt, k, vt, kx, vtx, sink, bs, n_s), w_out)
        fg = final_g[None, :] if l == DEPTH - 1 else None
        xp = _ffn_call(xp, mod_p, g[2:3], ffn2_w_gu, ffn2_w_down, l, 2, n_p, tm_p, final_g=fg, **{mix_key: mix_p})
        xs = _ffn_call(xs, mod_s, g[2:3], ffn2_w_gu, ffn2_w_down, l, 2, n_s, tm_s, final_g=fg, **{mix_key: mix_s})

    return (xp.reshape(bp, n_p, D_MODEL), xs.reshape(bs, n_s, D_MODEL),
            jnp.stack(outs["c"], axis=1), jnp.stack(outs["n"], axis=1), jnp.stack(outs["m"], axis=1),
            jnp.stack(outs["k"], axis=1), jnp.stack(outs["v"], axis=1))
```

```python
import functools

import jax
import jax.numpy as jnp
from jax import lax
from jax.experimental import pallas as pl
from jax.experimental.pallas import tpu as pltpu

D_MODEL = 1024
DEPTH = 2
GRID_W = 64
D_FF = 2816
FFN_RES = 0.5
NH_M = 8
DK_M = 64
DV_M = 128
NH_A = 16
NKV_A = 4
G_A = NH_A // NKV_A
HD_A = 64
WINDOW = 128
ATTN_BLOCK = 128
ROPE_THETA = 10000.0
EPS = 1e-6
NEG_INF = -1e30
N_MOD = 9

QK_W = NH_M * DK_M
V_W = NH_M * DV_M
KV_W = NKV_A * HD_A
Q_W = NH_A * HD_A

LANES = 128
MLSTM_L = 128
MLSTM_STEP_CHUNKS = 4
FFN_CHUNK = 256
FFN_W_STEPS = 11
FFN_WGU_COLS = 2 * D_FF // FFN_W_STEPS
FFN_WD_ROWS = D_FF // FFN_W_STEPS
ONES_ROWS = 16
LOG2_E = 1.4426950408889634
GATE_COPIES = 5
GATE_DIR_LANES = GATE_COPIES * NH_M
COND_ROWS = 16
VMEM_CAP = 60000 * 1024

F32 = jnp.float32
BF16 = jnp.bfloat16


def _cparams(sem, vmem_bytes):
    return pltpu.CompilerParams(dimension_semantics=sem, vmem_limit_bytes=min(int(vmem_bytes), VMEM_CAP))


def _dot(a, b):
    return jnp.dot(a, b, preferred_element_type=F32)


def _dot_nt(a, b):
    return lax.dot_general(a, b, (((1,), (1,)), ((), ())), preferred_element_type=F32)


def _rms(x):
    return x * lax.rsqrt(jnp.mean(x * x, axis=-1, keepdims=True) + EPS)


def _sigmoid(x):
    return 1.0 / (1.0 + jnp.exp(-x))


def _mod_norm(x, g, shift, scale):
    return (_rms(x) * g) * (1.0 + scale) + shift


def _mod_kernel(c_ref, w_ref, b_ref, o_ref):
    c = c_ref[...]
    s = (c * _sigmoid(c)).astype(BF16)
    o_ref[0] = _dot(s, w_ref[0].astype(BF16)) + b_ref[0]


def _mod_call(cond, w_mod, b_mod):
    tn = D_MODEL
    n_out = N_MOD * D_MODEL
    return pl.pallas_call(
        _mod_kernel,
        grid=(DEPTH, n_out // tn),
        in_specs=[
            pl.BlockSpec((COND_ROWS, D_MODEL), lambda l, n: (0, 0)),
            pl.BlockSpec((1, D_MODEL, tn), lambda l, n: (l, 0, n)),
            pl.BlockSpec((1, 1, tn), lambda l, n: (l, 0, n)),
        ],
        out_specs=pl.BlockSpec((1, COND_ROWS, tn), lambda l, n: (l, 0, n)),
        out_shape=jax.ShapeDtypeStruct((DEPTH, COND_ROWS, n_out), F32),
        compiler_params=_cparams(("parallel", "parallel"), 4 * (2 * D_MODEL * tn * 4)),
        name="adaln_mod",
    )(cond, w_mod, b_mod.reshape(DEPTH, 1, n_out))


def _mod_spec(mod, tiles_per_seq, tile=lambda i: i):
    if mod.shape[0] == 1:
        return pl.BlockSpec((1, N_MOD, D_MODEL), lambda i: (0, 0, 0))
    return pl.BlockSpec((1, N_MOD, D_MODEL), lambda i: (tile(i) // tiles_per_seq, 0, 0))


def _row_spec(width):
    return pl.BlockSpec((1, width), lambda i: (0, 0))


def _resident(shape):
    return pl.BlockSpec(shape, lambda i: (0,) * len(shape), pipeline_mode=pl.Buffered(1))


def _mlstm_gated_heads(hf_ref, hb_ref, o_ref, gh_ref):
    hs = hf_ref[...].astype(F32) + hb_ref[...].astype(F32)
    parts = [_rms(hs[:, h * DV_M:(h + 1) * DV_M]) for h in range(NH_M)]
    hn = jnp.concatenate(parts, axis=1) * gh_ref[...]
    return (hn * _sigmoid(o_ref[...].astype(F32))).astype(BF16)


def _ffn_kernel(*refs, j, final, mixer):
    x_ref, mod_ref, g_ref, wgu32_ref, wd32_ref, fg_ref = refs[:6]
    o_ref, wgu_ref, wd_ref = refs[-3:]
    step = pl.program_id(0)

    @pl.when(step < FFN_W_STEPS)
    def _():
        c0 = pl.multiple_of(step * FFN_WGU_COLS, FFN_WGU_COLS)
        r0 = pl.multiple_of(step * FFN_WD_ROWS, FFN_WD_ROWS)
        wgu_ref[:, pl.ds(c0, FFN_WGU_COLS)] = wgu32_ref[0].astype(BF16)
        wd_ref[pl.ds(r0, FFN_WD_ROWS), :] = wd32_ref[0].astype(BF16)

    @pl.when(step >= FFN_W_STEPS)
    def _():
        x = x_ref[...]
        mod = mod_ref[0]
        if mixer == "mlstm":
            hf_ref, hb_ref, og_ref, gh_ref, wo_ref = refs[6:11]
            x = x + mod[5:6] * _dot(_mlstm_gated_heads(hf_ref, hb_ref, og_ref, gh_ref), wo_ref[...])
        elif mixer == "attn":
            a_ref, wo_ref = refs[6:8]
            x = x + mod[5:6] * _dot(a_ref[...], wo_ref[...])
        shift, scale, gate = mod[3 * j:3 * j + 1], mod[3 * j + 1:3 * j + 2], mod[3 * j + 2:3 * j + 3]
        hb = _mod_norm(x, g_ref[...], shift, scale).astype(BF16)
        acc = jnp.zeros(x.shape, F32)
        for c in range(D_FF // FFN_CHUNK):
            lo = c * FFN_CHUNK
            gg = _dot(hb, wgu_ref[:, lo:lo + FFN_CHUNK])
            uu = _dot(hb, wgu_ref[:, D_FF + lo:D_FF + lo + FFN_CHUNK])
            act = (gg * _sigmoid(gg) * uu).astype(BF16)
            acc = acc + _dot(act, wd_ref[lo:lo + FFN_CHUNK, :])
        y = x + (FFN_RES * gate) * acc
        if final:
            y = _rms(y) * fg_ref[...]
        o_ref[...] = y


def _ffn_call(x, mod, g, wgu, wd, layer, j, seq_len, tm, final_g=None, mlstm=None, attn=None):
    t = x.shape[0]
    final = final_g is not None
    fg = final_g if final else g
    tile = lambda i: jnp.maximum(i - FFN_W_STEPS, 0)
    wstep = lambda i: jnp.minimum(i, FFN_W_STEPS - 1)
    tok = lambda width: pl.BlockSpec((tm, width), lambda i: (tile(i), 0))
    in_specs = [tok(D_MODEL), _mod_spec(mod, seq_len // tm, tile), _row_spec(D_MODEL),
                pl.BlockSpec((1, D_MODEL, FFN_WGU_COLS), lambda i: (layer, 0, wstep(i))),
                pl.BlockSpec((1, FFN_WD_ROWS, D_MODEL), lambda i: (layer, wstep(i), 0)), _row_spec(D_MODEL)]
    args = [x, mod, g, wgu, wd, fg]
    mixer, extra = None, 0
    if mlstm is not None:
        mixer = "mlstm"
        in_specs += [tok(V_W), tok(V_W), tok(V_W), _row_spec(V_W), _resident((V_W, D_MODEL))]
        args += list(mlstm)
        extra = 2 * V_W * D_MODEL + 2 * tm * 3 * V_W * 2 + 3 * tm * V_W * 4
    elif attn is not None:
        mixer = "attn"
        in_specs += [tok(Q_W), _resident((Q_W, D_MODEL))]
        args += list(attn)
        extra = 2 * Q_W * D_MODEL + 2 * tm * Q_W * 2 + tm * Q_W * 4
    w_stream = 2 * 4 * (D_MODEL * FFN_WGU_COLS + FFN_WD_ROWS * D_MODEL)
    vmem = 2 * (D_MODEL * 2 * D_FF + D_FF * D_MODEL) + w_stream + 12 * tm * D_MODEL * 4 + extra + (8 << 20)
    return pl.pallas_call(
        functools.partial(_ffn_kernel, j=j, final=final, mixer=mixer),
        grid=(FFN_W_STEPS + t // tm,),
        in_specs=in_specs,
        out_specs=tok(D_MODEL),
        out_shape=jax.ShapeDtypeStruct((t, D_MODEL), F32),
        scratch_shapes=[pltpu.VMEM((D_MODEL, 2 * D_FF), BF16), pltpu.VMEM((D_FF, D_MODEL), BF16)],
        compiler_params=_cparams(("arbitrary",), vmem),
        name=f"ffn{j // 2 + 1}" + (f"_{mixer}" if mixer else ""),
    )(*args)


def _scan_rows(x, op, fill, reverse):
    n = x.shape[0]
    idx = lax.broadcasted_iota(jnp.int32, x.shape, 0)
    s = 1
    while s < n:
        if reverse:
            x = op(x, jnp.where(idx < n - s, pltpu.roll(x, n - s, axis=0), fill))
        else:
            x = op(x, jnp.where(idx >= s, pltpu.roll(x, s, axis=0), fill))
        s *= 2
    return x


def _mlstm_in_kernel(x_ref, mod_ref, g_ref, wq_ref, wkt_ref, wvo_ref, wg_ref, bg_ref,
                     q_ref, kt_ref, v_ref, o_ref, scan_ref):
    mod = mod_ref[0]
    hb = _mod_norm(x_ref[...], g_ref[...], mod[3:4], mod[4:5]).astype(BF16)
    gt = _dot(hb, wg_ref[...]) + bg_ref[...]
    q_ref[...] = (_dot(hb, wq_ref[...]) * DK_M ** -0.5).astype(BF16)
    kt_ref[...] = _dot_nt(wkt_ref[...], hb).astype(BF16)
    vo = _dot(hb, wvo_ref[...])
    v_ref[...] = vo[:, :V_W].astype(BF16)
    o_ref[...] = vo[:, V_W:].astype(BF16)
    n = MLSTM_L
    fwd = lax.broadcasted_iota(jnp.int32, (n, LANES), 1) < GATE_DIR_LANES
    for c in range(gt.shape[0] // n):
        rows = slice(c * n, (c + 1) * n)
        li = gt[rows, :LANES]
        fx = gt[rows, LANES:]
        lf = jnp.minimum(fx, 0.0) - jnp.log(1.0 + jnp.exp(-jnp.abs(fx)))
        pre = _scan_rows(lf, jnp.add, 0.0, False)
        b = jnp.where(fwd, pre, pre[n - 1:n, :] - pre + lf)
        a = li - b
        scan_ref[rows, 0:LANES] = a
        scan_ref[rows, LANES:2 * LANES] = b
        scan_ref[rows, 2 * LANES:] = jnp.where(fwd, _scan_rows(a, jnp.maximum, -jnp.inf, False),
                                               _scan_rows(a, jnp.maximum, -jnp.inf, True))


def _mlstm_in_call(x, mod, g, wq, wkt, wvo, wg, bg, seq_len, tm):
    t = x.shape[0]
    w_bytes = 2 * (D_MODEL * (2 * QK_W + 2 * V_W + 2 * LANES))
    vmem = w_bytes + 2 * tm * (D_MODEL * 4 + (2 * QK_W + 2 * V_W) * 2 + 2 * LANES * 4) + 6 * tm * D_MODEL * 4 + (4 << 20)
    return pl.pallas_call(
        _mlstm_in_kernel,
        grid=(t // tm,),
        in_specs=[
            pl.BlockSpec((tm, D_MODEL), lambda i: (i, 0)),
            _mod_spec(mod, seq_len // tm),
            _row_spec(D_MODEL),
            _resident((D_MODEL, QK_W)),
            _resident((QK_W, D_MODEL)),
            _resident((D_MODEL, 2 * V_W)),
            _resident((D_MODEL, 2 * LANES)),
            _row_spec(2 * LANES),
        ],
        out_specs=[
            pl.BlockSpec((tm, QK_W), lambda i: (i, 0)),
            pl.BlockSpec((QK_W, tm), lambda i: (0, i)),
            pl.BlockSpec((tm, V_W), lambda i: (i, 0)),
            pl.BlockSpec((tm, V_W), lambda i: (i, 0)),
            pl.BlockSpec((tm, 3 * LANES), lambda i: (i, 0)),
        ],
        out_shape=[
            jax.ShapeDtypeStruct((t, QK_W), BF16),
            jax.ShapeDtypeStruct((QK_W, t), BF16),
            jax.ShapeDtypeStruct((t, V_W), BF16),
            jax.ShapeDtypeStruct((t, V_W), BF16),
            jax.ShapeDtypeStruct((t, 3 * LANES), F32),
        ],
        compiler_params=_cparams(("parallel",), vmem),
        name="mlstm_in",
    )(x, mod, g, wq, wkt, wvo, wg, bg)


def _bf16_part(x):
    return x.astype(BF16).astype(F32)


def _mlstm_gates(d, scan_ref, rows, m):
    n = MLSTM_L
    a, b, cm = scan_ref[rows, 0:LANES], scan_ref[rows, LANES:2 * LANES], scan_ref[rows, 2 * LANES:]
    rev = d == 1
    mt = jnp.maximum(cm, m) * LOG2_E
    u_hi = _bf16_part(mt)
    u_lo = _bf16_part(mt - u_hi)
    z = -(b * LOG2_E + (u_hi + u_lo))
    z_hi = _bf16_part(z)
    z_mid = _bf16_part(z - z_hi)
    z_lo = _bf16_part(z - z_hi - z_mid)
    lane = lax.broadcasted_iota(jnp.int32, mt.shape, 1)
    k = (lane - d * GATE_DIR_LANES) // NH_M
    terms = jnp.where(k == 0, u_hi, jnp.where(k == 1, u_lo, jnp.where(k == 2, z_hi, jnp.where(k == 3, z_mid, z_lo))))
    end = 0 if rev else n - 1
    mx = jnp.maximum(m, cm[end:end + 1, :])
    keep = jnp.exp(m - mx)
    m_new = b[end:end + 1, :] + mx
    return terms, (a * LOG2_E).T, jnp.exp(a - mx).T, keep, m_new


def _mlstm_heads(dirs, c_sc, spread, m2):
    n = MLSTM_L
    row = lax.broadcasted_iota(jnp.int32, (n, n), 0)
    col = lax.broadcasted_iota(jnp.int32, (n, n), 1)
    ones = jnp.ones((n, LANES), BF16)
    heads = [(dd, h) for dd in dirs for h in range(NH_M)]
    lhs = []
    for (d, q_ref, kt_ref, _, _, rows, a_t, _, _), h in heads:
        c = d * GATE_DIR_LANES + h
        base = (d * NH_M + h) * 2 * LANES
        u = spread[:, base:base + LANES]
        qh = q_ref[rows, h * DK_M:(h + 1) * DK_M]
        e = jnp.exp2(a_t[c:c + 1, :] - u)
        mask = (col >= row) if d == 1 else (col <= row)
        w = jnp.where(mask, _dot(qh, kt_ref[h * DK_M:(h + 1) * DK_M, rows]) * e, 0.0).astype(BF16)
        inter = jnp.exp2(m2[:, c:c + 1] - u)
        qi = (inter[:, :DK_M] * qh.astype(F32)).astype(BF16)
        lhs.append(jnp.concatenate([w, qi], axis=1))
    for ((d, _, _, v_ref, h_ref, rows, _, _, _), h), wq in zip(heads, lhs):
        base = (d * NH_M + h) * 2 * LANES
        zz = spread[:, base + LANES:base + 2 * LANES]
        vext = jnp.concatenate([v_ref[rows, h * DV_M:(h + 1) * DV_M], ones], axis=1)
        r = _dot(wq, jnp.concatenate([vext, c_sc[d, h].astype(BF16)], axis=0))
        den = jnp.maximum(jnp.abs(r[:, DV_M:]), jnp.exp2(zz))
        h_ref[rows, h * DV_M:(h + 1) * DV_M] = (r[:, :DV_M] / den).astype(BF16)
    for (d, _, kt_ref, v_ref, _, rows, _, wend_t, keep), h in heads:
        c = d * GATE_DIR_LANES + h
        vext = jnp.concatenate([v_ref[rows, h * DV_M:(h + 1) * DV_M], ones], axis=1)
        kts = (kt_ref[h * DK_M:(h + 1) * DK_M, rows].astype(F32) * wend_t[c:c + 1, :]).astype(BF16)
        c_sc[d, h] = keep[:, c:c + 1] * c_sc[d, h] + _dot(kts, vext)


def _mlstm_kernel(*refs, zero_init):
    qf_ref, ktf_ref, vf_ref, scanf_ref, qb_ref, ktb_ref, vb_ref, scanb_ref, sel_ref = refs[:9]
    hf_ref, hb_ref, cout_ref, nout_ref, mout_ref, c_sc, m_sc = refs[-7:]
    i = pl.program_id(1)

    @pl.when(i == 0)
    def _():
        if zero_init:
            c_sc[...] = jnp.zeros(c_sc.shape, F32)
            m_sc[...] = jnp.zeros(m_sc.shape, F32)
        else:
            c0_ref, n0_ref, m0_ref = refs[9:12]
            c_sc[:, :, :, :DV_M] = c0_ref[0]
            c_sc[:, :, :, DV_M:] = n0_ref[0]
            m_sc[...] = m0_ref[0]

    nsub = qf_ref.shape[0] // MLSTM_L
    for sub in range(nsub):
        rows_f = slice(sub * MLSTM_L, (sub + 1) * MLSTM_L)
        rows_b = slice((nsub - 1 - sub) * MLSTM_L, (nsub - sub) * MLSTM_L)
        m = m_sc[...]
        terms_f, at_f, wend_f, keep_f, mnew_f = _mlstm_gates(0, scanf_ref, rows_f, m)
        terms_b, at_b, wend_b, keep_b, mnew_b = _mlstm_gates(1, scanb_ref, rows_b, m)
        lane = lax.broadcasted_iota(jnp.int32, terms_f.shape, 1)
        packed = jnp.where(lane < GATE_DIR_LANES, terms_f, jnp.where(lane < 2 * GATE_DIR_LANES, terms_b, 0.0))
        spread = _dot(packed.astype(BF16), sel_ref[...])
        m2 = m * LOG2_E
        _mlstm_heads([(0, qf_ref, ktf_ref, vf_ref, hf_ref, rows_f, at_f, wend_f, keep_f),
                      (1, qb_ref, ktb_ref, vb_ref, hb_ref, rows_b, at_b, wend_b, keep_b)], c_sc, spread, m2)
        lane_m = lax.broadcasted_iota(jnp.int32, m.shape, 1)
        m_sc[...] = jnp.where(lane_m < GATE_DIR_LANES, mnew_f, jnp.where(lane_m < 2 * GATE_DIR_LANES, mnew_b, m))

    @pl.when(i == pl.num_programs(1) - 1)
    def _():
        cout_ref[0] = c_sc[:, :, :, :DV_M]
        nout_ref[0] = c_sc[:, :, :, DV_M:]
        mout_ref[0] = m_sc[...]


def _mlstm_core_call(q, kt, v, scans, sel, init, bsz, seq_len):
    t = q.shape[0]
    rows = min(MLSTM_STEP_CHUNKS * MLSTM_L, seq_len)
    nc = seq_len // rows
    fwd = lambda b, i: (b * nc + i, 0)
    bwd = lambda b, i: (b * nc + nc - 1 - i, 0)
    fwd_t = lambda b, i: (0, b * nc + i)
    bwd_t = lambda b, i: (0, b * nc + nc - 1 - i)

    def specs(row_map, col_map):
        return [
            pl.BlockSpec((rows, QK_W), row_map),
            pl.BlockSpec((QK_W, rows), col_map),
            pl.BlockSpec((rows, V_W), row_map),
            pl.BlockSpec((rows, 3 * LANES), row_map),
        ]

    half_shape = (2, NH_M, DK_M, DV_M)
    half_spec = pl.BlockSpec((1,) + half_shape, lambda b, i: (b, 0, 0, 0, 0))
    m_spec = pl.BlockSpec((1, 1, LANES), lambda b, i: (b, 0, 0))
    state_bytes = 2 * NH_M * DK_M * 2 * DV_M * 4
    sel_shape = (LANES, 2 * NH_M * 2 * LANES)
    in_specs = specs(fwd, fwd_t) + specs(bwd, bwd_t) + [pl.BlockSpec(sel_shape, lambda b, i: (0, 0))]
    args = [q, kt, v, scans, q, kt, v, scans, sel]
    if init is not None:
        in_specs += [half_spec, half_spec, m_spec]
        args += list(init)
    vmem = 6 * state_bytes + 8 * rows * (2 * QK_W + 2 * V_W) * 2 + 4 * sel_shape[0] * sel_shape[1] + (16 << 20)
    return pl.pallas_call(
        functools.partial(_mlstm_kernel, zero_init=init is None),
        grid=(bsz, nc),
        in_specs=in_specs,
        out_specs=[pl.BlockSpec((rows, V_W), fwd), pl.BlockSpec((rows, V_W), bwd), half_spec, half_spec, m_spec],
        out_shape=[
            jax.ShapeDtypeStruct((t, V_W), BF16),
            jax.ShapeDtypeStruct((t, V_W), BF16),
            jax.ShapeDtypeStruct((bsz,) + half_shape, F32),
            jax.ShapeDtypeStruct((bsz,) + half_shape, F32),
            jax.ShapeDtypeStruct((bsz, 1, LANES), F32),
        ],
        scratch_shapes=[pltpu.VMEM((2, NH_M, DK_M, 2 * DV_M), F32), pltpu.VMEM((1, LANES), F32)],
        compiler_params=_cparams(("parallel", "arbitrary"), vmem),
        name="mlstm_core",
    )(*args)


def _rope_rows_swap(x):
    q = HD_A // 4
    parts = []
    for base in range(0, x.shape[0], 2 * q):
        parts += [x[base + q:base + 2 * q], x[base:base + q]]
    return jnp.concatenate(parts, axis=0)


def _attn_in_kernel(*refs, rope, emit_f32):
    x_ref, mod_ref, g_ref, wqt_ref, wk_ref, wvt_ref = refs[:6]
    pos = 6
    if rope:
        cos_ref, sina_ref, sinb_ref, cost_ref, sint_ref = refs[pos:pos + 5]
        pos += 5
    if emit_f32:
        wv_ref = refs[pos]
        pos += 1
    qt_ref, k_ref, vt_ref = refs[pos:pos + 3]
    pos += 3
    mod = mod_ref[0]
    hb = _mod_norm(x_ref[...], g_ref[...], mod[3:4], mod[4:5]).astype(BF16)
    qt = _dot_nt(wqt_ref[...], hb)
    k = _dot(hb, wk_ref[...])
    blk = ATTN_BLOCK
    vt = _dot_nt(wvt_ref[...], hb).astype(BF16)
    for tb in range(vt.shape[1] // blk):
        vt_ref[tb] = vt[:, tb * blk:(tb + 1) * blk]
    if emit_f32:
        kf_ref, vf_ref = refs[pos:pos + 2]
        kf_ref[...] = k
        vf_ref[...] = _dot(hb, wv_ref[...])
    for gi in range(Q_W // LANES):
        qg = qt[gi * LANES:(gi + 1) * LANES, :]
        if rope:
            cos_t = jnp.concatenate([cost_ref[...]] * (LANES // HD_A), axis=0)
            sin_t = jnp.concatenate([sint_ref[...]] * (LANES // HD_A), axis=0)
            qg = qg * cos_t + _rope_rows_swap(qg) * sin_t
        qg = (qg * (HD_A ** -0.5 * LOG2_E)).astype(BF16)
        p, g = gi // G_A, gi % G_A
        for tb in range(qt.shape[1] // blk):
            qt_ref[tb, p, :, g * blk:(g + 1) * blk] = qg[:, tb * blk:(tb + 1) * blk]
    for gi in range(KV_W // LANES):
        kg = k[:, gi * LANES:(gi + 1) * LANES]
        if rope:
            kg = (kg * cos_ref[...] + pltpu.roll(kg, LANES - HD_A // 4, axis=1) * sina_ref[...]
                  + pltpu.roll(kg, HD_A // 4, axis=1) * sinb_ref[...])
        k_ref[:, gi * LANES:(gi + 1) * LANES] = kg.astype(BF16)


def _attn_in_call(x, mod, g, wqt, wk, wvt, seq_len, tm, rope_tabs=None, wv=None):
    t = x.shape[0]
    rope = rope_tabs is not None
    emit_f32 = wv is not None
    tiles = seq_len // tm
    tok = lambda width: pl.BlockSpec((tm, width), lambda i: (i, 0))
    in_specs = [tok(D_MODEL), _mod_spec(mod, tiles), _row_spec(D_MODEL),
                _resident((Q_W, D_MODEL)), _resident((D_MODEL, KV_W)), _resident((KV_W, D_MODEL))]
    args = [x, mod, g, wqt, wk, wvt]
    if rope:
        pos_row = lambda i: (i % tiles, 0)
        pos_col = lambda i: (0, i % tiles)
        in_specs += [pl.BlockSpec((tm, LANES), pos_row)] * 3 + [pl.BlockSpec((HD_A, tm), pos_col)] * 2
        args += list(rope_tabs)
    if emit_f32:
        in_specs.append(_resident((D_MODEL, KV_W)))
        args.append(wv)
    nblk = tm // ATTN_BLOCK
    qt_slab = (NKV_A // 2, LANES, G_A * ATTN_BLOCK)
    vt_slab = (KV_W, ATTN_BLOCK)
    out_specs = [pl.BlockSpec((nblk,) + qt_slab, lambda i: (i, 0, 0, 0)), tok(KV_W),
                 pl.BlockSpec((nblk,) + vt_slab, lambda i: (i, 0, 0))]
    out_shape = [jax.ShapeDtypeStruct((t // ATTN_BLOCK,) + qt_slab, BF16), jax.ShapeDtypeStruct((t, KV_W), BF16),
                 jax.ShapeDtypeStruct((t // ATTN_BLOCK,) + vt_slab, BF16)]
    if emit_f32:
        out_specs += [tok(KV_W), tok(KV_W)]
        out_shape += [jax.ShapeDtypeStruct((t, KV_W), F32)] * 2
    vmem = 2 * D_MODEL * (Q_W + 3 * KV_W) + 2 * tm * (D_MODEL * 4 + Q_W * 2 + 8 * KV_W + 5 * LANES * 4) \
        + 8 * tm * D_MODEL * 4 + (4 << 20)
    return pl.pallas_call(
        functools.partial(_attn_in_kernel, rope=rope, emit_f32=emit_f32),
        grid=(t // tm,),
        in_specs=in_specs,
        out_specs=out_specs,
        out_shape=out_shape,
        compiler_params=_cparams(("parallel",), vmem),
        name="attn_in",
    )(*args)


def _block_attention(sink_ref, qt_ref, o_ref, pair_keys, mask_of):
    blk = ATTN_BLOCK
    cols_all = G_A * blk
    col_g = lax.broadcasted_iota(jnp.int32, (1, cols_all), 1) // blk
    staged = []
    for p, keys in enumerate(pair_keys):
        qp = qt_ref[0, p]
        row = lax.broadcasted_iota(jnp.int32, qp.shape, 0)
        for e in range(2):
            kv = 2 * p + e
            qm = jnp.where((row >= e * HD_A) & (row < (e + 1) * HD_A), qp, jnp.zeros_like(qp))
            sink = jnp.full((1, cols_all), sink_ref[0, kv * G_A], F32)
            for g in range(1, G_A):
                sink = jnp.where(col_g == g, sink_ref[0, kv * G_A + g], sink)
            sink = sink * LOG2_E
            scores = []
            m = sink
            for i, (k, _) in enumerate(keys):
                s = _dot(k, qm)
                msk = mask_of(i)
                s = s if msk is None else jnp.where(msk, s, NEG_INF)
                m = jnp.maximum(m, jnp.max(s, axis=0, keepdims=True))
                scores.append(s)
            staged.append((scores, m, sink))
    outs = []
    for idx, (scores, m, sink) in enumerate(staged):
        acc = None
        for s, (_, vt) in zip(scores, pair_keys[idx // 2]):
            vx = jnp.concatenate([vt, jnp.ones((ONES_ROWS, vt.shape[1]), BF16)], axis=0)
            pv = _dot(vx, jnp.exp2(s - m).astype(BF16))
            acc = pv if acc is None else acc + pv
        outs.append(acc[:LANES] / (acc[LANES:LANES + 1] + jnp.exp2(sink - m)))
    row_o = lax.broadcasted_iota(jnp.int32, outs[0].shape, 0)
    for p in range(len(pair_keys)):
        a_t = jnp.where(row_o < HD_A, outs[2 * p], outs[2 * p + 1])
        for g in range(G_A):
            c0 = (p * G_A + g) * LANES
            o_ref[:, c0:c0 + LANES] = a_t[:, g * blk:(g + 1) * blk].T.astype(BF16)


def _attn_latent_kernel(sink_ref, qt_ref, kp_ref, kc_ref, kn_ref, vtp_ref, vtc_ref, vtn_ref, kx_ref, vtx_ref,
                        o_ref, *, n_tok):
    j = pl.program_id(1)
    blk = ATTN_BLOCK
    shape = (3 * blk, G_A * blk)
    s_pos = (j - 1) * blk + lax.broadcasted_iota(jnp.int32, shape, 0)
    t_pos = j * blk + lax.broadcasted_iota(jnp.int32, shape, 1) % blk
    local = (jnp.abs(t_pos - s_pos) <= WINDOW) & (s_pos >= 0) & (s_pos < n_tok)
    pair_keys = []
    for p in range(NKV_A // 2):
        pl_ = slice(p * LANES, (p + 1) * LANES)
        k_loc = jnp.concatenate([kp_ref[:, pl_], kc_ref[:, pl_], kn_ref[:, pl_]], axis=0)
        vt_loc = jnp.concatenate([vtp_ref[0, pl_, :], vtc_ref[0, pl_, :], vtn_ref[0, pl_, :]], axis=1)
        pair_keys.append([(k_loc, vt_loc), (kx_ref[0, :, pl_], vtx_ref[0, pl_, :])])
    _block_attention(sink_ref, qt_ref, o_ref, pair_keys, lambda i: local if i == 0 else None)


def _qt_pair_spec(nb):
    return pl.BlockSpec((1, NKV_A // 2, LANES, G_A * ATTN_BLOCK), lambda b, j: (b * nb + j, 0, 0, 0))


def _attn_latent_call(qt, k, vt, kx, vtx, sink, bsz, seq_len):
    t = k.shape[0]
    nb = seq_len // ATTN_BLOCK
    n_ctx = kx.shape[1]
    prev = lambda j: jnp.maximum(j - 1, 0)
    nxt = lambda j: jnp.minimum(j + 1, nb - 1)
    k_spec = lambda f: pl.BlockSpec((ATTN_BLOCK, KV_W), lambda b, j: (b * nb + f(j), 0))
    vt_spec = lambda f: pl.BlockSpec((1, KV_W, ATTN_BLOCK), lambda b, j: (b * nb + f(j), 0, 0))
    same = lambda j: j
    return pl.pallas_call(
        functools.partial(_attn_latent_kernel, n_tok=seq_len),
        grid=(bsz, nb),
        in_specs=[
            pl.BlockSpec(memory_space=pltpu.SMEM),
            _qt_pair_spec(nb),
            k_spec(prev), k_spec(same), k_spec(nxt),
            vt_spec(prev), vt_spec(same), vt_spec(nxt),
            pl.BlockSpec((1, n_ctx, KV_W), lambda b, j: (b, 0, 0)),
            pl.BlockSpec((1, KV_W, n_ctx), lambda b, j: (b, 0, 0)),
        ],
        out_specs=pl.BlockSpec((ATTN_BLOCK, Q_W), lambda b, j: (b * nb + j, 0)),
        out_shape=jax.ShapeDtypeStruct((t, Q_W), BF16),
        compiler_params=_cparams(("parallel", "parallel"), 32 << 20),
        name="attn_latent",
    )(sink, qt, k, k, k, vt, vt, vt, kx, vtx)


def _attn_context_kernel(sink_ref, qt_ref, k_ref, vt_ref, o_ref):
    pair_keys = []
    for p in range(NKV_A // 2):
        pl_ = slice(p * LANES, (p + 1) * LANES)
        vt = jnp.concatenate([vt_ref[c, pl_, :] for c in range(vt_ref.shape[0])], axis=1)
        pair_keys.append([(k_ref[:, pl_], vt)])
    _block_attention(sink_ref, qt_ref, o_ref, pair_keys, lambda i: None)


def _attn_context_call(qt, k, vt, sink, bsz, seq_len):
    t = k.shape[0]
    nb = seq_len // ATTN_BLOCK
    return pl.pallas_call(
        _attn_context_kernel,
        grid=(bsz, nb),
        in_specs=[
            pl.BlockSpec(memory_space=pltpu.SMEM),
            _qt_pair_spec(nb),
            pl.BlockSpec((seq_len, KV_W), lambda b, j: (b, 0)),
            pl.BlockSpec((nb, KV_W, ATTN_BLOCK), lambda b, j: (b, 0, 0)),
        ],
        out_specs=pl.BlockSpec((ATTN_BLOCK, Q_W), lambda b, j: (b * nb + j, 0)),
        out_shape=jax.ShapeDtypeStruct((t, Q_W), BF16),
        compiler_params=_cparams(("parallel", "parallel"), 32 << 20),
        name="attn_context",
    )(sink, qt, k, vt)


def _rope_tables(n_tok):
    quarter = HD_A // 4
    freqs = ROPE_THETA ** (-jnp.arange(quarter, dtype=F32) / quarter)
    pos = jnp.arange(n_tok)
    row = (pos // GRID_W).astype(F32)
    col = (pos % GRID_W).astype(F32)
    ang_r, ang_c = row[:, None] * freqs, col[:, None] * freqs
    ang = jnp.concatenate([ang_r, ang_r, ang_c, ang_c], axis=-1)
    cos, sin = jnp.cos(ang), jnp.sin(ang)
    first = (jnp.arange(HD_A) % (2 * quarter)) < quarter
    sin_a = jnp.where(first, -sin, 0.0)
    sin_b = jnp.where(first, 0.0, sin)
    sin_t = jnp.where(first, -sin, sin)
    lane_tile = lambda a: jnp.tile(a, (1, LANES // HD_A))
    return lane_tile(cos), lane_tile(sin_a), lane_tile(sin_b), cos.T, sin_t.T


def _gate_lanes(a):
    lead = a.shape[:-2]
    rep = jnp.broadcast_to(a[..., :, None, :], lead + (2, GATE_COPIES, NH_M)).reshape(lead + (2 * GATE_DIR_LANES,))
    return jnp.concatenate([rep, jnp.zeros(lead + (LANES - 2 * GATE_DIR_LANES,), a.dtype)], axis=-1)


def _gate_layout(a):
    a4 = a.reshape(a.shape[:-1] + (2, 2, NH_M))
    return jnp.concatenate([_gate_lanes(a4[..., :, 0, :]), _gate_lanes(a4[..., :, 1, :])], axis=-1)


def _gate_selector():
    r = jnp.arange(LANES)[:, None]
    c = jnp.arange(2 * NH_M * 2 * LANES)[None, :]
    hd, j = c // (2 * LANES), c % (2 * LANES)
    rd, rk, rh = r // GATE_DIR_LANES, (r % GATE_DIR_LANES) // NH_M, r % NH_M
    hit = (r < 2 * GATE_DIR_LANES) & (rd == hd // NH_M) & (rh == hd % NH_M) & ((j < LANES) == (rk < 2))
    return hit.astype(BF16)


def _pair_layout_cols(w):
    d_in = w.shape[0]
    w5 = w.reshape(d_in, NKV_A // 2, 2, G_A, HD_A)
    return jnp.transpose(w5, (0, 1, 3, 2, 4)).reshape(d_in, Q_W)


def _mlstm_init(state_c, state_n, state_m):
    n_rep = jnp.broadcast_to(state_n[..., None].astype(F32), state_n.shape + (DV_M,))
    return state_c.astype(F32), n_rep, _gate_lanes(state_m.astype(F32))[:, None, :]


def kernel(x_prompt, x_sample, state_c, state_n, state_m, cache_k, cache_v, c, c_ctx, w_mod, b_mod, norm_g,
           ffn1_w_gu, ffn1_w_down, ffn2_w_gu, ffn2_w_down, mlstm_w_in, mlstm_b_gate, mlstm_g_head, mlstm_w_out,
           attn_w_in, attn_sink, attn_w_out, final_g):
    bp, n_p, _ = x_prompt.shape
    bs, n_s, _ = x_sample.shape
    xp = x_prompt.reshape(bp * n_p, D_MODEL)
    xs = x_sample.reshape(bs * n_s, D_MODEL)
    tm_p, tm_s = 512, 512

    cond = jnp.concatenate([c_ctx[None, :], c], axis=0)
    cond = jnp.pad(cond, ((0, COND_ROWS - cond.shape[0]), (0, 0)))
    mod_all = _mod_call(cond, w_mod, b_mod).reshape(DEPTH, COND_ROWS, N_MOD, D_MODEL)

    outs = {}
    for l in range(DEPTH):
        mod_p = mod_all[l, 0:1]
        mod_s = mod_all[l, 1:1 + bs]
        g = norm_g[l]
        xp = _ffn_call(xp, mod_p, g[0:1], ffn1_w_gu, ffn1_w_down, l, 0, n_p, tm_p)
        xs = _ffn_call(xs, mod_s, g[0:1], ffn1_w_gu, ffn1_w_down, l, 0, n_s, tm_s)
        i = l // 2
        if l % 2 == 0:
            w_in = mlstm_w_in[i]
            wq = w_in[:, :QK_W].astype(BF16)
            wkt = w_in[:, QK_W:2 * QK_W].T.astype(BF16)
            wvo = w_in[:, 2 * QK_W:2 * QK_W + 2 * V_W].astype(BF16)
            wg = _gate_layout(w_in[:, 2 * QK_W + 2 * V_W:]).astype(BF16)
            bg = _gate_layout(mlstm_b_gate[i].astype(F32))[None, :]
            gh = mlstm_g_head[i].astype(F32)[None, :]
            w_out = mlstm_w_out[i].astype(BF16)
            sel = _gate_selector()
            streams = [("p", xp, mod_p, n_p, tm_p, bp, None),
                       ("s", xs, mod_s, n_s, tm_s, bs, _mlstm_init(state_c[:, i], state_n[:, i], state_m[:, i]))]
            res = {}
            for tag, x, mod, n_tok, tm, bsz, init in streams:
                q, kt, v, o, scans = _mlstm_in_call(x, mod, g[1:2], wq, wkt, wvo, wg, bg, n_tok, tm)
                hf, hb, c_fin, n_fin, m_fin = _mlstm_core_call(q, kt, v, scans, sel, init, bsz, n_tok)
                res[tag] = ((hf, hb, o, gh, w_out), c_fin, n_fin, m_fin)
            mix_p, c_fin, n_fin, m_fin = res["p"]
            mix_s = res["s"][0]
            mix_key = "mlstm"
            dt = x_prompt.dtype
            m_heads = m_fin[:, 0, :2 * GATE_DIR_LANES].reshape(bp, 2, GATE_COPIES, NH_M)[:, :, 0, :]
            outs.setdefault("c", []).append(c_fin.astype(dt))
            outs.setdefault("n", []).append(n_fin[..., 0].astype(dt))
            outs.setdefault("m", []).append(m_heads.astype(dt))
        else:
            w_in = attn_w_in[i]
            wqt = _pair_layout_cols(w_in[:, :Q_W]).T.astype(BF16)
            wk = w_in[:, Q_W:Q_W + KV_W].astype(BF16)
            wv = w_in[:, Q_W + KV_W:].astype(BF16)
            wvt = w_in[:, Q_W + KV_W:].T.astype(BF16)
            w_out = _pair_layout_cols(attn_w_out[i].T).T.astype(BF16)
            sink = attn_sink[i].astype(F32)[None, :]
            qt, k, vt, kf, vf = _attn_in_call(xp, mod_p, g[1:2], wqt, wk, wvt, n_p, tm_p, wv=wv)
            mix_p = (_attn_context_call(qt, k, vt, sink, bp, n_p), w_out)
            mix_key = "attn"
            outs.setdefault("k", []).append(kf.reshape(bp, n_p, NKV_A, HD_A))
            outs.setdefault("v", []).append(vf.reshape(bp, n_p, NKV_A, HD_A))
            qt, k, vt = _attn_in_call(xs, mod_s, g[1:2], wqt, wk, wvt, n_s, tm_s, rope_tabs=_rope_tables(n_s))
            n_ctx = cache_k.shape[2]
            kx = cache_k[:, i].reshape(bs, n_ctx, KV_W).astype(BF16)
            vtx = jnp.swapaxes(cache_v[:, i].reshape(bs, n_ctx, KV_W), 1, 2).astype(BF16)
            mix_s = (_attn_latent_call(qt, k, vt, kx, vtx, sink, bs, n_s), w_out)
        fg = final_g[None, :] if l == DEPTH - 1 else None
        xp = _ffn_call(xp, mod_p, g[2:3], ffn2_w_gu, ffn2_w_down, l, 2, n_p, tm_p, final_g=fg, **{mix_key: mix_p})
        xs = _ffn_call(xs, mod_s, g[2:3], ffn2_w_gu, ffn2_w_down, l, 2, n_s, tm_s, final_g=fg, **{mix_key: mix_s})

    return (xp.reshape(bp, n_p, D_MODEL), xs.reshape(bs, n_s, D_MODEL),
            jnp.stack(outs["c"], axis=1), jnp.stack(outs["n"], axis=1), jnp.stack(outs["m"], axis=1),
            jnp.stack(outs["k"], axis=1), jnp.stack(outs["v"], axis=1))
```

```python
import functools

import jax
import jax.numpy as jnp
from jax import lax
from jax.experimental import pallas as pl
from jax.experimental.pallas import tpu as pltpu

D_MODEL = 1024
DEPTH = 2
GRID_W = 64
D_FF = 2816
FFN_RES = 0.5
NH_M = 8
DK_M = 64
DV_M = 128
NH_A = 16
NKV_A = 4
G_A = NH_A // NKV_A
HD_A = 64
WINDOW = 128
ATTN_BLOCK = 128
ROPE_THETA = 10000.0
EPS = 1e-6
NEG_INF = -1e30
N_MOD = 9

QK_W = NH_M * DK_M
V_W = NH_M * DV_M
KV_W = NKV_A * HD_A
Q_W = NH_A * HD_A

LANES = 128
MLSTM_L = 128
MLSTM_STEP_CHUNKS = 4
FFN_CHUNK = 256
FFN_W_STEPS = 11
FFN_WGU_COLS = 2 * D_FF // FFN_W_STEPS
FFN_WD_ROWS = D_FF // FFN_W_STEPS
ATTN_STEP_BLOCKS = 4
ONES_ROWS = 16
LOG2_E = 1.4426950408889634
GATE_COPIES = 5
GATE_DIR_LANES = GATE_COPIES * NH_M
COND_ROWS = 16
VMEM_CAP = 60000 * 1024
COMPILER_TEMP_BYTES = 8 << 20

F32 = jnp.float32
BF16 = jnp.bfloat16


def _cparams(sem, resident=0, per_step=0, values=0):
    need = resident + 2 * per_step + values + COMPILER_TEMP_BYTES
    return pltpu.CompilerParams(dimension_semantics=sem, vmem_limit_bytes=min(int(need), VMEM_CAP))


def _dot(a, b):
    return jnp.dot(a, b, preferred_element_type=F32)


def _dot_nt(a, b):
    return lax.dot_general(a, b, (((1,), (1,)), ((), ())), preferred_element_type=F32)


def _rms(x):
    return x * lax.rsqrt(jnp.mean(x * x, axis=-1, keepdims=True) + EPS)


def _sigmoid(x):
    return 1.0 / (1.0 + jnp.exp(-x))


def _mod_norm(x, g, shift, scale):
    return (_rms(x) * g) * (1.0 + scale) + shift


def _mod_kernel(c_ref, w_ref, b_ref, o_ref):
    c = c_ref[...]
    s = (c * _sigmoid(c)).astype(BF16)
    o_ref[0] = _dot(s, w_ref[0].astype(BF16)) + b_ref[0]


def _mod_call(cond, w_mod, b_mod):
    tn = D_MODEL
    n_out = N_MOD * D_MODEL
    return pl.pallas_call(
        _mod_kernel,
        grid=(DEPTH, n_out // tn),
        in_specs=[
            pl.BlockSpec((COND_ROWS, D_MODEL), lambda l, n: (0, 0)),
            pl.BlockSpec((1, D_MODEL, tn), lambda l, n: (l, 0, n)),
            pl.BlockSpec((1, 1, tn), lambda l, n: (l, 0, n)),
        ],
        out_specs=pl.BlockSpec((1, COND_ROWS, tn), lambda l, n: (l, 0, n)),
        out_shape=jax.ShapeDtypeStruct((DEPTH, COND_ROWS, n_out), F32),
        compiler_params=_cparams(("parallel", "parallel"), per_step=D_MODEL * tn * 4, values=D_MODEL * tn * 2),
        name="adaln_mod",
    )(cond, w_mod, b_mod.reshape(DEPTH, 1, n_out))


def _mod_spec(mod, tiles_per_seq, tile=lambda i: i):
    if mod.shape[0] == 1:
        return pl.BlockSpec((1, N_MOD, D_MODEL), lambda i: (0, 0, 0))
    return pl.BlockSpec((1, N_MOD, D_MODEL), lambda i: (tile(i) // tiles_per_seq, 0, 0))


def _row_spec(width):
    return pl.BlockSpec((1, width), lambda i: (0, 0))


def _resident(shape):
    return pl.BlockSpec(shape, lambda i: (0,) * len(shape), pipeline_mode=pl.Buffered(1))


def _mlstm_gated_heads(hf_ref, hb_ref, o_ref, gh_ref):
    hs = hf_ref[...].astype(F32) + hb_ref[...].astype(F32)
    parts = [_rms(hs[:, h * DV_M:(h + 1) * DV_M]) for h in range(NH_M)]
    hn = jnp.concatenate(parts, axis=1) * gh_ref[...]
    return (hn * _sigmoid(o_ref[...].astype(F32))).astype(BF16)


def _ffn_kernel(*refs, j, final, mixer):
    x_ref, mod_ref, g_ref, wgu32_ref, wd32_ref, fg_ref = refs[:6]
    o_ref, wgu_ref, wd_ref = refs[-3:]
    step = pl.program_id(0)

    @pl.when(step < FFN_W_STEPS)
    def _():
        c0 = pl.multiple_of(step * FFN_WGU_COLS, FFN_WGU_COLS)
        r0 = pl.multiple_of(step * FFN_WD_ROWS, FFN_WD_ROWS)
        wgu_ref[:, pl.ds(c0, FFN_WGU_COLS)] = wgu32_ref[0].astype(BF16)
        wd_ref[pl.ds(r0, FFN_WD_ROWS), :] = wd32_ref[0].astype(BF16)

    @pl.when(step >= FFN_W_STEPS)
    def _():
        x = x_ref[...]
        mod = mod_ref[0]
        if mixer == "mlstm":
            hf_ref, hb_ref, og_ref, gh_ref, wo_ref = refs[6:11]
            x = x + mod[5:6] * _dot(_mlstm_gated_heads(hf_ref, hb_ref, og_ref, gh_ref), wo_ref[...])
        elif mixer == "attn":
            a_ref, wo_ref = refs[6:8]
            x = x + mod[5:6] * _dot(a_ref[...], wo_ref[...])
        shift, scale, gate = mod[3 * j:3 * j + 1], mod[3 * j + 1:3 * j + 2], mod[3 * j + 2:3 * j + 3]
        hb = _mod_norm(x, g_ref[...], shift, scale).astype(BF16)
        acc = jnp.zeros(x.shape, F32)
        for c in range(D_FF // FFN_CHUNK):
            lo = c * FFN_CHUNK
            gg = _dot(hb, wgu_ref[:, lo:lo + FFN_CHUNK])
            uu = _dot(hb, wgu_ref[:, D_FF + lo:D_FF + lo + FFN_CHUNK])
            act = (gg * _sigmoid(gg) * uu).astype(BF16)
            acc = acc + _dot(act, wd_ref[lo:lo + FFN_CHUNK, :])
        y = x + (FFN_RES * gate) * acc
        if final:
            y = _rms(y) * fg_ref[...]
        o_ref[...] = y


def _ffn_call(x, mod, g, wgu, wd, layer, j, seq_len, tm, final_g=None, mlstm=None, attn=None):
    t = x.shape[0]
    final = final_g is not None
    fg = final_g if final else g
    tile = lambda i: jnp.maximum(i - FFN_W_STEPS, 0)
    wstep = lambda i: jnp.minimum(i, FFN_W_STEPS - 1)
    tok = lambda width: pl.BlockSpec((tm, width), lambda i: (tile(i), 0))
    in_specs = [tok(D_MODEL), _mod_spec(mod, seq_len // tm, tile), _row_spec(D_MODEL),
                pl.BlockSpec((1, D_MODEL, FFN_WGU_COLS), lambda i: (layer, 0, wstep(i))),
                pl.BlockSpec((1, FFN_WD_ROWS, D_MODEL), lambda i: (layer, wstep(i), 0)), _row_spec(D_MODEL)]
    args = [x, mod, g, wgu, wd, fg]
    resident = 2 * (D_MODEL * 2 * D_FF + D_FF * D_MODEL)
    per_step = 2 * tm * D_MODEL * 4 + 4 * (D_MODEL * FFN_WGU_COLS + FFN_WD_ROWS * D_MODEL)
    values = tm * D_MODEL * (2 + 4 + 4) + 3 * tm * FFN_CHUNK * 4
    mixer = None
    if mlstm is not None:
        mixer = "mlstm"
        in_specs += [tok(V_W), tok(V_W), tok(V_W), _row_spec(V_W), _resident((V_W, D_MODEL))]
        args += list(mlstm)
        resident += 2 * V_W * D_MODEL
        per_step += 3 * tm * V_W * 2
        values += 3 * tm * V_W * 4
    elif attn is not None:
        mixer = "attn"
        in_specs += [tok(Q_W), _resident((Q_W, D_MODEL))]
        args += list(attn)
        resident += 2 * Q_W * D_MODEL
        per_step += tm * Q_W * 2
        values += tm * Q_W * 4
    return pl.pallas_call(
        functools.partial(_ffn_kernel, j=j, final=final, mixer=mixer),
        grid=(FFN_W_STEPS + t // tm,),
        in_specs=in_specs,
        out_specs=tok(D_MODEL),
        out_shape=jax.ShapeDtypeStruct((t, D_MODEL), F32),
        scratch_shapes=[pltpu.VMEM((D_MODEL, 2 * D_FF), BF16), pltpu.VMEM((D_FF, D_MODEL), BF16)],
        compiler_params=_cparams(("arbitrary",), resident, per_step, values),
        name=f"ffn{j // 2 + 1}" + (f"_{mixer}" if mixer else ""),
    )(*args)


def _scan_rows(x, op, fill, reverse):
    n = x.shape[0]
    idx = lax.broadcasted_iota(jnp.int32, x.shape, 0)
    s = 1
    while s < n:
        if reverse:
            x = op(x, jnp.where(idx < n - s, pltpu.roll(x, n - s, axis=0), fill))
        else:
            x = op(x, jnp.where(idx >= s, pltpu.roll(x, s, axis=0), fill))
        s *= 2
    return x


def _mlstm_in_kernel(x_ref, mod_ref, g_ref, wq_ref, wkt_ref, wvo_ref, wg_ref, bg_ref,
                     q_ref, kt_ref, v_ref, o_ref, scan_ref):
    mod = mod_ref[0]
    hb = _mod_norm(x_ref[...], g_ref[...], mod[3:4], mod[4:5]).astype(BF16)
    gt = _dot(hb, wg_ref[...]) + bg_ref[...]
    q_ref[...] = (_dot(hb, wq_ref[...]) * DK_M ** -0.5).astype(BF16)
    kt_ref[...] = _dot_nt(wkt_ref[...], hb).astype(BF16)
    vo = _dot(hb, wvo_ref[...])
    v_ref[...] = vo[:, :V_W].astype(BF16)
    o_ref[...] = vo[:, V_W:].astype(BF16)
    n = MLSTM_L
    fwd = lax.broadcasted_iota(jnp.int32, (n, LANES), 1) < GATE_DIR_LANES
    for c in range(gt.shape[0] // n):
        rows = slice(c * n, (c + 1) * n)
        li = gt[rows, :LANES]
        fx = gt[rows, LANES:]
        lf = jnp.minimum(fx, 0.0) - jnp.log(1.0 + jnp.exp(-jnp.abs(fx)))
        pre = _scan_rows(lf, jnp.add, 0.0, False)
        b = jnp.where(fwd, pre, pre[n - 1:n, :] - pre + lf)
        a = li - b
        scan_ref[rows, 0:LANES] = a
        scan_ref[rows, LANES:2 * LANES] = b
        scan_ref[rows, 2 * LANES:] = jnp.where(fwd, _scan_rows(a, jnp.maximum, -jnp.inf, False),
                                               _scan_rows(a, jnp.maximum, -jnp.inf, True))


def _mlstm_in_call(x, mod, g, wq, wkt, wvo, wg, bg, seq_len, tm):
    t = x.shape[0]
    out_cols = 2 * QK_W + 2 * V_W
    resident = 2 * D_MODEL * (out_cols + 2 * LANES)
    per_step = tm * (D_MODEL * 4 + out_cols * 2 + 3 * LANES * 4)
    values = tm * (D_MODEL * 2 + out_cols * 4 + 5 * LANES * 4)
    return pl.pallas_call(
        _mlstm_in_kernel,
        grid=(t // tm,),
        in_specs=[
            pl.BlockSpec((tm, D_MODEL), lambda i: (i, 0)),
            _mod_spec(mod, seq_len // tm),
            _row_spec(D_MODEL),
            _resident((D_MODEL, QK_W)),
            _resident((QK_W, D_MODEL)),
            _resident((D_MODEL, 2 * V_W)),
            _resident((D_MODEL, 2 * LANES)),
            _row_spec(2 * LANES),
        ],
        out_specs=[
            pl.BlockSpec((tm, QK_W), lambda i: (i, 0)),
            pl.BlockSpec((QK_W, tm), lambda i: (0, i)),
            pl.BlockSpec((tm, V_W), lambda i: (i, 0)),
            pl.BlockSpec((tm, V_W), lambda i: (i, 0)),
            pl.BlockSpec((tm, 3 * LANES), lambda i: (i, 0)),
        ],
        out_shape=[
            jax.ShapeDtypeStruct((t, QK_W), BF16),
            jax.ShapeDtypeStruct((QK_W, t), BF16),
            jax.ShapeDtypeStruct((t, V_W), BF16),
            jax.ShapeDtypeStruct((t, V_W), BF16),
            jax.ShapeDtypeStruct((t, 3 * LANES), F32),
        ],
        compiler_params=_cparams(("parallel",), resident, per_step, values),
        name="mlstm_in",
    )(x, mod, g, wq, wkt, wvo, wg, bg)


def _bf16_part(x):
    return x.astype(BF16).astype(F32)


def _mlstm_gates(d, scan_ref, rows, m):
    n = MLSTM_L
    a, b, cm = scan_ref[rows, 0:LANES], scan_ref[rows, LANES:2 * LANES], scan_ref[rows, 2 * LANES:]
    rev = d == 1
    mt = jnp.maximum(cm, m) * LOG2_E
    u_hi = _bf16_part(mt)
    u_lo = _bf16_part(mt - u_hi)
    z = -(b * LOG2_E + (u_hi + u_lo))
    z_hi = _bf16_part(z)
    z_mid = _bf16_part(z - z_hi)
    z_lo = _bf16_part(z - z_hi - z_mid)
    lane = lax.broadcasted_iota(jnp.int32, mt.shape, 1)
    k = (lane - d * GATE_DIR_LANES) // NH_M
    terms = jnp.where(k == 0, u_hi, jnp.where(k == 1, u_lo, jnp.where(k == 2, z_hi, jnp.where(k == 3, z_mid, z_lo))))
    end = 0 if rev else n - 1
    mx = jnp.maximum(m, cm[end:end + 1, :])
    keep = jnp.exp(m - mx)
    m_new = b[end:end + 1, :] + mx
    return terms, (a * LOG2_E).T, jnp.exp(a - mx).T, keep, m_new


def _mlstm_heads(dirs, c_sc, spread, m2):
    n = MLSTM_L
    row = lax.broadcasted_iota(jnp.int32, (n, n), 0)
    col = lax.broadcasted_iota(jnp.int32, (n, n), 1)
    ones = jnp.ones((n, LANES), BF16)
    heads = [(dd, h) for dd in dirs for h in range(NH_M)]
    kpad = jnp.zeros((DK_M, DK_M), BF16)
    lhs = []
    for (d, q_ref, kt_ref, _, _, rows, a_t, wend_t, _), h in heads:
        c = d * GATE_DIR_LANES + h
        base = (d * NH_M + h) * 2 * LANES
        u = spread[:, base:base + LANES]
        qh = q_ref[rows, h * DK_M:(h + 1) * DK_M]
        kth = kt_ref[h * DK_M:(h + 1) * DK_M, rows]
        e = jnp.exp2(a_t[c:c + 1, :] - u)
        mask = (col >= row) if d == 1 else (col <= row)
        w = jnp.where(mask, _dot(qh, kth) * e, 0.0).astype(BF16)
        inter = jnp.exp2(m2[:, c:c + 1] - u)
        qi = (inter[:, :DK_M] * qh.astype(F32)).astype(BF16)
        kts = (kth.astype(F32) * wend_t[c:c + 1, :]).astype(BF16)
        lhs.append(jnp.concatenate([jnp.concatenate([w, qi], axis=1), jnp.concatenate([kts, kpad], axis=1)], axis=0))
    for ((d, _, _, v_ref, h_ref, rows, _, _, keep), h), wq in zip(heads, lhs):
        c = d * GATE_DIR_LANES + h
        base = (d * NH_M + h) * 2 * LANES
        zz = spread[:, base + LANES:base + 2 * LANES]
        vext = jnp.concatenate([v_ref[rows, h * DV_M:(h + 1) * DV_M], ones], axis=1)
        cext = c_sc[d, h]
        r = _dot(wq, jnp.concatenate([vext, cext.astype(BF16)], axis=0))
        den = jnp.maximum(jnp.abs(r[:n, DV_M:]), jnp.exp2(zz))
        h_ref[rows, h * DV_M:(h + 1) * DV_M] = (r[:n, :DV_M] / den).astype(BF16)
        c_sc[d, h] = keep[:, c:c + 1] * cext + r[n:]


def _mlstm_kernel(*refs, zero_init):
    qf_ref, ktf_ref, vf_ref, scanf_ref, qb_ref, ktb_ref, vb_ref, scanb_ref, sel_ref = refs[:9]
    hf_ref, hb_ref, cout_ref, nout_ref, mout_ref, c_sc, m_sc = refs[-7:]
    i = pl.program_id(1)

    @pl.when(i == 0)
    def _():
        if zero_init:
            c_sc[...] = jnp.zeros(c_sc.shape, F32)
            m_sc[...] = jnp.zeros(m_sc.shape, F32)
        else:
            c0_ref, n0_ref, m0_ref = refs[9:12]
            c_sc[:, :, :, :DV_M] = c0_ref[0]
            c_sc[:, :, :, DV_M:] = n0_ref[0]
            m_sc[...] = m0_ref[0]

    nsub = qf_ref.shape[0] // MLSTM_L
    for sub in range(nsub):
        rows_f = slice(sub * MLSTM_L, (sub + 1) * MLSTM_L)
        rows_b = slice((nsub - 1 - sub) * MLSTM_L, (nsub - sub) * MLSTM_L)
        m = m_sc[...]
        terms_f, at_f, wend_f, keep_f, mnew_f = _mlstm_gates(0, scanf_ref, rows_f, m)
        terms_b, at_b, wend_b, keep_b, mnew_b = _mlstm_gates(1, scanb_ref, rows_b, m)
        lane = lax.broadcasted_iota(jnp.int32, terms_f.shape, 1)
        packed = jnp.where(lane < GATE_DIR_LANES, terms_f, jnp.where(lane < 2 * GATE_DIR_LANES, terms_b, 0.0))
        spread = _dot(packed.astype(BF16), sel_ref[...])
        m2 = m * LOG2_E
        _mlstm_heads([(0, qf_ref, ktf_ref, vf_ref, hf_ref, rows_f, at_f, wend_f, keep_f),
                      (1, qb_ref, ktb_ref, vb_ref, hb_ref, rows_b, at_b, wend_b, keep_b)], c_sc, spread, m2)
        lane_m = lax.broadcasted_iota(jnp.int32, m.shape, 1)
        m_sc[...] = jnp.where(lane_m < GATE_DIR_LANES, mnew_f, jnp.where(lane_m < 2 * GATE_DIR_LANES, mnew_b, m))

    @pl.when(i == pl.num_programs(1) - 1)
    def _():
        cout_ref[0] = c_sc[:, :, :, :DV_M]
        nout_ref[0] = c_sc[:, :, :, DV_M:]
        mout_ref[0] = m_sc[...]


def _mlstm_core_call(q, kt, v, scans, sel, init, bsz, seq_len):
    t = q.shape[0]
    rows = min(MLSTM_STEP_CHUNKS * MLSTM_L, seq_len)
    nc = seq_len // rows
    fwd = lambda b, i: (b * nc + i, 0)
    bwd = lambda b, i: (b * nc + nc - 1 - i, 0)
    fwd_t = lambda b, i: (0, b * nc + i)
    bwd_t = lambda b, i: (0, b * nc + nc - 1 - i)

    def specs(row_map, col_map):
        return [
            pl.BlockSpec((rows, QK_W), row_map),
            pl.BlockSpec((QK_W, rows), col_map),
            pl.BlockSpec((rows, V_W), row_map),
            pl.BlockSpec((rows, 3 * LANES), row_map),
        ]

    half_shape = (2, NH_M, DK_M, DV_M)
    half_spec = pl.BlockSpec((1,) + half_shape, lambda b, i: (b, 0, 0, 0, 0))
    m_spec = pl.BlockSpec((1, 1, LANES), lambda b, i: (b, 0, 0))
    state_bytes = 2 * NH_M * DK_M * 2 * DV_M * 4
    sel_shape = (LANES, 2 * NH_M * 2 * LANES)
    in_specs = specs(fwd, fwd_t) + specs(bwd, bwd_t) + [pl.BlockSpec(sel_shape, lambda b, i: (0, 0))]
    args = [q, kt, v, scans, q, kt, v, scans, sel]
    if init is not None:
        in_specs += [half_spec, half_spec, m_spec]
        args += list(init)
    per_step = 2 * rows * ((2 * QK_W + 2 * V_W) * 2 + 3 * LANES * 4) + 2 * sel_shape[0] * sel_shape[1] + 2 * state_bytes
    values = MLSTM_L * sel_shape[1] * 4 + 2 * NH_M * MLSTM_L * 4 * LANES * 4
    return pl.pallas_call(
        functools.partial(_mlstm_kernel, zero_init=init is None),
        grid=(bsz, nc),
        in_specs=in_specs,
        out_specs=[pl.BlockSpec((rows, V_W), fwd), pl.BlockSpec((rows, V_W), bwd), half_spec, half_spec, m_spec],
        out_shape=[
            jax.ShapeDtypeStruct((t, V_W), BF16),
            jax.ShapeDtypeStruct((t, V_W), BF16),
            jax.ShapeDtypeStruct((bsz,) + half_shape, F32),
            jax.ShapeDtypeStruct((bsz,) + half_shape, F32),
            jax.ShapeDtypeStruct((bsz, 1, LANES), F32),
        ],
        scratch_shapes=[pltpu.VMEM((2, NH_M, DK_M, 2 * DV_M), F32), pltpu.VMEM((1, LANES), F32)],
        compiler_params=_cparams(("parallel", "arbitrary"), state_bytes, per_step, values),
        name="mlstm_core",
    )(*args)


def _rope_rows_swap(x):
    q = HD_A // 4
    parts = []
    for base in range(0, x.shape[0], 2 * q):
        parts += [x[base + q:base + 2 * q], x[base:base + q]]
    return jnp.concatenate(parts, axis=0)


def _attn_in_kernel(*refs, rope, emit_f32):
    x_ref, mod_ref, g_ref, wqt_ref, wk_ref, wvt_ref = refs[:6]
    pos = 6
    if rope:
        cos_ref, sina_ref, sinb_ref, cost_ref, sint_ref = refs[pos:pos + 5]
        pos += 5
    if emit_f32:
        wv_ref = refs[pos]
        pos += 1
    qt_ref, k_ref, vt_ref = refs[pos:pos + 3]
    pos += 3
    mod = mod_ref[0]
    hb = _mod_norm(x_ref[...], g_ref[...], mod[3:4], mod[4:5]).astype(BF16)
    qt = _dot_nt(wqt_ref[...], hb)
    k = _dot(hb, wk_ref[...])
    blk = ATTN_BLOCK
    vt = _dot_nt(wvt_ref[...], hb).astype(BF16)
    for tb in range(vt.shape[1] // blk):
        vt_ref[tb] = vt[:, tb * blk:(tb + 1) * blk]
    if emit_f32:
        kf_ref, vf_ref = refs[pos:pos + 2]
        kf_ref[...] = k
        vf_ref[...] = _dot(hb, wv_ref[...])
    for gi in range(Q_W // LANES):
        qg = qt[gi * LANES:(gi + 1) * LANES, :]
        if rope:
            cos_t = jnp.concatenate([cost_ref[...]] * (LANES // HD_A), axis=0)
            sin_t = jnp.concatenate([sint_ref[...]] * (LANES // HD_A), axis=0)
            qg = qg * cos_t + _rope_rows_swap(qg) * sin_t
        qg = (qg * (HD_A ** -0.5 * LOG2_E)).astype(BF16)
        p, g = gi // G_A, gi % G_A
        for tb in range(qt.shape[1] // blk):
            qt_ref[tb, p, :, g * blk:(g + 1) * blk] = qg[:, tb * blk:(tb + 1) * blk]
    for gi in range(KV_W // LANES):
        kg = k[:, gi * LANES:(gi + 1) * LANES]
        if rope:
            kg = (kg * cos_ref[...] + pltpu.roll(kg, LANES - HD_A // 4, axis=1) * sina_ref[...]
                  + pltpu.roll(kg, HD_A // 4, axis=1) * sinb_ref[...])
        k_ref[:, gi * LANES:(gi + 1) * LANES] = kg.astype(BF16)


def _attn_in_call(x, mod, g, wqt, wk, wvt, seq_len, tm, rope_tabs=None, wv=None):
    t = x.shape[0]
    rope = rope_tabs is not None
    emit_f32 = wv is not None
    tiles = seq_len // tm
    tok = lambda width: pl.BlockSpec((tm, width), lambda i: (i, 0))
    in_specs = [tok(D_MODEL), _mod_spec(mod, tiles), _row_spec(D_MODEL),
                _resident((Q_W, D_MODEL)), _resident((D_MODEL, KV_W)), _resident((KV_W, D_MODEL))]
    args = [x, mod, g, wqt, wk, wvt]
    if rope:
        pos_row = lambda i: (i % tiles, 0)
        pos_col = lambda i: (0, i % tiles)
        in_specs += [pl.BlockSpec((tm, LANES), pos_row)] * 3 + [pl.BlockSpec((HD_A, tm), pos_col)] * 2
        args += list(rope_tabs)
    if emit_f32:
        in_specs.append(_resident((D_MODEL, KV_W)))
        args.append(wv)
    nblk = tm // ATTN_BLOCK
    qt_slab = (NKV_A // 2, LANES, G_A * ATTN_BLOCK)
    vt_slab = (KV_W, ATTN_BLOCK)
    out_specs = [pl.BlockSpec((nblk,) + qt_slab, lambda i: (i, 0, 0, 0)), tok(KV_W),
                 pl.BlockSpec((nblk,) + vt_slab, lambda i: (i, 0, 0))]
    out_shape = [jax.ShapeDtypeStruct((t // ATTN_BLOCK,) + qt_slab, BF16), jax.ShapeDtypeStruct((t, KV_W), BF16),
                 jax.ShapeDtypeStruct((t // ATTN_BLOCK,) + vt_slab, BF16)]
    if emit_f32:
        out_specs += [tok(KV_W), tok(KV_W)]
        out_shape += [jax.ShapeDtypeStruct((t, KV_W), F32)] * 2
    resident = 2 * D_MODEL * (Q_W + 3 * KV_W)
    per_step = tm * (D_MODEL * 4 + (Q_W + 2 * KV_W) * 2 + 2 * KV_W * 4 + (3 * LANES + 2 * HD_A) * 4)
    values = tm * (D_MODEL * 2 + (Q_W + 3 * KV_W) * 4 + Q_W * 4)
    return pl.pallas_call(
        functools.partial(_attn_in_kernel, rope=rope, emit_f32=emit_f32),
        grid=(t // tm,),
        in_specs=in_specs,
        out_specs=out_specs,
        out_shape=out_shape,
        compiler_params=_cparams(("parallel",), resident, per_step, values),
        name="attn_in",
    )(*args)


def _block_attention(sink_ref, qt_ref, o_ref, sub, pair_keys, mask_of):
    blk = ATTN_BLOCK
    cols_all = G_A * blk
    col_g = lax.broadcasted_iota(jnp.int32, (1, cols_all), 1) // blk
    staged = []
    for p, keys in enumerate(pair_keys):
        qp = qt_ref[sub, p]
        row = lax.broadcasted_iota(jnp.int32, qp.shape, 0)
        for e in range(2):
            kv = 2 * p + e
            qm = jnp.where((row >= e * HD_A) & (row < (e + 1) * HD_A), qp, jnp.zeros_like(qp))
            sink = jnp.full((1, cols_all), sink_ref[0, kv * G_A], F32)
            for g in range(1, G_A):
                sink = jnp.where(col_g == g, sink_ref[0, kv * G_A + g], sink)
            sink = sink * LOG2_E
            scores = []
            m = sink
            for i, (k, _) in enumerate(keys):
                s = _dot(k, qm)
                msk = mask_of(i)
                s = s if msk is None else jnp.where(msk, s, NEG_INF)
                m = jnp.maximum(m, jnp.max(s, axis=0, keepdims=True))
                scores.append(s)
            staged.append((scores, m, sink))
    outs = []
    for idx, (scores, m, sink) in enumerate(staged):
        acc = None
        for s, (_, vt) in zip(scores, pair_keys[idx // 2]):
            vx = jnp.concatenate([vt, jnp.ones((ONES_ROWS, vt.shape[1]), BF16)], axis=0)
            pv = _dot(vx, jnp.exp2(s - m).astype(BF16))
            acc = pv if acc is None else acc + pv
        outs.append(acc[:LANES] / (acc[LANES:LANES + 1] + jnp.exp2(sink - m)))
    row_o = lax.broadcasted_iota(jnp.int32, outs[0].shape, 0)
    for p in range(len(pair_keys)):
        a_t = jnp.where(row_o < HD_A, outs[2 * p], outs[2 * p + 1])
        for g in range(G_A):
            c0 = (p * G_A + g) * LANES
            o_ref[sub * blk:(sub + 1) * blk, c0:c0 + LANES] = a_t[:, g * blk:(g + 1) * blk].T.astype(BF16)


def _attn_latent_kernel(sink_ref, qt_ref, kp_ref, kc_ref, kn_ref, vtp_ref, vtc_ref, vtn_ref, kx_ref, vtx_ref,
                        o_ref, *, n_tok):
    blk = ATTN_BLOCK
    nsub = ATTN_STEP_BLOCKS
    shape = (3 * blk, G_A * blk)
    for sub in range(nsub):
        j = pl.program_id(1) * nsub + sub
        s_pos = (j - 1) * blk + lax.broadcasted_iota(jnp.int32, shape, 0)
        t_pos = j * blk + lax.broadcasted_iota(jnp.int32, shape, 1) % blk
        local = (jnp.abs(t_pos - s_pos) <= WINDOW) & (s_pos >= 0) & (s_pos < n_tok)
        pair_keys = []
        for p in range(NKV_A // 2):
            pl_ = slice(p * LANES, (p + 1) * LANES)
            k_blocks = ([kp_ref[:, pl_]] + [kc_ref[c * blk:(c + 1) * blk, pl_] for c in range(nsub)]
                        + [kn_ref[:, pl_]])[sub:sub + 3]
            vt_blocks = ([vtp_ref[0, pl_, :]] + [vtc_ref[c, pl_, :] for c in range(nsub)]
                         + [vtn_ref[0, pl_, :]])[sub:sub + 3]
            pair_keys.append([(jnp.concatenate(k_blocks, axis=0), jnp.concatenate(vt_blocks, axis=1)),
                              (kx_ref[0, :, pl_], vtx_ref[0, pl_, :])])
        _block_attention(sink_ref, qt_ref, o_ref, sub, pair_keys, lambda i, local=local: local if i == 0 else None)


def _attn_latent_call(qt, k, vt, kx, vtx, sink, bsz, seq_len):
    t = k.shape[0]
    nsub = ATTN_STEP_BLOCKS
    nb = seq_len // ATTN_BLOCK
    ns = nb // nsub
    n_ctx = kx.shape[1]
    prev = lambda b, j: b * nb + jnp.maximum(j * nsub - 1, 0)
    nxt = lambda b, j: b * nb + jnp.minimum(j * nsub + nsub, nb - 1)
    cur = lambda b, j: b * ns + j
    return pl.pallas_call(
        functools.partial(_attn_latent_kernel, n_tok=seq_len),
        grid=(bsz, ns),
        in_specs=[
            pl.BlockSpec(memory_space=pltpu.SMEM),
            pl.BlockSpec((nsub, NKV_A // 2, LANES, G_A * ATTN_BLOCK), lambda b, j: (cur(b, j), 0, 0, 0)),
            pl.BlockSpec((ATTN_BLOCK, KV_W), lambda b, j: (prev(b, j), 0)),
            pl.BlockSpec((nsub * ATTN_BLOCK, KV_W), lambda b, j: (cur(b, j), 0)),
            pl.BlockSpec((ATTN_BLOCK, KV_W), lambda b, j: (nxt(b, j), 0)),
            pl.BlockSpec((1, KV_W, ATTN_BLOCK), lambda b, j: (prev(b, j), 0, 0)),
            pl.BlockSpec((nsub, KV_W, ATTN_BLOCK), lambda b, j: (cur(b, j), 0, 0)),
            pl.BlockSpec((1, KV_W, ATTN_BLOCK), lambda b, j: (nxt(b, j), 0, 0)),
            pl.BlockSpec((1, n_ctx, KV_W), lambda b, j: (b, 0, 0)),
            pl.BlockSpec((1, KV_W, n_ctx), lambda b, j: (b, 0, 0)),
        ],
        out_specs=pl.BlockSpec((nsub * ATTN_BLOCK, Q_W), lambda b, j: (cur(b, j), 0)),
        out_shape=jax.ShapeDtypeStruct((t, Q_W), BF16),
        compiler_params=_cparams(
            ("parallel", "parallel"),
            per_step=(nsub * ATTN_BLOCK * (2 * Q_W + 2 * KV_W) + 2 * (2 * ATTN_BLOCK + n_ctx) * KV_W) * 2,
            values=nsub * NKV_A * (3 * ATTN_BLOCK + n_ctx) * G_A * ATTN_BLOCK * 6),
        name="attn_latent",
    )(sink, qt, k, k, k, vt, vt, vt, kx, vtx)


def _attn_context_kernel(sink_ref, qt_ref, k_ref, vt_ref, o_ref):
    pair_keys = []
    for p in range(NKV_A // 2):
        pl_ = slice(p * LANES, (p + 1) * LANES)
        vt = jnp.concatenate([vt_ref[c, pl_, :] for c in range(vt_ref.shape[0])], axis=1)
        pair_keys.append([(k_ref[:, pl_], vt)])
    for sub in range(qt_ref.shape[0]):
        _block_attention(sink_ref, qt_ref, o_ref, sub, pair_keys, lambda i: None)


def _attn_context_call(qt, k, vt, sink, bsz, seq_len):
    t = k.shape[0]
    nb = seq_len // ATTN_BLOCK
    return pl.pallas_call(
        _attn_context_kernel,
        grid=(bsz,),
        in_specs=[
            pl.BlockSpec(memory_space=pltpu.SMEM),
            pl.BlockSpec((nb, NKV_A // 2, LANES, G_A * ATTN_BLOCK), lambda b: (b, 0, 0, 0)),
            pl.BlockSpec((seq_len, KV_W), lambda b: (b, 0)),
            pl.BlockSpec((nb, KV_W, ATTN_BLOCK), lambda b: (b, 0, 0)),
        ],
        out_specs=pl.BlockSpec((seq_len, Q_W), lambda b: (b, 0)),
        out_shape=jax.ShapeDtypeStruct((t, Q_W), BF16),
        compiler_params=_cparams(("parallel",), per_step=seq_len * (2 * Q_W + 2 * KV_W) * 2,
                                 values=nb * NKV_A * seq_len * G_A * ATTN_BLOCK * 6),
        name="attn_context",
    )(sink, qt, k, vt)


def _rope_tables(n_tok):
    quarter = HD_A // 4
    freqs = ROPE_THETA ** (-jnp.arange(quarter, dtype=F32) / quarter)
    pos = jnp.arange(n_tok)
    row = (pos // GRID_W).astype(F32)
    col = (pos % GRID_W).astype(F32)
    ang_r, ang_c = row[:, None] * freqs, col[:, None] * freqs
    ang = jnp.concatenate([ang_r, ang_r, ang_c, ang_c], axis=-1)
    cos, sin = jnp.cos(ang), jnp.sin(ang)
    first = (jnp.arange(HD_A) % (2 * quarter)) < quarter
    sin_a = jnp.where(first, -sin, 0.0)
    sin_b = jnp.where(first, 0.0, sin)
    sin_t = jnp.where(first, -sin, sin)
    lane_tile = lambda a: jnp.tile(a, (1, LANES // HD_A))
    return lane_tile(cos), lane_tile(sin_a), lane_tile(sin_b), cos.T, sin_t.T


def _gate_lanes(a):
    lead = a.shape[:-2]
    rep = jnp.broadcast_to(a[..., :, None, :], lead + (2, GATE_COPIES, NH_M)).reshape(lead + (2 * GATE_DIR_LANES,))
    return jnp.concatenate([rep, jnp.zeros(lead + (LANES - 2 * GATE_DIR_LANES,), a.dtype)], axis=-1)


def _gate_layout(a):
    a4 = a.reshape(a.shape[:-1] + (2, 2, NH_M))
    return jnp.concatenate([_gate_lanes(a4[..., :, 0, :]), _gate_lanes(a4[..., :, 1, :])], axis=-1)


def _gate_selector():
    r = jnp.arange(LANES)[:, None]
    c = jnp.arange(2 * NH_M * 2 * LANES)[None, :]
    hd, j = c // (2 * LANES), c % (2 * LANES)
    rd, rk, rh = r // GATE_DIR_LANES, (r % GATE_DIR_LANES) // NH_M, r % NH_M
    hit = (r < 2 * GATE_DIR_LANES) & (rd == hd // NH_M) & (rh == hd % NH_M) & ((j < LANES) == (rk < 2))
    return hit.astype(BF16)


def _pair_layout_cols(w):
    d_in = w.shape[0]
    w5 = w.reshape(d_in, NKV_A // 2, 2, G_A, HD_A)
    return jnp.transpose(w5, (0, 1, 3, 2, 4)).reshape(d_in, Q_W)


def _mlstm_init(state_c, state_n, state_m):
    n_rep = jnp.broadcast_to(state_n[..., None].astype(F32), state_n.shape + (DV_M,))
    return state_c.astype(F32), n_rep, _gate_lanes(state_m.astype(F32))[:, None, :]


def kernel(x_prompt, x_sample, state_c, state_n, state_m, cache_k, cache_v, c, c_ctx, w_mod, b_mod, norm_g,
           ffn1_w_gu, ffn1_w_down, ffn2_w_gu, ffn2_w_down, mlstm_w_in, mlstm_b_gate, mlstm_g_head, mlstm_w_out,
           attn_w_in, attn_sink, attn_w_out, final_g):
    bp, n_p, _ = x_prompt.shape
    bs, n_s, _ = x_sample.shape
    xp = x_prompt.reshape(bp * n_p, D_MODEL)
    xs = x_sample.reshape(bs * n_s, D_MODEL)
    tm_p, tm_s = 512, 512
    tm_ffn1 = 1024

    cond = jnp.concatenate([c_ctx[None, :], c], axis=0)
    cond = jnp.pad(cond, ((0, COND_ROWS - cond.shape[0]), (0, 0)))
    mod_all = _mod_call(cond, w_mod, b_mod).reshape(DEPTH, COND_ROWS, N_MOD, D_MODEL)

    outs = {}
    for l in range(DEPTH):
        mod_p = mod_all[l, 0:1]
        mod_s = mod_all[l, 1:1 + bs]
        g = norm_g[l]
        xp = _ffn_call(xp, mod_p, g[0:1], ffn1_w_gu, ffn1_w_down, l, 0, n_p, tm_ffn1)
        xs = _ffn_call(xs, mod_s, g[0:1], ffn1_w_gu, ffn1_w_down, l, 0, n_s, tm_ffn1)
        i = l // 2
        if l % 2 == 0:
            w_in = mlstm_w_in[i]
            wq = w_in[:, :QK_W].astype(BF16)
            wkt = w_in[:, QK_W:2 * QK_W].T.astype(BF16)
            wvo = w_in[:, 2 * QK_W:2 * QK_W + 2 * V_W].astype(BF16)
            wg = _gate_layout(w_in[:, 2 * QK_W + 2 * V_W:]).astype(BF16)
            bg = _gate_layout(mlstm_b_gate[i].astype(F32))[None, :]
            gh = mlstm_g_head[i].astype(F32)[None, :]
            w_out = mlstm_w_out[i].astype(BF16)
            sel = _gate_selector()
            streams = [("p", xp, mod_p, n_p, tm_p, bp, None),
                       ("s", xs, mod_s, n_s, tm_s, bs, _mlstm_init(state_c[:, i], state_n[:, i], state_m[:, i]))]
            res = {}
            for tag, x, mod, n_tok, tm, bsz, init in streams:
                q, kt, v, o, scans = _mlstm_in_call(x, mod, g[1:2], wq, wkt, wvo, wg, bg, n_tok, tm)
                hf, hb, c_fin, n_fin, m_fin = _mlstm_core_call(q, kt, v, scans, sel, init, bsz, n_tok)
                res[tag] = ((hf, hb, o, gh, w_out), c_fin, n_fin, m_fin)
            mix_p, c_fin, n_fin, m_fin = res["p"]
            mix_s = res["s"][0]
            mix_key = "mlstm"
            dt = x_prompt.dtype
            m_heads = m_fin[:, 0, :2 * GATE_DIR_LANES].reshape(bp, 2, GATE_COPIES, NH_M)[:, :, 0, :]
            outs.setdefault("c", []).append(c_fin.astype(dt))
            outs.setdefault("n", []).append(n_fin[..., 0].astype(dt))
            outs.setdefault("m", []).append(m_heads.astype(dt))
        else:
            w_in = attn_w_in[i]
            wqt = _pair_layout_cols(w_in[:, :Q_W]).T.astype(BF16)
            wk = w_in[:, Q_W:Q_W + KV_W].astype(BF16)
            wv = w_in[:, Q_W + KV_W:].astype(BF16)
            wvt = w_in[:, Q_W + KV_W:].T.astype(BF16)
            w_out = _pair_layout_cols(attn_w_out[i].T).T.astype(BF16)
            sink = attn_sink[i].astype(F32)[None, :]
            qt, k, vt, kf, vf = _attn_in_call(xp, mod_p, g[1:2], wqt, wk, wvt, n_p, tm_p, wv=wv)
            mix_p = (_attn_context_call(qt, k, vt, sink, bp, n_p), w_out)
            mix_key = "attn"
            outs.setdefault("k", []).append(kf.reshape(bp, n_p, NKV_A, HD_A))
            outs.setdefault("v", []).append(vf.reshape(bp, n_p, NKV_A, HD_A))
            qt, k, vt = _attn_in_call(xs, mod_s, g[1:2], wqt, wk, wvt, n_s, tm_s, rope_tabs=_rope_tables(n_s))
            n_ctx = cache_k.shape[2]
            kx = cache_k[:, i].reshape(bs, n_ctx, KV_W).astype(BF16)
            vtx = jnp.swapaxes(cache_v[:, i].reshape(bs, n_ctx, KV_W), 1, 2).astype(BF16)
            mix_s = (_attn_latent_call(qt, k, vt, kx, vtx, sink, bs, n_s), w_out)
        fg = final_g[None, :] if l == DEPTH - 1 else None
        xp = _ffn_call(xp, mod_p, g[2:3], ffn2_w_gu, ffn2_w_down, l, 2, n_p, tm_p, final_g=fg, **{mix_key: mix_p})
        xs = _ffn_call(xs, mod_s, g[2:3], ffn2_w_gu, ffn2_w_down, l, 2, n_s, tm_s, final_g=fg, **{mix_key: mix_s})

    return (xp.reshape(bp, n_p, D_MODEL), xs.reshape(bs, n_s, D_MODEL),
            jnp.stack(outs["c"], axis=1), jnp.stack(outs["n"], axis=1), jnp.stack(outs["m"], axis=1),
            jnp.stack(outs["k"], axis=1), jnp.stack(outs["v"], axis=1))
```

```python
import functools

import jax
import jax.numpy as jnp
from jax import lax
from jax.experimental import pallas as pl
from jax.experimental.pallas import tpu as pltpu

D_MODEL = 1024
DEPTH = 2
GRID_W = 64
D_FF = 2816
FFN_RES = 0.5
NH_M = 8
DK_M = 64
DV_M = 128
NH_A = 16
NKV_A = 4
G_A = NH_A // NKV_A
HD_A = 64
WINDOW = 128
ATTN_BLOCK = 128
ROPE_THETA = 10000.0
EPS = 1e-6
NEG_INF = -1e30
N_MOD = 9

QK_W = NH_M * DK_M
V_W = NH_M * DV_M
KV_W = NKV_A * HD_A
Q_W = NH_A * HD_A

LANES = 128
MLSTM_L = 128
MLSTM_STEP_CHUNKS = 4
FFN_CHUNK = 256
FFN_W_STEPS = 11
FFN_WGU_COLS = 2 * D_FF // FFN_W_STEPS
FFN_WD_ROWS = D_FF // FFN_W_STEPS
ATTN_STEP_BLOCKS = 4
ONES_ROWS = 16
LOG2_E = 1.4426950408889634
GATE_COPIES = 5
GATE_DIR_LANES = GATE_COPIES * NH_M
COND_ROWS = 16
VMEM_CAP = 60000 * 1024
COMPILER_TEMP_BYTES = 8 << 20

F32 = jnp.float32
BF16 = jnp.bfloat16


def _cparams(sem, resident=0, per_step=0, values=0):
    need = resident + 2 * per_step + values + COMPILER_TEMP_BYTES
    return pltpu.CompilerParams(dimension_semantics=sem, vmem_limit_bytes=min(int(need), VMEM_CAP))


def _dot(a, b):
    return jnp.dot(a, b, preferred_element_type=F32)


def _dot_nt(a, b):
    return lax.dot_general(a, b, (((1,), (1,)), ((), ())), preferred_element_type=F32)


def _rms(x):
    return x * lax.rsqrt(jnp.mean(x * x, axis=-1, keepdims=True) + EPS)


def _sigmoid(x):
    return 1.0 / (1.0 + jnp.exp(-x))


def _mod_norm(x, g, shift, scale):
    return (_rms(x) * g) * (1.0 + scale) + shift


def _mod_kernel(c_ref, w_ref, b_ref, o_ref):
    c = c_ref[...]
    s = (c * _sigmoid(c)).astype(BF16)
    o_ref[0] = _dot(s, w_ref[0].astype(BF16)) + b_ref[0]


def _mod_call(cond, w_mod, b_mod):
    tn = D_MODEL
    n_out = N_MOD * D_MODEL
    return pl.pallas_call(
        _mod_kernel,
        grid=(DEPTH, n_out // tn),
        in_specs=[
            pl.BlockSpec((COND_ROWS, D_MODEL), lambda l, n: (0, 0)),
            pl.BlockSpec((1, D_MODEL, tn), lambda l, n: (l, 0, n)),
            pl.BlockSpec((1, 1, tn), lambda l, n: (l, 0, n)),
        ],
        out_specs=pl.BlockSpec((1, COND_ROWS, tn), lambda l, n: (l, 0, n)),
        out_shape=jax.ShapeDtypeStruct((DEPTH, COND_ROWS, n_out), F32),
        compiler_params=_cparams(("parallel", "parallel"), per_step=D_MODEL * tn * 4, values=D_MODEL * tn * 2),
        name="adaln_mod",
    )(cond, w_mod, b_mod.reshape(DEPTH, 1, n_out))


def _mod_spec(mod, tiles_per_seq, tile=lambda i: i):
    if mod.shape[0] == 1:
        return pl.BlockSpec((1, N_MOD, D_MODEL), lambda i: (0, 0, 0))
    return pl.BlockSpec((1, N_MOD, D_MODEL), lambda i: (tile(i) // tiles_per_seq, 0, 0))


def _row_spec(width):
    return pl.BlockSpec((1, width), lambda i: (0, 0))


def _resident(shape):
    return pl.BlockSpec(shape, lambda i: (0,) * len(shape), pipeline_mode=pl.Buffered(1))


def _mlstm_gated_heads(hf_ref, hb_ref, o_ref, gh_ref):
    hs = hf_ref[...].astype(F32) + hb_ref[...].astype(F32)
    parts = [_rms(hs[:, h * DV_M:(h + 1) * DV_M]) for h in range(NH_M)]
    hn = jnp.concatenate(parts, axis=1) * gh_ref[...]
    return (hn * _sigmoid(o_ref[...].astype(F32))).astype(BF16)


def _ffn_kernel(*refs, j, final, mixer):
    x_ref, mod_ref, g_ref, wgu32_ref, wd32_ref, fg_ref = refs[:6]
    o_ref, wgu_ref, wd_ref = refs[-3:]
    step = pl.program_id(0)

    @pl.when(step < FFN_W_STEPS)
    def _():
        c0 = pl.multiple_of(step * FFN_WGU_COLS, FFN_WGU_COLS)
        r0 = pl.multiple_of(step * FFN_WD_ROWS, FFN_WD_ROWS)
        wgu_ref[:, pl.ds(c0, FFN_WGU_COLS)] = wgu32_ref[0].astype(BF16)
        wd_ref[pl.ds(r0, FFN_WD_ROWS), :] = wd32_ref[0].astype(BF16)

    @pl.when(step >= FFN_W_STEPS)
    def _():
        x = x_ref[...]
        mod = mod_ref[0]
        if mixer == "mlstm":
            hf_ref, hb_ref, og_ref, gh_ref, wo_ref = refs[6:11]
            x = x + mod[5:6] * _dot(_mlstm_gated_heads(hf_ref, hb_ref, og_ref, gh_ref), wo_ref[...])
        elif mixer == "attn":
            a_ref, wo_ref = refs[6:8]
            x = x + mod[5:6] * _dot(a_ref[...], wo_ref[...])
        shift, scale, gate = mod[3 * j:3 * j + 1], mod[3 * j + 1:3 * j + 2], mod[3 * j + 2:3 * j + 3]
        hb = _mod_norm(x, g_ref[...], shift, scale).astype(BF16)
        acc = jnp.zeros(x.shape, F32)
        for c in range(D_FF // FFN_CHUNK):
            lo = c * FFN_CHUNK
            gg = _dot(hb, wgu_ref[:, lo:lo + FFN_CHUNK])
            uu = _dot(hb, wgu_ref[:, D_FF + lo:D_FF + lo + FFN_CHUNK])
            act = (gg * _sigmoid(gg) * uu).astype(BF16)
            acc = acc + _dot(act, wd_ref[lo:lo + FFN_CHUNK, :])
        y = x + (FFN_RES * gate) * acc
        if final:
            y = _rms(y) * fg_ref[...]
        o_ref[...] = y


def _ffn_call(x, mod, g, wgu, wd, layer, j, seq_len, tm, final_g=None, mlstm=None, attn=None):
    t = x.shape[0]
    final = final_g is not None
    fg = final_g if final else g
    tile = lambda i: jnp.maximum(i - FFN_W_STEPS, 0)
    wstep = lambda i: jnp.minimum(i, FFN_W_STEPS - 1)
    tok = lambda width: pl.BlockSpec((tm, width), lambda i: (tile(i), 0))
    in_specs = [tok(D_MODEL), _mod_spec(mod, seq_len // tm, tile), _row_spec(D_MODEL),
                pl.BlockSpec((1, D_MODEL, FFN_WGU_COLS), lambda i: (layer, 0, wstep(i))),
                pl.BlockSpec((1, FFN_WD_ROWS, D_MODEL), lambda i: (layer, wstep(i), 0)), _row_spec(D_MODEL)]
    args = [x, mod, g, wgu, wd, fg]
    resident = 2 * (D_MODEL * 2 * D_FF + D_FF * D_MODEL)
    per_step = 2 * tm * D_MODEL * 4 + 4 * (D_MODEL * FFN_WGU_COLS + FFN_WD_ROWS * D_MODEL)
    values = tm * D_MODEL * (2 + 4 + 4) + 3 * tm * FFN_CHUNK * 4
    mixer = None
    if mlstm is not None:
        mixer = "mlstm"
        in_specs += [tok(V_W), tok(V_W), tok(V_W), _row_spec(V_W), _resident((V_W, D_MODEL))]
        args += list(mlstm)
        resident += 2 * V_W * D_MODEL
        per_step += 3 * tm * V_W * 2
        values += 3 * tm * V_W * 4
    elif attn is not None:
        mixer = "attn"
        in_specs += [tok(Q_W), _resident((Q_W, D_MODEL))]
        args += list(attn)
        resident += 2 * Q_W * D_MODEL
        per_step += tm * Q_W * 2
        values += tm * Q_W * 4
    return pl.pallas_call(
        functools.partial(_ffn_kernel, j=j, final=final, mixer=mixer),
        grid=(FFN_W_STEPS + t // tm,),
        in_specs=in_specs,
        out_specs=tok(D_MODEL),
        out_shape=jax.ShapeDtypeStruct((t, D_MODEL), F32),
        scratch_shapes=[pltpu.VMEM((D_MODEL, 2 * D_FF), BF16), pltpu.VMEM((D_FF, D_MODEL), BF16)],
        compiler_params=_cparams(("arbitrary",), resident, per_step, values),
        name=f"ffn{j // 2 + 1}" + (f"_{mixer}" if mixer else ""),
    )(*args)


def _scan_rows(x, op, fill, reverse):
    n = x.shape[0]
    idx = lax.broadcasted_iota(jnp.int32, x.shape, 0)
    s = 1
    while s < n:
        if reverse:
            x = op(x, jnp.where(idx < n - s, pltpu.roll(x, n - s, axis=0), fill))
        else:
            x = op(x, jnp.where(idx >= s, pltpu.roll(x, s, axis=0), fill))
        s *= 2
    return x


def _mlstm_in_kernel(x_ref, mod_ref, g_ref, wq_ref, wkt_ref, wvo_ref, wg_ref, bg_ref,
                     q_ref, kt_ref, v_ref, o_ref, scan_ref):
    mod = mod_ref[0]
    hb = _mod_norm(x_ref[...], g_ref[...], mod[3:4], mod[4:5]).astype(BF16)
    gt = _dot(hb, wg_ref[...]) + bg_ref[...]
    q_ref[...] = (_dot(hb, wq_ref[...]) * DK_M ** -0.5).astype(BF16)
    kt_ref[...] = _dot_nt(wkt_ref[...], hb).astype(BF16)
    vo = _dot(hb, wvo_ref[...])
    v_ref[...] = vo[:, :V_W].astype(BF16)
    o_ref[...] = vo[:, V_W:].astype(BF16)
    n = MLSTM_L
    fwd = lax.broadcasted_iota(jnp.int32, (n, LANES), 1) < GATE_DIR_LANES
    for c in range(gt.shape[0] // n):
        rows = slice(c * n, (c + 1) * n)
        li = gt[rows, :LANES]
        fx = gt[rows, LANES:]
        lf = jnp.minimum(fx, 0.0) - jnp.log(1.0 + jnp.exp(-jnp.abs(fx)))
        pre = _scan_rows(lf, jnp.add, 0.0, False)
        b = jnp.where(fwd, pre, pre[n - 1:n, :] - pre + lf)
        a = li - b
        scan_ref[rows, 0:LANES] = a
        scan_ref[rows, LANES:2 * LANES] = b
        scan_ref[rows, 2 * LANES:] = jnp.where(fwd, _scan_rows(a, jnp.maximum, -jnp.inf, False),
                                               _scan_rows(a, jnp.maximum, -jnp.inf, True))


def _mlstm_in_call(x, mod, g, wq, wkt, wvo, wg, bg, seq_len, tm):
    t = x.shape[0]
    out_cols = 2 * QK_W + 2 * V_W
    resident = 2 * D_MODEL * (out_cols + 2 * LANES)
    per_step = tm * (D_MODEL * 4 + out_cols * 2 + 3 * LANES * 4)
    values = tm * (D_MODEL * 2 + out_cols * 4 + 5 * LANES * 4)
    return pl.pallas_call(
        _mlstm_in_kernel,
        grid=(t // tm,),
        in_specs=[
            pl.BlockSpec((tm, D_MODEL), lambda i: (i, 0)),
            _mod_spec(mod, seq_len // tm),
            _row_spec(D_MODEL),
            _resident((D_MODEL, QK_W)),
            _resident((QK_W, D_MODEL)),
            _resident((D_MODEL, 2 * V_W)),
            _resident((D_MODEL, 2 * LANES)),
            _row_spec(2 * LANES),
        ],
        out_specs=[
            pl.BlockSpec((tm, QK_W), lambda i: (i, 0)),
            pl.BlockSpec((QK_W, tm), lambda i: (0, i)),
            pl.BlockSpec((tm, V_W), lambda i: (i, 0)),
            pl.BlockSpec((tm, V_W), lambda i: (i, 0)),
            pl.BlockSpec((tm, 3 * LANES), lambda i: (i, 0)),
        ],
        out_shape=[
            jax.ShapeDtypeStruct((t, QK_W), BF16),
            jax.ShapeDtypeStruct((QK_W, t), BF16),
            jax.ShapeDtypeStruct((t, V_W), BF16),
            jax.ShapeDtypeStruct((t, V_W), BF16),
            jax.ShapeDtypeStruct((t, 3 * LANES), F32),
        ],
        compiler_params=_cparams(("parallel",), resident, per_step, values),
        name="mlstm_in",
    )(x, mod, g, wq, wkt, wvo, wg, bg)


def _bf16_part(x):
    return x.astype(BF16).astype(F32)


def _mlstm_gates(d, scan_ref, rows, m):
    n = MLSTM_L
    a, b, cm = scan_ref[rows, 0:LANES], scan_ref[rows, LANES:2 * LANES], scan_ref[rows, 2 * LANES:]
    rev = d == 1
    mt = jnp.maximum(cm, m) * LOG2_E
    u_hi = _bf16_part(mt)
    u_lo = _bf16_part(mt - u_hi)
    z = -(b * LOG2_E + (u_hi + u_lo))
    z_hi = _bf16_part(z)
    z_mid = _bf16_part(z - z_hi)
    z_lo = _bf16_part(z - z_hi - z_mid)
    lane = lax.broadcasted_iota(jnp.int32, mt.shape, 1)
    k = (lane - d * GATE_DIR_LANES) // NH_M
    terms = jnp.where(k == 0, u_hi, jnp.where(k == 1, u_lo, jnp.where(k == 2, z_hi, jnp.where(k == 3, z_mid, z_lo))))
    end = 0 if rev else n - 1
    mx = jnp.maximum(m, cm[end:end + 1, :])
    keep = jnp.exp(m - mx)
    m_new = b[end:end + 1, :] + mx
    return terms, (a * LOG2_E).T, jnp.exp(a - mx).T, keep, m_new


def _mlstm_heads(dirs, c_sc, spread, m2):
    n = MLSTM_L
    row = lax.broadcasted_iota(jnp.int32, (n, n), 0)
    col = lax.broadcasted_iota(jnp.int32, (n, n), 1)
    ones = jnp.ones((n, LANES), BF16)
    heads = [(dd, h) for dd in dirs for h in range(NH_M)]
    kpad = jnp.zeros((DK_M, DK_M), BF16)
    lhs = []
    for (d, q_ref, kt_ref, _, _, rows, a_t, wend_t, _), h in heads:
        c = d * GATE_DIR_LANES + h
        base = (d * NH_M + h) * 2 * LANES
        u = spread[:, base:base + LANES]
        qh = q_ref[rows, h * DK_M:(h + 1) * DK_M]
        kth = kt_ref[h * DK_M:(h + 1) * DK_M, rows]
        e = jnp.exp2(a_t[c:c + 1, :] - u)
        mask = (col >= row) if d == 1 else (col <= row)
        w = jnp.where(mask, _dot(qh, kth) * e, 0.0).astype(BF16)
        inter = jnp.exp2(m2[:, c:c + 1] - u)
        qi = (inter[:, :DK_M] * qh.astype(F32)).astype(BF16)
        kts = (kth.astype(F32) * wend_t[c:c + 1, :]).astype(BF16)
        lhs.append(jnp.concatenate([jnp.concatenate([w, qi], axis=1), jnp.concatenate([kts, kpad], axis=1)], axis=0))
    for ((d, _, _, v_ref, h_ref, rows, _, _, keep), h), wq in zip(heads, lhs):
        c = d * GATE_DIR_LANES + h
        base = (d * NH_M + h) * 2 * LANES
        zz = spread[:, base + LANES:base + 2 * LANES]
        vext = jnp.concatenate([v_ref[rows, h * DV_M:(h + 1) * DV_M], ones], axis=1)
        cext = c_sc[d, h]
        r = _dot(wq, jnp.concatenate([vext, cext.astype(BF16)], axis=0))
        den = jnp.maximum(jnp.abs(r[:n, DV_M:]), jnp.exp2(zz))
        h_ref[rows, h * DV_M:(h + 1) * DV_M] = (r[:n, :DV_M] / den).astype(BF16)
        c_sc[d, h] = keep[:, c:c + 1] * cext + r[n:]


def _mlstm_kernel(*refs, zero_init):
    qf_ref, ktf_ref, vf_ref, scanf_ref, qb_ref, ktb_ref, vb_ref, scanb_ref, sel_ref = refs[:9]
    hf_ref, hb_ref, cout_ref, nout_ref, mout_ref, c_sc, m_sc = refs[-7:]
    i = pl.program_id(1)

    @pl.when(i == 0)
    def _():
        if zero_init:
            c_sc[...] = jnp.zeros(c_sc.shape, F32)
            m_sc[...] = jnp.zeros(m_sc.shape, F32)
        else:
            c0_ref, n0_ref, m0_ref = refs[9:12]
            c_sc[:, :, :, :DV_M] = c0_ref[0]
            c_sc[:, :, :, DV_M:] = n0_ref[0]
            m_sc[...] = m0_ref[0]

    nsub = qf_ref.shape[0] // MLSTM_L
    for sub in range(nsub):
        rows_f = slice(sub * MLSTM_L, (sub + 1) * MLSTM_L)
        rows_b = slice((nsub - 1 - sub) * MLSTM_L, (nsub - sub) * MLSTM_L)
        m = m_sc[...]
        terms_f, at_f, wend_f, keep_f, mnew_f = _mlstm_gates(0, scanf_ref, rows_f, m)
        terms_b, at_b, wend_b, keep_b, mnew_b = _mlstm_gates(1, scanb_ref, rows_b, m)
        lane = lax.broadcasted_iota(jnp.int32, terms_f.shape, 1)
        packed = jnp.where(lane < GATE_DIR_LANES, terms_f, jnp.where(lane < 2 * GATE_DIR_LANES, terms_b, 0.0))
        spread = _dot(packed.astype(BF16), sel_ref[...])
        m2 = m * LOG2_E
        _mlstm_heads([(0, qf_ref, ktf_ref, vf_ref, hf_ref, rows_f, at_f, wend_f, keep_f),
                      (1, qb_ref, ktb_ref, vb_ref, hb_ref, rows_b, at_b, wend_b, keep_b)], c_sc, spread, m2)
        lane_m = lax.broadcasted_iota(jnp.int32, m.shape, 1)
        m_sc[...] = jnp.where(lane_m < GATE_DIR_LANES, mnew_f, jnp.where(lane_m < 2 * GATE_DIR_LANES, mnew_b, m))

    @pl.when(i == pl.num_programs(1) - 1)
    def _():
        cout_ref[0] = c_sc[:, :, :, :DV_M]
        nout_ref[0] = c_sc[:, :, :, DV_M:]
        mout_ref[0] = m_sc[...]


def _mlstm_core_call(q, kt, v, scans, sel, init, bsz, seq_len):
    t = q.shape[0]
    rows = min(MLSTM_STEP_CHUNKS * MLSTM_L, seq_len)
    nc = seq_len // rows
    fwd = lambda b, i: (b * nc + i, 0)
    bwd = lambda b, i: (b * nc + nc - 1 - i, 0)
    fwd_t = lambda b, i: (0, b * nc + i)
    bwd_t = lambda b, i: (0, b * nc + nc - 1 - i)

    def specs(row_map, col_map):
        return [
            pl.BlockSpec((rows, QK_W), row_map),
            pl.BlockSpec((QK_W, rows), col_map),
            pl.BlockSpec((rows, V_W), row_map),
            pl.BlockSpec((rows, 3 * LANES), row_map),
        ]

    half_shape = (2, NH_M, DK_M, DV_M)
    half_spec = pl.BlockSpec((1,) + half_shape, lambda b, i: (b, 0, 0, 0, 0))
    m_spec = pl.BlockSpec((1, 1, LANES), lambda b, i: (b, 0, 0))
    state_bytes = 2 * NH_M * DK_M * 2 * DV_M * 4
    sel_shape = (LANES, 2 * NH_M * 2 * LANES)
    in_specs = specs(fwd, fwd_t) + specs(bwd, bwd_t) + [pl.BlockSpec(sel_shape, lambda b, i: (0, 0))]
    args = [q, kt, v, scans, q, kt, v, scans, sel]
    if init is not None:
        in_specs += [half_spec, half_spec, m_spec]
        args += list(init)
    per_step = 2 * rows * ((2 * QK_W + 2 * V_W) * 2 + 3 * LANES * 4) + 2 * sel_shape[0] * sel_shape[1] + 2 * state_bytes
    values = MLSTM_L * sel_shape[1] * 4 + 2 * NH_M * MLSTM_L * 4 * LANES * 4
    return pl.pallas_call(
        functools.partial(_mlstm_kernel, zero_init=init is None),
        grid=(bsz, nc),
        in_specs=in_specs,
        out_specs=[pl.BlockSpec((rows, V_W), fwd), pl.BlockSpec((rows, V_W), bwd), half_spec, half_spec, m_spec],
        out_shape=[
            jax.ShapeDtypeStruct((t, V_W), BF16),
            jax.ShapeDtypeStruct((t, V_W), BF16),
            jax.ShapeDtypeStruct((bsz,) + half_shape, F32),
            jax.ShapeDtypeStruct((bsz,) + half_shape, F32),
            jax.ShapeDtypeStruct((bsz, 1, LANES), F32),
        ],
        scratch_shapes=[pltpu.VMEM((2, NH_M, DK_M, 2 * DV_M), F32), pltpu.VMEM((1, LANES), F32)],
        compiler_params=_cparams(("parallel", "arbitrary"), state_bytes, per_step, values),
        name="mlstm_core",
    )(*args)


def _rope_rows_swap(x):
    q = HD_A // 4
    parts = []
    for base in range(0, x.shape[0], 2 * q):
        parts += [x[base + q:base + 2 * q], x[base:base + q]]
    return jnp.concatenate(parts, axis=0)


def _attn_in_kernel(*refs, rope, emit_f32):
    x_ref, mod_ref, g_ref, wqt_ref, wk_ref, wvt_ref = refs[:6]
    pos = 6
    if rope:
        cos_ref, sina_ref, sinb_ref, cost_ref, sint_ref = refs[pos:pos + 5]
        pos += 5
    if emit_f32:
        wv_ref = refs[pos]
        pos += 1
    qt_ref, k_ref, vt_ref = refs[pos:pos + 3]
    pos += 3
    mod = mod_ref[0]
    hb = _mod_norm(x_ref[...], g_ref[...], mod[3:4], mod[4:5]).astype(BF16)
    qt = _dot_nt(wqt_ref[...], hb)
    k = _dot(hb, wk_ref[...])
    blk = ATTN_BLOCK
    vt = _dot_nt(wvt_ref[...], hb).astype(BF16)
    for tb in range(vt.shape[1] // blk):
        vt_ref[tb] = vt[:, tb * blk:(tb + 1) * blk]
    if emit_f32:
        kf_ref, vf_ref = refs[pos:pos + 2]
        kf_ref[...] = k
        vf_ref[...] = _dot(hb, wv_ref[...])
    for gi in range(Q_W // LANES):
        qg = qt[gi * LANES:(gi + 1) * LANES, :]
        if rope:
            cos_t = jnp.concatenate([cost_ref[...]] * (LANES // HD_A), axis=0)
            sin_t = jnp.concatenate([sint_ref[...]] * (LANES // HD_A), axis=0)
            qg = qg * cos_t + _rope_rows_swap(qg) * sin_t
        qg = (qg * (HD_A ** -0.5 * LOG2_E)).astype(BF16)
        p, g = gi // G_A, gi % G_A
        for tb in range(qt.shape[1] // blk):
            qt_ref[tb, p, :, g * blk:(g + 1) * blk] = qg[:, tb * blk:(tb + 1) * blk]
    for gi in range(KV_W // LANES):
        kg = k[:, gi * LANES:(gi + 1) * LANES]
        if rope:
            kg = (kg * cos_ref[...] + pltpu.roll(kg, LANES - HD_A // 4, axis=1) * sina_ref[...]
                  + pltpu.roll(kg, HD_A // 4, axis=1) * sinb_ref[...])
        k_ref[:, gi * LANES:(gi + 1) * LANES] = kg.astype(BF16)


def _attn_in_call(x, mod, g, wqt, wk, wvt, seq_len, tm, rope_tabs=None, wv=None):
    t = x.shape[0]
    rope = rope_tabs is not None
    emit_f32 = wv is not None
    tiles = seq_len // tm
    tok = lambda width: pl.BlockSpec((tm, width), lambda i: (i, 0))
    in_specs = [tok(D_MODEL), _mod_spec(mod, tiles), _row_spec(D_MODEL),
                _resident((Q_W, D_MODEL)), _resident((D_MODEL, KV_W)), _resident((KV_W, D_MODEL))]
    args = [x, mod, g, wqt, wk, wvt]
    if rope:
        pos_row = lambda i: (i % tiles, 0)
        pos_col = lambda i: (0, i % tiles)
        in_specs += [pl.BlockSpec((tm, LANES), pos_row)] * 3 + [pl.BlockSpec((HD_A, tm), pos_col)] * 2
        args += list(rope_tabs)
    if emit_f32:
        in_specs.append(_resident((D_MODEL, KV_W)))
        args.append(wv)
    nblk = tm // ATTN_BLOCK
    qt_slab = (NKV_A // 2, LANES, G_A * ATTN_BLOCK)
    vt_slab = (KV_W, ATTN_BLOCK)
    out_specs = [pl.BlockSpec((nblk,) + qt_slab, lambda i: (i, 0, 0, 0)), tok(KV_W),
                 pl.BlockSpec((nblk,) + vt_slab, lambda i: (i, 0, 0))]
    out_shape = [jax.ShapeDtypeStruct((t // ATTN_BLOCK,) + qt_slab, BF16), jax.ShapeDtypeStruct((t, KV_W), BF16),
                 jax.ShapeDtypeStruct((t // ATTN_BLOCK,) + vt_slab, BF16)]
    if emit_f32:
        out_specs += [tok(KV_W), tok(KV_W)]
        out_shape += [jax.ShapeDtypeStruct((t, KV_W), F32)] * 2
    resident = 2 * D_MODEL * (Q_W + 3 * KV_W)
    per_step = tm * (D_MODEL * 4 + (Q_W + 2 * KV_W) * 2 + 2 * KV_W * 4 + (3 * LANES + 2 * HD_A) * 4)
    values = tm * (D_MODEL * 2 + (Q_W + 3 * KV_W) * 4 + Q_W * 4)
    return pl.pallas_call(
        functools.partial(_attn_in_kernel, rope=rope, emit_f32=emit_f32),
        grid=(t // tm,),
        in_specs=in_specs,
        out_specs=out_specs,
        out_shape=out_shape,
        compiler_params=_cparams(("parallel",), resident, per_step, values),
        name="attn_in",
    )(*args)


def _block_attention(sink_ref, qt_ref, o_ref, sub, pair_keys, mask_of):
    blk = ATTN_BLOCK
    cols_all = G_A * blk
    col_g = lax.broadcasted_iota(jnp.int32, (1, cols_all), 1) // blk
    staged = []
    for p, keys in enumerate(pair_keys):
        qp = qt_ref[sub, p]
        row = lax.broadcasted_iota(jnp.int32, qp.shape, 0)
        for e in range(2):
            kv = 2 * p + e
            qm = jnp.where((row >= e * HD_A) & (row < (e + 1) * HD_A), qp, jnp.zeros_like(qp))
            sink = jnp.full((1, cols_all), sink_ref[0, kv * G_A], F32)
            for g in range(1, G_A):
                sink = jnp.where(col_g == g, sink_ref[0, kv * G_A + g], sink)
            sink = sink * LOG2_E
            scores = []
            m = sink
            for i, (k, _) in enumerate(keys):
                s = _dot(k, qm)
                msk = mask_of(i)
                s = s if msk is None else jnp.where(msk, s, NEG_INF)
                m = jnp.maximum(m, jnp.max(s, axis=0, keepdims=True))
                scores.append(s)
            staged.append((scores, m, sink))
    outs = []
    for idx, (scores, m, sink) in enumerate(staged):
        acc = None
        for s, (_, vt) in zip(scores, pair_keys[idx // 2]):
            vx = jnp.concatenate([vt, jnp.ones((ONES_ROWS, vt.shape[1]), BF16)], axis=0)
            pv = _dot(vx, jnp.exp2(s - m).astype(BF16))
            acc = pv if acc is None else acc + pv
        outs.append(acc[:LANES] / (acc[LANES:LANES + 1] + jnp.exp2(sink - m)))
    row_o = lax.broadcasted_iota(jnp.int32, outs[0].shape, 0)
    for p in range(len(pair_keys)):
        a_t = jnp.where(row_o < HD_A, outs[2 * p], outs[2 * p + 1])
        for g in range(G_A):
            c0 = (p * G_A + g) * LANES
            o_ref[sub * blk:(sub + 1) * blk, c0:c0 + LANES] = a_t[:, g * blk:(g + 1) * blk].T.astype(BF16)


def _attn_latent_kernel(sink_ref, qt_ref, kp_ref, kc_ref, kn_ref, vtp_ref, vtc_ref, vtn_ref, kx_ref, vtx_ref,
                        o_ref, *, n_tok):
    blk = ATTN_BLOCK
    nsub = ATTN_STEP_BLOCKS
    shape = (3 * blk, G_A * blk)
    krow = lax.broadcasted_iota(jnp.int32, shape, 0)
    band = jnp.abs(blk + lax.broadcasted_iota(jnp.int32, shape, 1) % blk - krow) <= WINDOW
    for sub in range(nsub):
        j = pl.program_id(1) * nsub + sub
        local = band
        if sub == 0:
            local = local & ((krow >= blk) | (j > 0))
        if sub == nsub - 1:
            local = local & ((krow < 2 * blk) | (j < n_tok // blk - 1))
        pair_keys = []
        for p in range(NKV_A // 2):
            pl_ = slice(p * LANES, (p + 1) * LANES)
            k_blocks = ([kp_ref[:, pl_]] + [kc_ref[c * blk:(c + 1) * blk, pl_] for c in range(nsub)]
                        + [kn_ref[:, pl_]])[sub:sub + 3]
            vt_blocks = ([vtp_ref[0, pl_, :]] + [vtc_ref[c, pl_, :] for c in range(nsub)]
                         + [vtn_ref[0, pl_, :]])[sub:sub + 3]
            pair_keys.append([(jnp.concatenate(k_blocks, axis=0), jnp.concatenate(vt_blocks, axis=1)),
                              (kx_ref[0, :, pl_], vtx_ref[0, pl_, :])])
        _block_attention(sink_ref, qt_ref, o_ref, sub, pair_keys, lambda i, local=local: local if i == 0 else None)


def _attn_latent_call(qt, k, vt, kx, vtx, sink, bsz, seq_len):
    t = k.shape[0]
    nsub = ATTN_STEP_BLOCKS
    nb = seq_len // ATTN_BLOCK
    ns = nb // nsub
    n_ctx = kx.shape[1]
    prev = lambda b, j: b * nb + jnp.maximum(j * nsub - 1, 0)
    nxt = lambda b, j: b * nb + jnp.minimum(j * nsub + nsub, nb - 1)
    cur = lambda b, j: b * ns + j
    return pl.pallas_call(
        functools.partial(_attn_latent_kernel, n_tok=seq_len),
        grid=(bsz, ns),
        in_specs=[
            pl.BlockSpec(memory_space=pltpu.SMEM),
            pl.BlockSpec((nsub, NKV_A // 2, LANES, G_A * ATTN_BLOCK), lambda b, j: (cur(b, j), 0, 0, 0)),
            pl.BlockSpec((ATTN_BLOCK, KV_W), lambda b, j: (prev(b, j), 0)),
            pl.BlockSpec((nsub * ATTN_BLOCK, KV_W), lambda b, j: (cur(b, j), 0)),
            pl.BlockSpec((ATTN_BLOCK, KV_W), lambda b, j: (nxt(b, j), 0)),
            pl.BlockSpec((1, KV_W, ATTN_BLOCK), lambda b, j: (prev(b, j), 0, 0)),
            pl.BlockSpec((nsub, KV_W, ATTN_BLOCK), lambda b, j: (cur(b, j), 0, 0)),
            pl.BlockSpec((1, KV_W, ATTN_BLOCK), lambda b, j: (nxt(b, j), 0, 0)),
            pl.BlockSpec((1, n_ctx, KV_W), lambda b, j: (b, 0, 0)),
            pl.BlockSpec((1, KV_W, n_ctx), lambda b, j: (b, 0, 0)),
        ],
        out_specs=pl.BlockSpec((nsub * ATTN_BLOCK, Q_W), lambda b, j: (cur(b, j), 0)),
        out_shape=jax.ShapeDtypeStruct((t, Q_W), BF16),
        compiler_params=_cparams(
            ("parallel", "parallel"),
            per_step=(nsub * ATTN_BLOCK * (2 * Q_W + 2 * KV_W) + 2 * (2 * ATTN_BLOCK + n_ctx) * KV_W) * 2,
            values=nsub * NKV_A * (3 * ATTN_BLOCK + n_ctx) * G_A * ATTN_BLOCK * 6),
        name="attn_latent",
    )(sink, qt, k, k, k, vt, vt, vt, kx, vtx)


def _attn_context_kernel(sink_ref, qt_ref, k_ref, vt_ref, o_ref):
    pair_keys = []
    for p in range(NKV_A // 2):
        pl_ = slice(p * LANES, (p + 1) * LANES)
        vt = jnp.concatenate([vt_ref[c, pl_, :] for c in range(vt_ref.shape[0])], axis=1)
        pair_keys.append([(k_ref[:, pl_], vt)])
    for sub in range(qt_ref.shape[0]):
        _block_attention(sink_ref, qt_ref, o_ref, sub, pair_keys, lambda i: None)


def _attn_context_call(qt, k, vt, sink, bsz, seq_len):
    t = k.shape[0]
    nb = seq_len // ATTN_BLOCK
    return pl.pallas_call(
        _attn_context_kernel,
        grid=(bsz,),
        in_specs=[
            pl.BlockSpec(memory_space=pltpu.SMEM),
            pl.BlockSpec((nb, NKV_A // 2, LANES, G_A * ATTN_BLOCK), lambda b: (b, 0, 0, 0)),
            pl.BlockSpec((seq_len, KV_W), lambda b: (b, 0)),
            pl.BlockSpec((nb, KV_W, ATTN_BLOCK), lambda b: (b, 0, 0)),
        ],
        out_specs=pl.BlockSpec((seq_len, Q_W), lambda b: (b, 0)),
        out_shape=jax.ShapeDtypeStruct((t, Q_W), BF16),
        compiler_params=_cparams(("parallel",), per_step=seq_len * (2 * Q_W + 2 * KV_W) * 2,
                                 values=nb * NKV_A * seq_len * G_A * ATTN_BLOCK * 6),
        name="attn_context",
    )(sink, qt, k, vt)


def _rope_tables(n_tok):
    quarter = HD_A // 4
    freqs = ROPE_THETA ** (-jnp.arange(quarter, dtype=F32) / quarter)
    pos = jnp.arange(n_tok)
    row = (pos // GRID_W).astype(F32)
    col = (pos % GRID_W).astype(F32)
    ang_r, ang_c = row[:, None] * freqs, col[:, None] * freqs
    ang = jnp.concatenate([ang_r, ang_r, ang_c, ang_c], axis=-1)
    cos, sin = jnp.cos(ang), jnp.sin(ang)
    first = (jnp.arange(HD_A) % (2 * quarter)) < quarter
    sin_a = jnp.where(first, -sin, 0.0)
    sin_b = jnp.where(first, 0.0, sin)
    sin_t = jnp.where(first, -sin, sin)
    lane_tile = lambda a: jnp.tile(a, (1, LANES // HD_A))
    return lane_tile(cos), lane_tile(sin_a), lane_tile(sin_b), cos.T, sin_t.T


def _gate_lanes(a):
    lead = a.shape[:-2]
    rep = jnp.broadcast_to(a[..., :, None, :], lead + (2, GATE_COPIES, NH_M)).reshape(lead + (2 * GATE_DIR_LANES,))
    return jnp.concatenate([rep, jnp.zeros(lead + (LANES - 2 * GATE_DIR_LANES,), a.dtype)], axis=-1)


def _gate_layout(a):
    a4 = a.reshape(a.shape[:-1] + (2, 2, NH_M))
    return jnp.concatenate([_gate_lanes(a4[..., :, 0, :]), _gate_lanes(a4[..., :, 1, :])], axis=-1)


def _gate_selector():
    r = jnp.arange(LANES)[:, None]
    c = jnp.arange(2 * NH_M * 2 * LANES)[None, :]
    hd, j = c // (2 * LANES), c % (2 * LANES)
    rd, rk, rh = r // GATE_DIR_LANES, (r % GATE_DIR_LANES) // NH_M, r % NH_M
    hit = (r < 2 * GATE_DIR_LANES) & (rd == hd // NH_M) & (rh == hd % NH_M) & ((j < LANES) == (rk < 2))
    return hit.astype(BF16)


def _pair_layout_cols(w):
    d_in = w.shape[0]
    w5 = w.reshape(d_in, NKV_A // 2, 2, G_A, HD_A)
    return jnp.transpose(w5, (0, 1, 3, 2, 4)).reshape(d_in, Q_W)


def _mlstm_init(state_c, state_n, state_m):
    n_rep = jnp.broadcast_to(state_n[..., None].astype(F32), state_n.shape + (DV_M,))
    return state_c.astype(F32), n_rep, _gate_lanes(state_m.astype(F32))[:, None, :]


def kernel(x_prompt, x_sample, state_c, state_n, state_m, cache_k, cache_v, c, c_ctx, w_mod, b_mod, norm_g,
           ffn1_w_gu, ffn1_w_down, ffn2_w_gu, ffn2_w_down, mlstm_w_in, mlstm_b_gate, mlstm_g_head, mlstm_w_out,
           attn_w_in, attn_sink, attn_w_out, final_g):
    bp, n_p, _ = x_prompt.shape
    bs, n_s, _ = x_sample.shape
    xp = x_prompt.reshape(bp * n_p, D_MODEL)
    xs = x_sample.reshape(bs * n_s, D_MODEL)
    tm_p, tm_s = 512, 512
    tm_ffn1 = tm_in = 1024

    cond = jnp.concatenate([c_ctx[None, :], c], axis=0)
    cond = jnp.pad(cond, ((0, COND_ROWS - cond.shape[0]), (0, 0)))
    mod_all = _mod_call(cond, w_mod, b_mod).reshape(DEPTH, COND_ROWS, N_MOD, D_MODEL)

    outs = {}
    for l in range(DEPTH):
        mod_p = mod_all[l, 0:1]
        mod_s = mod_all[l, 1:1 + bs]
        g = norm_g[l]
        xp = _ffn_call(xp, mod_p, g[0:1], ffn1_w_gu, ffn1_w_down, l, 0, n_p, tm_ffn1)
        xs = _ffn_call(xs, mod_s, g[0:1], ffn1_w_gu, ffn1_w_down, l, 0, n_s, tm_ffn1)
        i = l // 2
        if l % 2 == 0:
            w_in = mlstm_w_in[i]
            wq = w_in[:, :QK_W].astype(BF16)
            wkt = w_in[:, QK_W:2 * QK_W].T.astype(BF16)
            wvo = w_in[:, 2 * QK_W:2 * QK_W + 2 * V_W].astype(BF16)
            wg = _gate_layout(w_in[:, 2 * QK_W + 2 * V_W:]).astype(BF16)
            bg = _gate_layout(mlstm_b_gate[i].astype(F32))[None, :]
            gh = mlstm_g_head[i].astype(F32)[None, :]
            w_out = mlstm_w_out[i].astype(BF16)
            sel = _gate_selector()
            streams = [("p", xp, mod_p, n_p, tm_in, bp, None),
                       ("s", xs, mod_s, n_s, tm_in, bs, _mlstm_init(state_c[:, i], state_n[:, i], state_m[:, i]))]
            res = {}
            for tag, x, mod, n_tok, tm, bsz, init in streams:
                q, kt, v, o, scans = _mlstm_in_call(x, mod, g[1:2], wq, wkt, wvo, wg, bg, n_tok, tm)
                hf, hb, c_fin, n_fin, m_fin = _mlstm_core_call(q, kt, v, scans, sel, init, bsz, n_tok)
                res[tag] = ((hf, hb, o, gh, w_out), c_fin, n_fin, m_fin)
            mix_p, c_fin, n_fin, m_fin = res["p"]
            mix_s = res["s"][0]
            mix_key = "mlstm"
            dt = x_prompt.dtype
            m_heads = m_fin[:, 0, :2 * GATE_DIR_LANES].reshape(bp, 2, GATE_COPIES, NH_M)[:, :, 0, :]
            outs.setdefault("c", []).append(c_fin.astype(dt))
            outs.setdefault("n", []).append(n_fin[..., 0].astype(dt))
            outs.setdefault("m", []).append(m_heads.astype(dt))
        else:
            w_in = attn_w_in[i]
            wqt = _pair_layout_cols(w_in[:, :Q_W]).T.astype(BF16)
            wk = w_in[:, Q_W:Q_W + KV_W].astype(BF16)
            wv = w_in[:, Q_W + KV_W:].astype(BF16)
            wvt = w_in[:, Q_W + KV_W:].T.astype(BF16)
            w_out = _pair_layout_cols(attn_w_out[i].T).T.astype(BF16)
            sink = attn_sink[i].astype(F32)[None, :]
            qt, k, vt, kf, vf = _attn_in_call(xp, mod_p, g[1:2], wqt, wk, wvt, n_p, tm_in, wv=wv)
            mix_p = (_attn_context_call(qt, k, vt, sink, bp, n_p), w_out)
            mix_key = "attn"
            outs.setdefault("k", []).append(kf.reshape(bp, n_p, NKV_A, HD_A))
            outs.setdefault("v", []).append(vf.reshape(bp, n_p, NKV_A, HD_A))
            qt, k, vt = _attn_in_call(xs, mod_s, g[1:2], wqt, wk, wvt, n_s, tm_in, rope_tabs=_rope_tables(n_s))
            n_ctx = cache_k.shape[2]
            kx = cache_k[:, i].reshape(bs, n_ctx, KV_W).astype(BF16)
            vtx = jnp.swapaxes(cache_v[:, i].reshape(bs, n_ctx, KV_W), 1, 2).astype(BF16)
            mix_s = (_attn_latent_call(qt, k, vt, kx, vtx, sink, bs, n_s), w_out)
        fg = final_g[None, :] if l == DEPTH - 1 else None
        xp = _ffn_call(xp, mod_p, g[2:3], ffn2_w_gu, ffn2_w_down, l, 2, n_p, tm_p, final_g=fg, **{mix_key: mix_p})
        xs = _ffn_call(xs, mod_s, g[2:3], ffn2_w_gu, ffn2_w_down, l, 2, n_s, tm_s, final_g=fg, **{mix_key: mix_s})

    return (xp.reshape(bp, n_p, D_MODEL), xs.reshape(bs, n_s, D_MODEL),
            jnp.stack(outs["c"], axis=1), jnp.stack(outs["n"], axis=1), jnp.stack(outs["m"], axis=1),
            jnp.stack(outs["k"], axis=1), jnp.stack(outs["v"], axis=1))
```

```python
import functools

import jax
import jax.numpy as jnp
from jax import lax
from jax.experimental import pallas as pl
from jax.experimental.pallas import tpu as pltpu

D_MODEL = 1024
DEPTH = 2
GRID_W = 64
D_FF = 2816
FFN_RES = 0.5
NH_M = 8
DK_M = 64
DV_M = 128
NH_A = 16
NKV_A = 4
G_A = NH_A // NKV_A
HD_A = 64
WINDOW = 128
ATTN_BLOCK = 128
ROPE_THETA = 10000.0
EPS = 1e-6
NEG_INF = -1e30
N_MOD = 9

QK_W = NH_M * DK_M
V_W = NH_M * DV_M
KV_W = NKV_A * HD_A
Q_W = NH_A * HD_A

LANES = 128
MLSTM_L = 128
MLSTM_STEP_CHUNKS = 8
FFN_CHUNK = 256
FFN_W_STEPS = D_FF // FFN_CHUNK
FFN_WGU_COLS = 2 * D_FF // FFN_W_STEPS
FFN_WD_ROWS = D_FF // FFN_W_STEPS
ATTN_STEP_BLOCKS = 8
ONES_ROWS = 16
LOG2_E = 1.4426950408889634
GATE_COPIES = 5
GATE_DIR_LANES = GATE_COPIES * NH_M
COND_ROWS = 16
VMEM_CAP = 60000 * 1024
COMPILER_TEMP_BYTES = 8 << 20

F32 = jnp.float32
BF16 = jnp.bfloat16


def _cparams(sem, resident=0, per_step=0, values=0):
    need = resident + 2 * per_step + values + COMPILER_TEMP_BYTES
    return pltpu.CompilerParams(dimension_semantics=sem, vmem_limit_bytes=min(int(need), VMEM_CAP))


def _dot(a, b):
    return jnp.dot(a, b, preferred_element_type=F32)


def _dot_nt(a, b):
    return lax.dot_general(a, b, (((1,), (1,)), ((), ())), preferred_element_type=F32)


def _rms(x):
    return x * lax.rsqrt(jnp.mean(x * x, axis=-1, keepdims=True) + EPS)


def _sigmoid(x):
    return 1.0 / (1.0 + jnp.exp(-x))


def _mod_norm(x, g, shift, scale):
    return (_rms(x) * g) * (1.0 + scale) + shift


def _mod_kernel(c_ref, w_ref, b_ref, o_ref):
    c = c_ref[...]
    s = (c * _sigmoid(c)).astype(BF16)
    o_ref[0] = _dot(s, w_ref[0].astype(BF16)) + b_ref[0]


def _mod_call(cond, w_mod, b_mod):
    tn = D_MODEL
    n_out = N_MOD * D_MODEL
    return pl.pallas_call(
        _mod_kernel,
        grid=(DEPTH, n_out // tn),
        in_specs=[
            pl.BlockSpec((COND_ROWS, D_MODEL), lambda l, n: (0, 0)),
            pl.BlockSpec((1, D_MODEL, tn), lambda l, n: (l, 0, n)),
            pl.BlockSpec((1, 1, tn), lambda l, n: (l, 0, n)),
        ],
        out_specs=pl.BlockSpec((1, COND_ROWS, tn), lambda l, n: (l, 0, n)),
        out_shape=jax.ShapeDtypeStruct((DEPTH, COND_ROWS, n_out), F32),
        compiler_params=_cparams(("parallel", "parallel"), per_step=D_MODEL * tn * 4, values=D_MODEL * tn * 2),
        name="adaln_mod",
    )(cond, w_mod, b_mod.reshape(DEPTH, 1, n_out))


def _mod_spec(mod, tiles_per_seq, tile=lambda i: i):
    if mod.shape[0] == 1:
        return pl.BlockSpec((1, N_MOD, D_MODEL), lambda i: (0, 0, 0))
    return pl.BlockSpec((1, N_MOD, D_MODEL), lambda i: (tile(i) // tiles_per_seq, 0, 0))


def _row_spec(width):
    return pl.BlockSpec((1, width), lambda i: (0, 0))


def _resident(shape):
    return pl.BlockSpec(shape, lambda i: (0,) * len(shape), pipeline_mode=pl.Buffered(1))


def _mlstm_gated_heads(hf_ref, hb_ref, o_ref, gh_ref):
    hs = hf_ref[...].astype(F32) + hb_ref[...].astype(F32)
    parts = [_rms(hs[:, h * DV_M:(h + 1) * DV_M]) for h in range(NH_M)]
    hn = jnp.concatenate(parts, axis=1) * gh_ref[...]
    return (hn * _sigmoid(o_ref[...].astype(F32))).astype(BF16)


def _ffn_kernel(*refs, j, final, mixer):
    x_ref, mod_ref, g_ref, wgu32_ref, wd32_ref, fg_ref = refs[:6]
    o_ref, wgu_ref, wd_ref = refs[-3:]
    step = pl.program_id(0)

    @pl.when(step < FFN_W_STEPS)
    def _():
        c0 = pl.multiple_of(step * FFN_WGU_COLS, FFN_WGU_COLS)
        r0 = pl.multiple_of(step * FFN_WD_ROWS, FFN_WD_ROWS)
        wgu_ref[:, pl.ds(c0, FFN_WGU_COLS)] = wgu32_ref[0].astype(BF16)
        wd_ref[pl.ds(r0, FFN_WD_ROWS), :] = wd32_ref[0].astype(BF16)

    @pl.when(step >= FFN_W_STEPS)
    def _():
        x = x_ref[...]
        mod = mod_ref[0]
        if mixer == "mlstm":
            hf_ref, hb_ref, og_ref, gh_ref, wo_ref = refs[6:11]
            x = x + mod[5:6] * _dot(_mlstm_gated_heads(hf_ref, hb_ref, og_ref, gh_ref), wo_ref[...])
        elif mixer == "attn":
            a_ref, wo_ref = refs[6:8]
            x = x + mod[5:6] * _dot(a_ref[...], wo_ref[...])
        shift, scale, gate = mod[3 * j:3 * j + 1], mod[3 * j + 1:3 * j + 2], mod[3 * j + 2:3 * j + 3]
        hb = _mod_norm(x, g_ref[...], shift, scale).astype(BF16)
        acc = jnp.zeros(x.shape, F32)
        for c in range(D_FF // FFN_CHUNK):
            lo = c * FFN_CHUNK
            gg = _dot(hb, wgu_ref[:, lo:lo + FFN_CHUNK])
            uu = _dot(hb, wgu_ref[:, D_FF + lo:D_FF + lo + FFN_CHUNK])
            act = (gg * _sigmoid(gg) * uu).astype(BF16)
            acc = acc + _dot(act, wd_ref[lo:lo + FFN_CHUNK, :])
        y = x + (FFN_RES * gate) * acc
        if final:
            y = _rms(y) * fg_ref[...]
        o_ref[...] = y


def _ffn_call(x, mod, g, wgu, wd, layer, j, seq_len, tm, final_g=None, mlstm=None, attn=None):
    t = x.shape[0]
    final = final_g is not None
    fg = final_g if final else g
    tile = lambda i: jnp.maximum(i - FFN_W_STEPS, 0)
    wstep = lambda i: jnp.minimum(i, FFN_W_STEPS - 1)
    tok = lambda width: pl.BlockSpec((tm, width), lambda i: (tile(i), 0))
    in_specs = [tok(D_MODEL), _mod_spec(mod, seq_len // tm, tile), _row_spec(D_MODEL),
                pl.BlockSpec((1, D_MODEL, FFN_WGU_COLS), lambda i: (layer, 0, wstep(i))),
                pl.BlockSpec((1, FFN_WD_ROWS, D_MODEL), lambda i: (layer, wstep(i), 0)), _row_spec(D_MODEL)]
    args = [x, mod, g, wgu, wd, fg]
    resident = 2 * (D_MODEL * 2 * D_FF + D_FF * D_MODEL)
    per_step = 2 * tm * D_MODEL * 4 + 4 * (D_MODEL * FFN_WGU_COLS + FFN_WD_ROWS * D_MODEL)
    values = tm * D_MODEL * (2 + 4 + 4) + 3 * tm * FFN_CHUNK * 4
    mixer = None
    if mlstm is not None:
        mixer = "mlstm"
        in_specs += [tok(V_W), tok(V_W), tok(V_W), _row_spec(V_W), _resident((V_W, D_MODEL))]
        args += list(mlstm)
        resident += 2 * V_W * D_MODEL
        per_step += 3 * tm * V_W * 2
        values += 3 * tm * V_W * 4
    elif attn is not None:
        mixer = "attn"
        in_specs += [tok(Q_W), _resident((Q_W, D_MODEL))]
        args += list(attn)
        resident += 2 * Q_W * D_MODEL
        per_step += tm * Q_W * 2
        values += tm * Q_W * 4
    return pl.pallas_call(
        functools.partial(_ffn_kernel, j=j, final=final, mixer=mixer),
        grid=(FFN_W_STEPS + t // tm,),
        in_specs=in_specs,
        out_specs=tok(D_MODEL),
        out_shape=jax.ShapeDtypeStruct((t, D_MODEL), F32),
        scratch_shapes=[pltpu.VMEM((D_MODEL, 2 * D_FF), BF16), pltpu.VMEM((D_FF, D_MODEL), BF16)],
        compiler_params=_cparams(("arbitrary",), resident, per_step, values),
        name=f"ffn{j // 2 + 1}" + (f"_{mixer}" if mixer else ""),
    )(*args)


def _scan_rows(x, op, fill, reverse):
    n = x.shape[0]
    idx = lax.broadcasted_iota(jnp.int32, x.shape, 0)
    s = 1
    while s < n:
        if reverse:
            x = op(x, jnp.where(idx < n - s, pltpu.roll(x, n - s, axis=0), fill))
        else:
            x = op(x, jnp.where(idx >= s, pltpu.roll(x, s, axis=0), fill))
        s *= 2
    return x


def _mlstm_in_kernel(x_ref, mod_ref, g_ref, wq_ref, wkt_ref, wvo_ref, wg_ref, bg_ref,
                     q_ref, kt_ref, v_ref, o_ref, scan_ref):
    mod = mod_ref[0]
    hb = _mod_norm(x_ref[...], g_ref[...], mod[3:4], mod[4:5]).astype(BF16)
    gt = _dot(hb, wg_ref[...]) + bg_ref[...]
    q_ref[...] = (_dot(hb, wq_ref[...]) * DK_M ** -0.5).astype(BF16)
    kt_ref[...] = _dot_nt(wkt_ref[...], hb).astype(BF16)
    vo = _dot(hb, wvo_ref[...])
    v_ref[...] = vo[:, :V_W].astype(BF16)
    o_ref[...] = vo[:, V_W:].astype(BF16)
    n = MLSTM_L
    fwd = lax.broadcasted_iota(jnp.int32, (n, LANES), 1) < GATE_DIR_LANES
    for c in range(gt.shape[0] // n):
        rows = slice(c * n, (c + 1) * n)
        li = gt[rows, :LANES]
        fx = gt[rows, LANES:]
        lf = jnp.minimum(fx, 0.0) - jnp.log(1.0 + jnp.exp(-jnp.abs(fx)))
        pre = _scan_rows(lf, jnp.add, 0.0, False)
        b = jnp.where(fwd, pre, pre[n - 1:n, :] - pre + lf)
        a = li - b
        scan_ref[rows, 0:LANES] = a
        scan_ref[rows, LANES:2 * LANES] = b
        scan_ref[rows, 2 * LANES:] = jnp.where(fwd, _scan_rows(a, jnp.maximum, -jnp.inf, False),
                                               _scan_rows(a, jnp.maximum, -jnp.inf, True))


def _mlstm_in_call(x, mod, g, wq, wkt, wvo, wg, bg, seq_len, tm):
    t = x.shape[0]
    out_cols = 2 * QK_W + 2 * V_W
    resident = 2 * D_MODEL * (out_cols + 2 * LANES)
    per_step = tm * (D_MODEL * 4 + out_cols * 2 + 3 * LANES * 4)
    values = tm * (D_MODEL * 2 + out_cols * 4 + 5 * LANES * 4)
    return pl.pallas_call(
        _mlstm_in_kernel,
        grid=(t // tm,),
        in_specs=[
            pl.BlockSpec((tm, D_MODEL), lambda i: (i, 0)),
            _mod_spec(mod, seq_len // tm),
            _row_spec(D_MODEL),
            _resident((D_MODEL, QK_W)),
            _resident((QK_W, D_MODEL)),
            _resident((D_MODEL, 2 * V_W)),
            _resident((D_MODEL, 2 * LANES)),
            _row_spec(2 * LANES),
        ],
        out_specs=[
            pl.BlockSpec((tm, QK_W), lambda i: (i, 0)),
            pl.BlockSpec((QK_W, tm), lambda i: (0, i)),
            pl.BlockSpec((tm, V_W), lambda i: (i, 0)),
            pl.BlockSpec((tm, V_W), lambda i: (i, 0)),
            pl.BlockSpec((tm, 3 * LANES), lambda i: (i, 0)),
        ],
        out_shape=[
            jax.ShapeDtypeStruct((t, QK_W), BF16),
            jax.ShapeDtypeStruct((QK_W, t), BF16),
            jax.ShapeDtypeStruct((t, V_W), BF16),
            jax.ShapeDtypeStruct((t, V_W), BF16),
            jax.ShapeDtypeStruct((t, 3 * LANES), F32),
        ],
        compiler_params=_cparams(("parallel",), resident, per_step, values),
        name="mlstm_in",
    )(x, mod, g, wq, wkt, wvo, wg, bg)


def _bf16_part(x):
    return x.astype(BF16).astype(F32)


def _mlstm_gates(d, scan_ref, rows, m):
    n = MLSTM_L
    a, b, cm = scan_ref[rows, 0:LANES], scan_ref[rows, LANES:2 * LANES], scan_ref[rows, 2 * LANES:]
    rev = d == 1
    mt = jnp.maximum(cm, m) * LOG2_E
    u_hi = _bf16_part(mt)
    u_lo = _bf16_part(mt - u_hi)
    z = -(b * LOG2_E + (u_hi + u_lo))
    z_hi = _bf16_part(z)
    z_mid = _bf16_part(z - z_hi)
    z_lo = _bf16_part(z - z_hi - z_mid)
    lane = lax.broadcasted_iota(jnp.int32, mt.shape, 1)
    k = (lane - d * GATE_DIR_LANES) // NH_M
    terms = jnp.where(k == 0, u_hi, jnp.where(k == 1, u_lo, jnp.where(k == 2, z_hi, jnp.where(k == 3, z_mid, z_lo))))
    end = 0 if rev else n - 1
    mx = jnp.maximum(m, cm[end:end + 1, :])
    keep = jnp.exp(m - mx)
    m_new = b[end:end + 1, :] + mx
    return terms, (a * LOG2_E).T, jnp.exp(a - mx).T, keep, m_new


def _mlstm_heads(dirs, c_sc, spread, m2):
    n = MLSTM_L
    row = lax.broadcasted_iota(jnp.int32, (n, n), 0)
    col = lax.broadcasted_iota(jnp.int32, (n, n), 1)
    ones = jnp.ones((n, LANES), BF16)
    heads = [(dd, h) for dd in dirs for h in range(NH_M)]
    kpad = jnp.zeros((DK_M, DK_M), BF16)
    lhs = []
    for (d, q_ref, kt_ref, _, _, rows, a_t, wend_t, _), h in heads:
        c = d * GATE_DIR_LANES + h
        base = (d * NH_M + h) * 2 * LANES
        u = spread[:, base:base + LANES]
        qh = q_ref[rows, h * DK_M:(h + 1) * DK_M]
        kth = kt_ref[h * DK_M:(h + 1) * DK_M, rows]
        e = jnp.exp2(a_t[c:c + 1, :] - u)
        mask = (col >= row) if d == 1 else (col <= row)
        w = jnp.where(mask, _dot(qh, kth) * e, 0.0).astype(BF16)
        inter = jnp.exp2(m2[:, c:c + 1] - u)
        qi = (inter[:, :DK_M] * qh.astype(F32)).astype(BF16)
        kts = (kth.astype(F32) * wend_t[c:c + 1, :]).astype(BF16)
        lhs.append(jnp.concatenate([jnp.concatenate([w, qi], axis=1), jnp.concatenate([kts, kpad], axis=1)], axis=0))
    for ((d, _, _, v_ref, h_ref, rows, _, _, keep), h), wq in zip(heads, lhs):
        c = d * GATE_DIR_LANES + h
        base = (d * NH_M + h) * 2 * LANES
        zz = spread[:, base + LANES:base + 2 * LANES]
        vext = jnp.concatenate([v_ref[rows, h * DV_M:(h + 1) * DV_M], ones], axis=1)
        cext = c_sc[d, h]
        r = _dot(wq, jnp.concatenate([vext, cext.astype(BF16)], axis=0))
        den = jnp.maximum(jnp.abs(r[:n, DV_M:]), jnp.exp2(zz))
        h_ref[rows, h * DV_M:(h + 1) * DV_M] = (r[:n, :DV_M] / den).astype(BF16)
        c_sc[d, h] = keep[:, c:c + 1] * cext + r[n:]


def _mlstm_kernel(*refs, zero_init):
    qf_ref, ktf_ref, vf_ref, scanf_ref, qb_ref, ktb_ref, vb_ref, scanb_ref, sel_ref = refs[:9]
    hf_ref, hb_ref, cout_ref, nout_ref, mout_ref, c_sc, m_sc = refs[-7:]
    i = pl.program_id(1)

    @pl.when(i == 0)
    def _():
        if zero_init:
            c_sc[...] = jnp.zeros(c_sc.shape, F32)
            m_sc[...] = jnp.zeros(m_sc.shape, F32)
        else:
            c0_ref, n0_ref, m0_ref = refs[9:12]
            c_sc[:, :, :, :DV_M] = c0_ref[0]
            c_sc[:, :, :, DV_M:] = n0_ref[0]
            m_sc[...] = m0_ref[0]

    nsub = qf_ref.shape[0] // MLSTM_L
    for sub in range(nsub):
        rows_f = slice(sub * MLSTM_L, (sub + 1) * MLSTM_L)
        rows_b = slice((nsub - 1 - sub) * MLSTM_L, (nsub - sub) * MLSTM_L)
        m = m_sc[...]
        terms_f, at_f, wend_f, keep_f, mnew_f = _mlstm_gates(0, scanf_ref, rows_f, m)
        terms_b, at_b, wend_b, keep_b, mnew_b = _mlstm_gates(1, scanb_ref, rows_b, m)
        lane = lax.broadcasted_iota(jnp.int32, terms_f.shape, 1)
        packed = jnp.where(lane < GATE_DIR_LANES, terms_f, jnp.where(lane < 2 * GATE_DIR_LANES, terms_b, 0.0))
        spread = _dot(packed.astype(BF16), sel_ref[...])
        m2 = m * LOG2_E
        _mlstm_heads([(0, qf_ref, ktf_ref, vf_ref, hf_ref, rows_f, at_f, wend_f, keep_f),
                      (1, qb_ref, ktb_ref, vb_ref, hb_ref, rows_b, at_b, wend_b, keep_b)], c_sc, spread, m2)
        lane_m = lax.broadcasted_iota(jnp.int32, m.shape, 1)
        m_sc[...] = jnp.where(lane_m < GATE_DIR_LANES, mnew_f, jnp.where(lane_m < 2 * GATE_DIR_LANES, mnew_b, m))

    @pl.when(i == pl.num_programs(1) - 1)
    def _():
        cout_ref[0] = c_sc[:, :, :, :DV_M]
        nout_ref[0] = c_sc[:, :, :, DV_M:]
        mout_ref[0] = m_sc[...]


def _mlstm_core_call(q, kt, v, scans, sel, init, bsz, seq_len):
    t = q.shape[0]
    rows = min(MLSTM_STEP_CHUNKS * MLSTM_L, seq_len)
    nc = seq_len // rows
    fwd = lambda b, i: (b * nc + i, 0)
    bwd = lambda b, i: (b * nc + nc - 1 - i, 0)
    fwd_t = lambda b, i: (0, b * nc + i)
    bwd_t = lambda b, i: (0, b * nc + nc - 1 - i)

    def specs(row_map, col_map):
        return [
            pl.BlockSpec((rows, QK_W), row_map),
            pl.BlockSpec((QK_W, rows), col_map),
            pl.BlockSpec((rows, V_W), row_map),
            pl.BlockSpec((rows, 3 * LANES), row_map),
        ]

    half_shape = (2, NH_M, DK_M, DV_M)
    half_spec = pl.BlockSpec((1,) + half_shape, lambda b, i: (b, 0, 0, 0, 0))
    m_spec = pl.BlockSpec((1, 1, LANES), lambda b, i: (b, 0, 0))
    state_bytes = 2 * NH_M * DK_M * 2 * DV_M * 4
    sel_shape = (LANES, 2 * NH_M * 2 * LANES)
    in_specs = specs(fwd, fwd_t) + specs(bwd, bwd_t) + [pl.BlockSpec(sel_shape, lambda b, i: (0, 0))]
    args = [q, kt, v, scans, q, kt, v, scans, sel]
    if init is not None:
        in_specs += [half_spec, half_spec, m_spec]
        args += list(init)
    per_step = 2 * rows * ((2 * QK_W + 2 * V_W) * 2 + 3 * LANES * 4) + 2 * sel_shape[0] * sel_shape[1] + 2 * state_bytes
    values = MLSTM_L * sel_shape[1] * 4 + 2 * NH_M * MLSTM_L * 4 * LANES * 4
    return pl.pallas_call(
        functools.partial(_mlstm_kernel, zero_init=init is None),
        grid=(bsz, nc),
        in_specs=in_specs,
        out_specs=[pl.BlockSpec((rows, V_W), fwd), pl.BlockSpec((rows, V_W), bwd), half_spec, half_spec, m_spec],
        out_shape=[
            jax.ShapeDtypeStruct((t, V_W), BF16),
            jax.ShapeDtypeStruct((t, V_W), BF16),
            jax.ShapeDtypeStruct((bsz,) + half_shape, F32),
            jax.ShapeDtypeStruct((bsz,) + half_shape, F32),
            jax.ShapeDtypeStruct((bsz, 1, LANES), F32),
        ],
        scratch_shapes=[pltpu.VMEM((2, NH_M, DK_M, 2 * DV_M), F32), pltpu.VMEM((1, LANES), F32)],
        compiler_params=_cparams(("parallel", "arbitrary"), state_bytes, per_step, values),
        name="mlstm_core",
    )(*args)


def _rope_rows_swap(x):
    q = HD_A // 4
    parts = []
    for base in range(0, x.shape[0], 2 * q):
        parts += [x[base + q:base + 2 * q], x[base:base + q]]
    return jnp.concatenate(parts, axis=0)


def _attn_in_kernel(*refs, rope, emit_f32):
    x_ref, mod_ref, g_ref, wqt_ref, wk_ref, wvt_ref = refs[:6]
    pos = 6
    if rope:
        cos_ref, sina_ref, sinb_ref, cost_ref, sint_ref = refs[pos:pos + 5]
        pos += 5
    if emit_f32:
        wv_ref = refs[pos]
        pos += 1
    qt_ref, k_ref, vt_ref = refs[pos:pos + 3]
    pos += 3
    mod = mod_ref[0]
    hb = _mod_norm(x_ref[...], g_ref[...], mod[3:4], mod[4:5]).astype(BF16)
    qt = _dot_nt(wqt_ref[...], hb)
    k = _dot(hb, wk_ref[...])
    blk = ATTN_BLOCK
    vt = _dot_nt(wvt_ref[...], hb).astype(BF16)
    for tb in range(vt.shape[1] // blk):
        vt_ref[tb] = vt[:, tb * blk:(tb + 1) * blk]
    if emit_f32:
        kf_ref, vf_ref = refs[pos:pos + 2]
        kf_ref[...] = k
        vf_ref[...] = _dot(hb, wv_ref[...])
    for gi in range(Q_W // LANES):
        qg = qt[gi * LANES:(gi + 1) * LANES, :]
        if rope:
            cos_t = jnp.concatenate([cost_ref[...]] * (LANES // HD_A), axis=0)
            sin_t = jnp.concatenate([sint_ref[...]] * (LANES // HD_A), axis=0)
            qg = qg * cos_t + _rope_rows_swap(qg) * sin_t
        qg = (qg * (HD_A ** -0.5 * LOG2_E)).astype(BF16)
        p, g = gi // G_A, gi % G_A
        for tb in range(qt.shape[1] // blk):
            qt_ref[tb, p, :, g * blk:(g + 1) * blk] = qg[:, tb * blk:(tb + 1) * blk]
    for gi in range(KV_W // LANES):
        kg = k[:, gi * LANES:(gi + 1) * LANES]
        if rope:
            kg = (kg * cos_ref[...] + pltpu.roll(kg, LANES - HD_A // 4, axis=1) * sina_ref[...]
                  + pltpu.roll(kg, HD_A // 4, axis=1) * sinb_ref[...])
        k_ref[:, gi * LANES:(gi + 1) * LANES] = kg.astype(BF16)


def _attn_in_call(x, mod, g, wqt, wk, wvt, seq_len, tm, rope_tabs=None, wv=None):
    t = x.shape[0]
    rope = rope_tabs is not None
    emit_f32 = wv is not None
    tiles = seq_len // tm
    tok = lambda width: pl.BlockSpec((tm, width), lambda i: (i, 0))
    in_specs = [tok(D_MODEL), _mod_spec(mod, tiles), _row_spec(D_MODEL),
                _resident((Q_W, D_MODEL)), _resident((D_MODEL, KV_W)), _resident((KV_W, D_MODEL))]
    args = [x, mod, g, wqt, wk, wvt]
    if rope:
        pos_row = lambda i: (i % tiles, 0)
        pos_col = lambda i: (0, i % tiles)
        in_specs += [pl.BlockSpec((tm, LANES), pos_row)] * 3 + [pl.BlockSpec((HD_A, tm), pos_col)] * 2
        args += list(rope_tabs)
    if emit_f32:
        in_specs.append(_resident((D_MODEL, KV_W)))
        args.append(wv)
    nblk = tm // ATTN_BLOCK
    qt_slab = (NKV_A // 2, LANES, G_A * ATTN_BLOCK)
    vt_slab = (KV_W, ATTN_BLOCK)
    out_specs = [pl.BlockSpec((nblk,) + qt_slab, lambda i: (i, 0, 0, 0)), tok(KV_W),
                 pl.BlockSpec((nblk,) + vt_slab, lambda i: (i, 0, 0))]
    out_shape = [jax.ShapeDtypeStruct((t // ATTN_BLOCK,) + qt_slab, BF16), jax.ShapeDtypeStruct((t, KV_W), BF16),
                 jax.ShapeDtypeStruct((t // ATTN_BLOCK,) + vt_slab, BF16)]
    if emit_f32:
        out_specs += [tok(KV_W), tok(KV_W)]
        out_shape += [jax.ShapeDtypeStruct((t, KV_W), F32)] * 2
    resident = 2 * D_MODEL * (Q_W + 3 * KV_W)
    per_step = tm * (D_MODEL * 4 + (Q_W + 2 * KV_W) * 2 + 2 * KV_W * 4 + (3 * LANES + 2 * HD_A) * 4)
    values = tm * (D_MODEL * 2 + (Q_W + 3 * KV_W) * 4 + Q_W * 4)
    return pl.pallas_call(
        functools.partial(_attn_in_kernel, rope=rope, emit_f32=emit_f32),
        grid=(t // tm,),
        in_specs=in_specs,
        out_specs=out_specs,
        out_shape=out_shape,
        compiler_params=_cparams(("parallel",), resident, per_step, values),
        name="attn_in",
    )(*args)


def _block_attention(sink_ref, qt_ref, o_ref, sub, pair_keys, mask_of):
    blk = ATTN_BLOCK
    cols_all = G_A * blk
    col_g = lax.broadcasted_iota(jnp.int32, (1, cols_all), 1) // blk
    staged = []
    for p, keys in enumerate(pair_keys):
        qp = qt_ref[sub, p]
        row = lax.broadcasted_iota(jnp.int32, qp.shape, 0)
        for e in range(2):
            kv = 2 * p + e
            qm = jnp.where((row >= e * HD_A) & (row < (e + 1) * HD_A), qp, jnp.zeros_like(qp))
            sink = jnp.full((1, cols_all), sink_ref[0, kv * G_A], F32)
            for g in range(1, G_A):
                sink = jnp.where(col_g == g, sink_ref[0, kv * G_A + g], sink)
            sink = sink * LOG2_E
            scores = []
            m = sink
            for i, (k, _) in enumerate(keys):
                s = _dot(k, qm)
                msk = mask_of(i)
                s = s if msk is None else jnp.where(msk, s, NEG_INF)
                m = jnp.maximum(m, jnp.max(s, axis=0, keepdims=True))
                scores.append(s)
            staged.append((scores, m, sink))
    outs = []
    for idx, (scores, m, sink) in enumerate(staged):
        acc = None
        for s, (_, vt) in zip(scores, pair_keys[idx // 2]):
            vx = jnp.concatenate([vt, jnp.ones((ONES_ROWS, vt.shape[1]), BF16)], axis=0)
            pv = _dot(vx, jnp.exp2(s - m).astype(BF16))
            acc = pv if acc is None else acc + pv
        outs.append(acc[:LANES] / (acc[LANES:LANES + 1] + jnp.exp2(sink - m)))
    row_o = lax.broadcasted_iota(jnp.int32, outs[0].shape, 0)
    for p in range(len(pair_keys)):
        a_t = jnp.where(row_o < HD_A, outs[2 * p], outs[2 * p + 1])
        for g in range(G_A):
            c0 = (p * G_A + g) * LANES
            o_ref[sub * blk:(sub + 1) * blk, c0:c0 + LANES] = a_t[:, g * blk:(g + 1) * blk].T.astype(BF16)


def _attn_latent_kernel(sink_ref, qt_ref, kp_ref, kc_ref, kn_ref, vtp_ref, vtc_ref, vtn_ref, kx_ref, vtx_ref,
                        o_ref, *, n_tok):
    blk = ATTN_BLOCK
    nsub = ATTN_STEP_BLOCKS
    shape = (3 * blk, G_A * blk)
    krow = lax.broadcasted_iota(jnp.int32, shape, 0)
    band = jnp.abs(blk + lax.broadcasted_iota(jnp.int32, shape, 1) % blk - krow) <= WINDOW
    for sub in range(nsub):
        j = pl.program_id(1) * nsub + sub
        local = band
        if sub == 0:
            local = local & ((krow >= blk) | (j > 0))
        if sub == nsub - 1:
            local = local & ((krow < 2 * blk) | (j < n_tok // blk - 1))
        pair_keys = []
        for p in range(NKV_A // 2):
            pl_ = slice(p * LANES, (p + 1) * LANES)
            k_blocks = ([kp_ref[:, pl_]] + [kc_ref[c * blk:(c + 1) * blk, pl_] for c in range(nsub)]
                        + [kn_ref[:, pl_]])[sub:sub + 3]
            vt_blocks = ([vtp_ref[0, pl_, :]] + [vtc_ref[c, pl_, :] for c in range(nsub)]
                         + [vtn_ref[0, pl_, :]])[sub:sub + 3]
            pair_keys.append([(jnp.concatenate(k_blocks, axis=0), jnp.concatenate(vt_blocks, axis=1)),
                              (kx_ref[0, :, pl_], vtx_ref[0, pl_, :])])
        _block_attention(sink_ref, qt_ref, o_ref, sub, pair_keys, lambda i, local=local: local if i == 0 else None)


def _attn_latent_call(qt, k, vt, kx, vtx, sink, bsz, seq_len):
    t = k.shape[0]
    nsub = ATTN_STEP_BLOCKS
    nb = seq_len // ATTN_BLOCK
    ns = nb // nsub
    n_ctx = kx.shape[1]
    prev = lambda b, j: b * nb + jnp.maximum(j * nsub - 1, 0)
    nxt = lambda b, j: b * nb + jnp.minimum(j * nsub + nsub, nb - 1)
    cur = lambda b, j: b * ns + j
    return pl.pallas_call(
        functools.partial(_attn_latent_kernel, n_tok=seq_len),
        grid=(bsz, ns),
        in_specs=[
            pl.BlockSpec(memory_space=pltpu.SMEM),
            pl.BlockSpec((nsub, NKV_A // 2, LANES, G_A * ATTN_BLOCK), lambda b, j: (cur(b, j), 0, 0, 0)),
            pl.BlockSpec((ATTN_BLOCK, KV_W), lambda b, j: (prev(b, j), 0)),
            pl.BlockSpec((nsub * ATTN_BLOCK, KV_W), lambda b, j: (cur(b, j), 0)),
            pl.BlockSpec((ATTN_BLOCK, KV_W), lambda b, j: (nxt(b, j), 0)),
            pl.BlockSpec((1, KV_W, ATTN_BLOCK), lambda b, j: (prev(b, j), 0, 0)),
            pl.BlockSpec((nsub, KV_W, ATTN_BLOCK), lambda b, j: (cur(b, j), 0, 0)),
            pl.BlockSpec((1, KV_W, ATTN_BLOCK), lambda b, j: (nxt(b, j), 0, 0)),
            pl.BlockSpec((1, n_ctx, KV_W), lambda b, j: (b, 0, 0)),
            pl.BlockSpec((1, KV_W, n_ctx), lambda b, j: (b, 0, 0)),
        ],
        out_specs=pl.BlockSpec((nsub * ATTN_BLOCK, Q_W), lambda b, j: (cur(b, j), 0)),
        out_shape=jax.ShapeDtypeStruct((t, Q_W), BF16),
        compiler_params=_cparams(
            ("parallel", "parallel"),
            per_step=(nsub * ATTN_BLOCK * (2 * Q_W + 2 * KV_W) + 2 * (2 * ATTN_BLOCK + n_ctx) * KV_W) * 2,
            values=nsub * NKV_A * (3 * ATTN_BLOCK + n_ctx) * G_A * ATTN_BLOCK * 6),
        name="attn_latent",
    )(sink, qt, k, k, k, vt, vt, vt, kx, vtx)


def _attn_context_kernel(sink_ref, qt_ref, k_ref, vt_ref, o_ref):
    pair_keys = []
    for p in range(NKV_A // 2):
        pl_ = slice(p * LANES, (p + 1) * LANES)
        vt = jnp.concatenate([vt_ref[c, pl_, :] for c in range(vt_ref.shape[0])], axis=1)
        pair_keys.append([(k_ref[:, pl_], vt)])
    for sub in range(qt_ref.shape[0]):
        _block_attention(sink_ref, qt_ref, o_ref, sub, pair_keys, lambda i: None)


def _attn_context_call(qt, k, vt, sink, bsz, seq_len):
    t = k.shape[0]
    nb = seq_len // ATTN_BLOCK
    return pl.pallas_call(
        _attn_context_kernel,
        grid=(bsz,),
        in_specs=[
            pl.BlockSpec(memory_space=pltpu.SMEM),
            pl.BlockSpec((nb, NKV_A // 2, LANES, G_A * ATTN_BLOCK), lambda b: (b, 0, 0, 0)),
            pl.BlockSpec((seq_len, KV_W), lambda b: (b, 0)),
            pl.BlockSpec((nb, KV_W, ATTN_BLOCK), lambda b: (b, 0, 0)),
        ],
        out_specs=pl.BlockSpec((seq_len, Q_W), lambda b: (b, 0)),
        out_shape=jax.ShapeDtypeStruct((t, Q_W), BF16),
        compiler_params=_cparams(("parallel",), per_step=seq_len * (2 * Q_W + 2 * KV_W) * 2,
                                 values=nb * NKV_A * seq_len * G_A * ATTN_BLOCK * 6),
        name="attn_context",
    )(sink, qt, k, vt)


def _rope_tables(n_tok):
    quarter = HD_A // 4
    freqs = ROPE_THETA ** (-jnp.arange(quarter, dtype=F32) / quarter)
    pos = jnp.arange(n_tok)
    row = (pos // GRID_W).astype(F32)
    col = (pos % GRID_W).astype(F32)
    ang_r, ang_c = row[:, None] * freqs, col[:, None] * freqs
    ang = jnp.concatenate([ang_r, ang_r, ang_c, ang_c], axis=-1)
    cos, sin = jnp.cos(ang), jnp.sin(ang)
    first = (jnp.arange(HD_A) % (2 * quarter)) < quarter
    sin_a = jnp.where(first, -sin, 0.0)
    sin_b = jnp.where(first, 0.0, sin)
    sin_t = jnp.where(first, -sin, sin)
    lane_tile = lambda a: jnp.tile(a, (1, LANES // HD_A))
    return lane_tile(cos), lane_tile(sin_a), lane_tile(sin_b), cos.T, sin_t.T


def _gate_lanes(a):
    lead = a.shape[:-2]
    rep = jnp.broadcast_to(a[..., :, None, :], lead + (2, GATE_COPIES, NH_M)).reshape(lead + (2 * GATE_DIR_LANES,))
    return jnp.concatenate([rep, jnp.zeros(lead + (LANES - 2 * GATE_DIR_LANES,), a.dtype)], axis=-1)


def _gate_layout(a):
    a4 = a.reshape(a.shape[:-1] + (2, 2, NH_M))
    return jnp.concatenate([_gate_lanes(a4[..., :, 0, :]), _gate_lanes(a4[..., :, 1, :])], axis=-1)


def _gate_selector():
    r = jnp.arange(LANES)[:, None]
    c = jnp.arange(2 * NH_M * 2 * LANES)[None, :]
    hd, j = c // (2 * LANES), c % (2 * LANES)
    rd, rk, rh = r // GATE_DIR_LANES, (r % GATE_DIR_LANES) // NH_M, r % NH_M
    hit = (r < 2 * GATE_DIR_LANES) & (rd == hd // NH_M) & (rh == hd % NH_M) & ((j < LANES) == (rk < 2))
    return hit.astype(BF16)


def _pair_layout_cols(w):
    d_in = w.shape[0]
    w5 = w.reshape(d_in, NKV_A // 2, 2, G_A, HD_A)
    return jnp.transpose(w5, (0, 1, 3, 2, 4)).reshape(d_in, Q_W)


def _mlstm_init(state_c, state_n, state_m):
    n_rep = jnp.broadcast_to(state_n[..., None].astype(F32), state_n.shape + (DV_M,))
    return state_c.astype(F32), n_rep, _gate_lanes(state_m.astype(F32))[:, None, :]


def kernel(x_prompt, x_sample, state_c, state_n, state_m, cache_k, cache_v, c, c_ctx, w_mod, b_mod, norm_g,
           ffn1_w_gu, ffn1_w_down, ffn2_w_gu, ffn2_w_down, mlstm_w_in, mlstm_b_gate, mlstm_g_head, mlstm_w_out,
           attn_w_in, attn_sink, attn_w_out, final_g):
    bp, n_p, _ = x_prompt.shape
    bs, n_s, _ = x_sample.shape
    xp = x_prompt.reshape(bp * n_p, D_MODEL)
    xs = x_sample.reshape(bs * n_s, D_MODEL)
    tm_p, tm_s = 512, 512
    tm_ffn1 = tm_in = 1024

    cond = jnp.concatenate([c_ctx[None, :], c], axis=0)
    cond = jnp.pad(cond, ((0, COND_ROWS - cond.shape[0]), (0, 0)))
    mod_all = _mod_call(cond, w_mod, b_mod).reshape(DEPTH, COND_ROWS, N_MOD, D_MODEL)

    outs = {}
    for l in range(DEPTH):
        mod_p = mod_all[l, 0:1]
        mod_s = mod_all[l, 1:1 + bs]
        g = norm_g[l]
        xp = _ffn_call(xp, mod_p, g[0:1], ffn1_w_gu, ffn1_w_down, l, 0, n_p, tm_ffn1)
        xs = _ffn_call(xs, mod_s, g[0:1], ffn1_w_gu, ffn1_w_down, l, 0, n_s, tm_ffn1)
        i = l // 2
        if l % 2 == 0:
            w_in = mlstm_w_in[i]
            wq = w_in[:, :QK_W].astype(BF16)
            wkt = w_in[:, QK_W:2 * QK_W].T.astype(BF16)
            wvo = w_in[:, 2 * QK_W:2 * QK_W + 2 * V_W].astype(BF16)
            wg = _gate_layout(w_in[:, 2 * QK_W + 2 * V_W:]).astype(BF16)
            bg = _gate_layout(mlstm_b_gate[i].astype(F32))[None, :]
            gh = mlstm_g_head[i].astype(F32)[None, :]
            w_out = mlstm_w_out[i].astype(BF16)
            sel = _gate_selector()
            streams = [("p", xp, mod_p, n_p, tm_in, bp, None),
                       ("s", xs, mod_s, n_s, tm_in, bs, _mlstm_init(state_c[:, i], state_n[:, i], state_m[:, i]))]
            res = {}
            for tag, x, mod, n_tok, tm, bsz, init in streams:
                q, kt, v, o, scans = _mlstm_in_call(x, mod, g[1:2], wq, wkt, wvo, wg, bg, n_tok, tm)
                hf, hb, c_fin, n_fin, m_fin = _mlstm_core_call(q, kt, v, scans, sel, init, bsz, n_tok)
                res[tag] = ((hf, hb, o, gh, w_out), c_fin, n_fin, m_fin)
            mix_p, c_fin, n_fin, m_fin = res["p"]
            mix_s = res["s"][0]
            mix_key = "mlstm"
            dt = x_prompt.dtype
            m_heads = m_fin[:, 0, :2 * GATE_DIR_LANES].reshape(bp, 2, GATE_COPIES, NH_M)[:, :, 0, :]
            outs.setdefault("c", []).append(c_fin.astype(dt))
            outs.setdefault("n", []).append(n_fin[..., 0].astype(dt))
            outs.setdefault("m", []).append(m_heads.astype(dt))
        else:
            w_in = attn_w_in[i]
            wqt = _pair_layout_cols(w_in[:, :Q_W]).T.astype(BF16)
            wk = w_in[:, Q_W:Q_W + KV_W].astype(BF16)
            wv = w_in[:, Q_W + KV_W:].astype(BF16)
            wvt = w_in[:, Q_W + KV_W:].T.astype(BF16)
            w_out = _pair_layout_cols(attn_w_out[i].T).T.astype(BF16)
            sink = attn_sink[i].astype(F32)[None, :]
            qt, k, vt, kf, vf = _attn_in_call(xp, mod_p, g[1:2], wqt, wk, wvt, n_p, tm_in, wv=wv)
            mix_p = (_attn_context_call(qt, k, vt, sink, bp, n_p), w_out)
            mix_key = "attn"
            outs.setdefault("k", []).append(kf.reshape(bp, n_p, NKV_A, HD_A))
            outs.setdefault("v", []).append(vf.reshape(bp, n_p, NKV_A, HD_A))
            qt, k, vt = _attn_in_call(xs, mod_s, g[1:2], wqt, wk, wvt, n_s, tm_in, rope_tabs=_rope_tables(n_s))
            n_ctx = cache_k.shape[2]
            kx = cache_k[:, i].reshape(bs, n_ctx, KV_W).astype(BF16)
            vtx = jnp.swapaxes(cache_v[:, i].reshape(bs, n_ctx, KV_W), 1, 2).astype(BF16)
            mix_s = (_attn_latent_call(qt, k, vt, kx, vtx, sink, bs, n_s), w_out)
        fg = final_g[None, :] if l == DEPTH - 1 else None
        xp = _ffn_call(xp, mod_p, g[2:3], ffn2_w_gu, ffn2_w_down, l, 2, n_p, tm_p, final_g=fg, **{mix_key: mix_p})
        xs = _ffn_call(xs, mod_s, g[2:3], ffn2_w_gu, ffn2_w_down, l, 2, n_s, tm_s, final_g=fg, **{mix_key: mix_s})

    return (xp.reshape(bp, n_p, D_MODEL), xs.reshape(bs, n_s, D_MODEL),
            jnp.stack(outs["c"], axis=1), jnp.stack(outs["n"], axis=1), jnp.stack(outs["m"], axis=1),
            jnp.stack(outs["k"], axis=1), jnp.stack(outs["v"], axis=1))
```

```python
import functools

import jax
import jax.numpy as jnp
from jax import lax
from jax.experimental import pallas as pl
from jax.experimental.pallas import tpu as pltpu

D_MODEL = 1024
DEPTH = 2
GRID_W = 64
D_FF = 2816
FFN_RES = 0.5
NH_M = 8
DK_M = 64
DV_M = 128
NH_A = 16
NKV_A = 4
G_A = NH_A // NKV_A
HD_A = 64
WINDOW = 128
ATTN_BLOCK = 128
ROPE_THETA = 10000.0
EPS = 1e-6
NEG_INF = -1e30
N_MOD = 9

QK_W = NH_M * DK_M
V_W = NH_M * DV_M
KV_W = NKV_A * HD_A
Q_W = NH_A * HD_A

LANES = 128
MLSTM_L = 128
MLSTM_STEP_CHUNKS = 8
FFN_CHUNK = 256
FFN_W_STEPS = D_FF // FFN_CHUNK
FFN_WGU_COLS = 2 * D_FF // FFN_W_STEPS
FFN_WD_ROWS = D_FF // FFN_W_STEPS
ATTN_STEP_BLOCKS = 8
ONES_ROWS = 16
LOG2_E = 1.4426950408889634
GATE_COPIES = 5
GATE_DIR_LANES = GATE_COPIES * NH_M
COND_ROWS = 16
VMEM_CAP = 60000 * 1024
COMPILER_TEMP_BYTES = 8 << 20

F32 = jnp.float32
BF16 = jnp.bfloat16


def _cparams(sem, resident=0, per_step=0, values=0):
    need = resident + 2 * per_step + values + COMPILER_TEMP_BYTES
    return pltpu.CompilerParams(dimension_semantics=sem, vmem_limit_bytes=min(int(need), VMEM_CAP))


def _dot(a, b):
    return jnp.dot(a, b, preferred_element_type=F32)


def _dot_nt(a, b):
    return lax.dot_general(a, b, (((1,), (1,)), ((), ())), preferred_element_type=F32)


def _rms(x):
    return x * lax.rsqrt(jnp.mean(x * x, axis=-1, keepdims=True) + EPS)


def _sigmoid(x):
    return 1.0 / (1.0 + jnp.exp(-x))


def _mod_norm(x, g, shift, scale):
    return (_rms(x) * g) * (1.0 + scale) + shift


def _mod_kernel(c_ref, w_ref, b_ref, o_ref):
    c = c_ref[...]
    s = (c * _sigmoid(c)).astype(BF16)
    o_ref[0] = _dot(s, w_ref[0].astype(BF16)) + b_ref[0]


def _mod_call(cond, w_mod, b_mod):
    tn = D_MODEL
    n_out = N_MOD * D_MODEL
    return pl.pallas_call(
        _mod_kernel,
        grid=(DEPTH, n_out // tn),
        in_specs=[
            pl.BlockSpec((COND_ROWS, D_MODEL), lambda l, n: (0, 0)),
            pl.BlockSpec((1, D_MODEL, tn), lambda l, n: (l, 0, n)),
            pl.BlockSpec((1, 1, tn), lambda l, n: (l, 0, n)),
        ],
        out_specs=pl.BlockSpec((1, COND_ROWS, tn), lambda l, n: (l, 0, n)),
        out_shape=jax.ShapeDtypeStruct((DEPTH, COND_ROWS, n_out), F32),
        compiler_params=_cparams(("parallel", "parallel"), per_step=D_MODEL * tn * 4, values=D_MODEL * tn * 2),
        name="adaln_mod",
    )(cond, w_mod, b_mod.reshape(DEPTH, 1, n_out))


def _mod_spec(mod, tiles_per_seq, tile=lambda i: i):
    if mod.shape[0] == 1:
        return pl.BlockSpec((1, N_MOD, D_MODEL), lambda i: (0, 0, 0))
    return pl.BlockSpec((1, N_MOD, D_MODEL), lambda i: (tile(i) // tiles_per_seq, 0, 0))


def _row_spec(width):
    return pl.BlockSpec((1, width), lambda i: (0, 0))


def _resident(shape):
    return pl.BlockSpec(shape, lambda i: (0,) * len(shape), pipeline_mode=pl.Buffered(1))


def _mlstm_gated_heads(hf_ref, hb_ref, o_ref, gh_ref):
    hs = hf_ref[...].astype(F32) + hb_ref[...].astype(F32)
    parts = [_rms(hs[:, h * DV_M:(h + 1) * DV_M]) for h in range(NH_M)]
    hn = jnp.concatenate(parts, axis=1) * gh_ref[...]
    return (hn * _sigmoid(o_ref[...].astype(F32))).astype(BF16)


def _ffn_kernel(*refs, j, final, mixer):
    x_ref, mod_ref, g_ref, wgu32_ref, wd32_ref, fg_ref = refs[:6]
    o_ref, wgu_ref, wd_ref = refs[-3:]
    step = pl.program_id(0)

    @pl.when(step < FFN_W_STEPS)
    def _():
        c0 = pl.multiple_of(step * FFN_WGU_COLS, FFN_WGU_COLS)
        r0 = pl.multiple_of(step * FFN_WD_ROWS, FFN_WD_ROWS)
        wgu_ref[:, pl.ds(c0, FFN_WGU_COLS)] = wgu32_ref[0].astype(BF16)
        wd_ref[pl.ds(r0, FFN_WD_ROWS), :] = wd32_ref[0].astype(BF16)

    @pl.when(step >= FFN_W_STEPS)
    def _():
        x = x_ref[...]
        mod = mod_ref[0]
        if mixer == "mlstm":
            hf_ref, hb_ref, og_ref, gh_ref, wo_ref = refs[6:11]
            x = x + mod[5:6] * _dot(_mlstm_gated_heads(hf_ref, hb_ref, og_ref, gh_ref), wo_ref[...])
        elif mixer == "attn":
            a_ref, wo_ref = refs[6:8]
            x = x + mod[5:6] * _dot(a_ref[...], wo_ref[...])
        shift, scale, gate = mod[3 * j:3 * j + 1], mod[3 * j + 1:3 * j + 2], mod[3 * j + 2:3 * j + 3]
        hb = _mod_norm(x, g_ref[...], shift, scale).astype(BF16)
        acc = jnp.zeros(x.shape, F32)
        for c in range(D_FF // FFN_CHUNK):
            lo = c * FFN_CHUNK
            gg = _dot(hb, wgu_ref[:, lo:lo + FFN_CHUNK])
            uu = _dot(hb, wgu_ref[:, D_FF + lo:D_FF + lo + FFN_CHUNK])
            act = (gg * _sigmoid(gg) * uu).astype(BF16)
            acc = acc + _dot(act, wd_ref[lo:lo + FFN_CHUNK, :])
        y = x + (FFN_RES * gate) * acc
        if final:
            y = _rms(y) * fg_ref[...]
        o_ref[...] = y


def _ffn_call(x, mod, g, wgu, wd, layer, j, seq_len, tm, final_g=None, mlstm=None, attn=None):
    t = x.shape[0]
    final = final_g is not None
    fg = final_g if final else g
    tile = lambda i: jnp.maximum(i - FFN_W_STEPS, 0)
    wstep = lambda i: jnp.minimum(i, FFN_W_STEPS - 1)
    tok = lambda width: pl.BlockSpec((tm, width), lambda i: (tile(i), 0))
    in_specs = [tok(D_MODEL), _mod_spec(mod, seq_len // tm, tile), _row_spec(D_MODEL),
                pl.BlockSpec((1, D_MODEL, FFN_WGU_COLS), lambda i: (layer, 0, wstep(i))),
                pl.BlockSpec((1, FFN_WD_ROWS, D_MODEL), lambda i: (layer, wstep(i), 0)), _row_spec(D_MODEL)]
    args = [x, mod, g, wgu, wd, fg]
    resident = 2 * (D_MODEL * 2 * D_FF + D_FF * D_MODEL)
    per_step = 2 * tm * D_MODEL * 4 + 4 * (D_MODEL * FFN_WGU_COLS + FFN_WD_ROWS * D_MODEL)
    values = tm * D_MODEL * (2 + 4 + 4) + 3 * tm * FFN_CHUNK * 4
    mixer = None
    if mlstm is not None:
        mixer = "mlstm"
        in_specs += [tok(V_W), tok(V_W), tok(V_W), _row_spec(V_W), _resident((V_W, D_MODEL))]
        args += list(mlstm)
        resident += 2 * V_W * D_MODEL
        per_step += 3 * tm * V_W * 2
        values += 3 * tm * V_W * 4
    elif attn is not None:
        mixer = "attn"
        in_specs += [tok(Q_W), _resident((Q_W, D_MODEL))]
        args += list(attn)
        resident += 2 * Q_W * D_MODEL
        per_step += tm * Q_W * 2
        values += tm * Q_W * 4
    return pl.pallas_call(
        functools.partial(_ffn_kernel, j=j, final=final, mixer=mixer),
        grid=(FFN_W_STEPS + t // tm,),
        in_specs=in_specs,
        out_specs=tok(D_MODEL),
        out_shape=jax.ShapeDtypeStruct((t, D_MODEL), F32),
        scratch_shapes=[pltpu.VMEM((D_MODEL, 2 * D_FF), BF16), pltpu.VMEM((D_FF, D_MODEL), BF16)],
        compiler_params=_cparams(("arbitrary",), resident, per_step, values),
        name=f"ffn{j // 2 + 1}" + (f"_{mixer}" if mixer else ""),
    )(*args)


def _scan_rows(x, op, fill, reverse):
    n = x.shape[0]
    idx = lax.broadcasted_iota(jnp.int32, x.shape, 0)
    s = 1
    while s < n:
        if reverse:
            x = op(x, jnp.where(idx < n - s, pltpu.roll(x, n - s, axis=0), fill))
        else:
            x = op(x, jnp.where(idx >= s, pltpu.roll(x, s, axis=0), fill))
        s *= 2
    return x


def _mlstm_in_kernel(x_ref, mod_ref, g_ref, wq_ref, wkt_ref, wvo_ref, wg_ref, bg_ref,
                     q_ref, kt_ref, v_ref, o_ref, scan_ref):
    mod = mod_ref[0]
    hb = _mod_norm(x_ref[...], g_ref[...], mod[3:4], mod[4:5]).astype(BF16)
    gt = _dot(hb, wg_ref[...]) + bg_ref[...]
    q_ref[...] = (_dot(hb, wq_ref[...]) * DK_M ** -0.5).astype(BF16)
    kt_ref[...] = _dot_nt(wkt_ref[...], hb).astype(BF16)
    vo = _dot(hb, wvo_ref[...])
    v_ref[...] = vo[:, :V_W].astype(BF16)
    o_ref[...] = vo[:, V_W:].astype(BF16)
    n = MLSTM_L
    fwd = lax.broadcasted_iota(jnp.int32, (n, LANES), 1) < GATE_DIR_LANES
    for c in range(gt.shape[0] // n):
        rows = slice(c * n, (c + 1) * n)
        li = gt[rows, :LANES]
        fx = gt[rows, LANES:]
        lf = jnp.minimum(fx, 0.0) - jnp.log(1.0 + jnp.exp(-jnp.abs(fx)))
        pre = _scan_rows(lf, jnp.add, 0.0, False)
        b = jnp.where(fwd, pre, pre[n - 1:n, :] - pre + lf)
        a = li - b
        scan_ref[rows, 0:LANES] = a
        scan_ref[rows, LANES:2 * LANES] = b
        scan_ref[rows, 2 * LANES:] = jnp.where(fwd, _scan_rows(a, jnp.maximum, -jnp.inf, False),
                                               _scan_rows(a, jnp.maximum, -jnp.inf, True))


def _mlstm_in_call(x, mod, g, wq, wkt, wvo, wg, bg, seq_len, tm):
    t = x.shape[0]
    out_cols = 2 * QK_W + 2 * V_W
    resident = 2 * D_MODEL * (out_cols + 2 * LANES)
    per_step = tm * (D_MODEL * 4 + out_cols * 2 + 3 * LANES * 4)
    values = tm * (D_MODEL * 2 + out_cols * 4 + 5 * LANES * 4)
    return pl.pallas_call(
        _mlstm_in_kernel,
        grid=(t // tm,),
        in_specs=[
            pl.BlockSpec((tm, D_MODEL), lambda i: (i, 0)),
            _mod_spec(mod, seq_len // tm),
            _row_spec(D_MODEL),
            _resident((D_MODEL, QK_W)),
            _resident((QK_W, D_MODEL)),
            _resident((D_MODEL, 2 * V_W)),
            _resident((D_MODEL, 2 * LANES)),
            _row_spec(2 * LANES),
        ],
        out_specs=[
            pl.BlockSpec((tm, QK_W), lambda i: (i, 0)),
            pl.BlockSpec((QK_W, tm), lambda i: (0, i)),
            pl.BlockSpec((tm, V_W), lambda i: (i, 0)),
            pl.BlockSpec((tm, V_W), lambda i: (i, 0)),
            pl.BlockSpec((tm, 3 * LANES), lambda i: (i, 0)),
        ],
        out_shape=[
            jax.ShapeDtypeStruct((t, QK_W), BF16),
            jax.ShapeDtypeStruct((QK_W, t), BF16),
            jax.ShapeDtypeStruct((t, V_W), BF16),
            jax.ShapeDtypeStruct((t, V_W), BF16),
            jax.ShapeDtypeStruct((t, 3 * LANES), F32),
        ],
        compiler_params=_cparams(("parallel",), resident, per_step, values),
        name="mlstm_in",
    )(x, mod, g, wq, wkt, wvo, wg, bg)


def _bf16_part(x):
    return x.astype(BF16).astype(F32)


def _mlstm_gates(d, scan_ref, rows, m):
    n = MLSTM_L
    a, b, cm = scan_ref[rows, 0:LANES], scan_ref[rows, LANES:2 * LANES], scan_ref[rows, 2 * LANES:]
    rev = d == 1
    mt = jnp.maximum(cm, m) * LOG2_E
    u_hi = _bf16_part(mt)
    u_lo = _bf16_part(mt - u_hi)
    z = -(b * LOG2_E + (u_hi + u_lo))
    z_hi = _bf16_part(z)
    z_mid = _bf16_part(z - z_hi)
    z_lo = _bf16_part(z - z_hi - z_mid)
    lane = lax.broadcasted_iota(jnp.int32, mt.shape, 1)
    k = (lane - d * GATE_DIR_LANES) // NH_M
    terms = jnp.where(k == 0, u_hi, jnp.where(k == 1, u_lo, jnp.where(k == 2, z_hi, jnp.where(k == 3, z_mid, z_lo))))
    end = 0 if rev else n - 1
    mx = jnp.maximum(m, cm[end:end + 1, :])
    keep = jnp.exp(m - mx)
    m_new = b[end:end + 1, :] + mx
    return terms, (a * LOG2_E).T, jnp.exp(a - mx).T, keep, m_new


def _mlstm_heads(dirs, c_sc, spread, m2):
    n = MLSTM_L
    row = lax.broadcasted_iota(jnp.int32, (n, n), 0)
    col = lax.broadcasted_iota(jnp.int32, (n, n), 1)
    ones = jnp.ones((n, LANES), BF16)
    heads = [(dd, h) for dd in dirs for h in range(NH_M)]
    kpad = jnp.zeros((DK_M, DK_M), BF16)
    lhs = []
    for (d, q_ref, kt_ref, _, _, rows, a_t, wend_t, _), h in heads:
        c = d * GATE_DIR_LANES + h
        base = (d * NH_M + h) * 2 * LANES
        u = spread[:, base:base + LANES]
        qh = q_ref[rows, h * DK_M:(h + 1) * DK_M]
        kth = kt_ref[h * DK_M:(h + 1) * DK_M, rows]
        e = jnp.exp2(a_t[c:c + 1, :] - u)
        mask = (col >= row) if d == 1 else (col <= row)
        w = jnp.where(mask, _dot(qh, kth) * e, 0.0).astype(BF16)
        inter = jnp.exp2(m2[:, c:c + 1] - u)
        qi = (inter[:, :DK_M] * qh.astype(F32)).astype(BF16)
        kts = (kth.astype(F32) * wend_t[c:c + 1, :]).astype(BF16)
        lhs.append(jnp.concatenate([jnp.concatenate([w, qi], axis=1), jnp.concatenate([kts, kpad], axis=1)], axis=0))
    for ((d, _, _, v_ref, h_ref, rows, _, _, keep), h), wq in zip(heads, lhs):
        c = d * GATE_DIR_LANES + h
        base = (d * NH_M + h) * 2 * LANES
        zz = spread[:, base + LANES:base + 2 * LANES]
        vext = jnp.concatenate([v_ref[rows, h * DV_M:(h + 1) * DV_M], ones], axis=1)
        cext = c_sc[d, h]
        r = _dot(wq, jnp.concatenate([vext, cext.astype(BF16)], axis=0))
        den = jnp.maximum(jnp.abs(r[:n, DV_M:]), jnp.exp2(zz))
        h_ref[rows, h * DV_M:(h + 1) * DV_M] = (r[:n, :DV_M] / den).astype(BF16)
        c_sc[d, h] = keep[:, c:c + 1] * cext + r[n:]


def _mlstm_kernel(*refs, zero_init):
    qf_ref, ktf_ref, vf_ref, scanf_ref, qb_ref, ktb_ref, vb_ref, scanb_ref, sel_ref = refs[:9]
    hf_ref, hb_ref, cout_ref, nout_ref, mout_ref, c_sc, m_sc = refs[-7:]
    i = pl.program_id(1)

    @pl.when(i == 0)
    def _():
        if zero_init:
            c_sc[...] = jnp.zeros(c_sc.shape, F32)
            m_sc[...] = jnp.zeros(m_sc.shape, F32)
        else:
            c0_ref, n0_ref, m0_ref = refs[9:12]
            c_sc[:, :, :, :DV_M] = c0_ref[0]
            c_sc[:, :, :, DV_M:] = n0_ref[0]
            m_sc[...] = m0_ref[0]

    nsub = qf_ref.shape[0] // MLSTM_L
    for sub in range(nsub):
        rows_f = slice(sub * MLSTM_L, (sub + 1) * MLSTM_L)
        rows_b = slice((nsub - 1 - sub) * MLSTM_L, (nsub - sub) * MLSTM_L)
        m = m_sc[...]
        terms_f, at_f, wend_f, keep_f, mnew_f = _mlstm_gates(0, scanf_ref, rows_f, m)
        terms_b, at_b, wend_b, keep_b, mnew_b = _mlstm_gates(1, scanb_ref, rows_b, m)
        lane = lax.broadcasted_iota(jnp.int32, terms_f.shape, 1)
        packed = jnp.where(lane < GATE_DIR_LANES, terms_f, jnp.where(lane < 2 * GATE_DIR_LANES, terms_b, 0.0))
        spread = _dot(packed.astype(BF16), sel_ref[...])
        m2 = m * LOG2_E
        _mlstm_heads([(0, qf_ref, ktf_ref, vf_ref, hf_ref, rows_f, at_f, wend_f, keep_f),
                      (1, qb_ref, ktb_ref, vb_ref, hb_ref, rows_b, at_b, wend_b, keep_b)], c_sc, spread, m2)
        lane_m = lax.broadcasted_iota(jnp.int32, m.shape, 1)
        m_sc[...] = jnp.where(lane_m < GATE_DIR_LANES, mnew_f, jnp.where(lane_m < 2 * GATE_DIR_LANES, mnew_b, m))

    @pl.when(i == pl.num_programs(1) - 1)
    def _():
        cout_ref[0] = c_sc[:, :, :, :DV_M]
        nout_ref[0] = c_sc[:, :, :, DV_M:]
        mout_ref[0] = m_sc[...]


def _mlstm_core_call(q, kt, v, scans, sel, init, bsz, seq_len):
    t = q.shape[0]
    rows = min(MLSTM_STEP_CHUNKS * MLSTM_L, seq_len)
    nc = seq_len // rows
    fwd = lambda b, i: (b * nc + i, 0)
    bwd = lambda b, i: (b * nc + nc - 1 - i, 0)
    fwd_t = lambda b, i: (0, b * nc + i)
    bwd_t = lambda b, i: (0, b * nc + nc - 1 - i)

    def specs(row_map, col_map):
        return [
            pl.BlockSpec((rows, QK_W), row_map),
            pl.BlockSpec((QK_W, rows), col_map),
            pl.BlockSpec((rows, V_W), row_map),
            pl.BlockSpec((rows, 3 * LANES), row_map),
        ]

    half_shape = (2, NH_M, DK_M, DV_M)
    half_spec = pl.BlockSpec((1,) + half_shape, lambda b, i: (b, 0, 0, 0, 0))
    m_spec = pl.BlockSpec((1, 1, LANES), lambda b, i: (b, 0, 0))
    state_bytes = 2 * NH_M * DK_M * 2 * DV_M * 4
    sel_shape = (LANES, 2 * NH_M * 2 * LANES)
    in_specs = specs(fwd, fwd_t) + specs(bwd, bwd_t) + [pl.BlockSpec(sel_shape, lambda b, i: (0, 0))]
    args = [q, kt, v, scans, q, kt, v, scans, sel]
    if init is not None:
        in_specs += [half_spec, half_spec, m_spec]
        args += list(init)
    per_step = 2 * rows * ((2 * QK_W + 2 * V_W) * 2 + 3 * LANES * 4) + 2 * sel_shape[0] * sel_shape[1] + 2 * state_bytes
    values = MLSTM_L * sel_shape[1] * 4 + 2 * NH_M * MLSTM_L * 4 * LANES * 4
    return pl.pallas_call(
        functools.partial(_mlstm_kernel, zero_init=init is None),
        grid=(bsz, nc),
        in_specs=in_specs,
        out_specs=[pl.BlockSpec((rows, V_W), fwd), pl.BlockSpec((rows, V_W), bwd), half_spec, half_spec, m_spec],
        out_shape=[
            jax.ShapeDtypeStruct((t, V_W), BF16),
            jax.ShapeDtypeStruct((t, V_W), BF16),
            jax.ShapeDtypeStruct((bsz,) + half_shape, F32),
            jax.ShapeDtypeStruct((bsz,) + half_shape, F32),
            jax.ShapeDtypeStruct((bsz, 1, LANES), F32),
        ],
        scratch_shapes=[pltpu.VMEM((2, NH_M, DK_M, 2 * DV_M), F32), pltpu.VMEM((1, LANES), F32)],
        compiler_params=_cparams(("parallel", "arbitrary"), state_bytes, per_step, values),
        name="mlstm_core",
    )(*args)


def _rope_rows_swap(x):
    q = HD_A // 4
    parts = []
    for base in range(0, x.shape[0], 2 * q):
        parts += [x[base + q:base + 2 * q], x[base:base + q]]
    return jnp.concatenate(parts, axis=0)


def _attn_in_kernel(*refs, rope, emit_f32):
    x_ref, mod_ref, g_ref, wqt_ref, wk_ref, wvt_ref = refs[:6]
    pos = 6
    if rope:
        cos_ref, sina_ref, sinb_ref, cost_ref, sint_ref = refs[pos:pos + 5]
        pos += 5
    if emit_f32:
        wv_ref = refs[pos]
        pos += 1
    qt_ref, k_ref, vt_ref = refs[pos:pos + 3]
    pos += 3
    mod = mod_ref[0]
    hb = _mod_norm(x_ref[...], g_ref[...], mod[3:4], mod[4:5]).astype(BF16)
    qt = _dot_nt(wqt_ref[...], hb)
    k = _dot(hb, wk_ref[...])
    blk = ATTN_BLOCK
    vt = _dot_nt(wvt_ref[...], hb).astype(BF16)
    for tb in range(vt.shape[1] // blk):
        vt_ref[tb] = vt[:, tb * blk:(tb + 1) * blk]
    if emit_f32:
        kf_ref, vf_ref = refs[pos:pos + 2]
        kf_ref[...] = k
        vf_ref[...] = _dot(hb, wv_ref[...])
    for gi in range(Q_W // LANES):
        qg = qt[gi * LANES:(gi + 1) * LANES, :]
        if rope:
            cos_t = jnp.concatenate([cost_ref[...]] * (LANES // HD_A), axis=0)
            sin_t = jnp.concatenate([sint_ref[...]] * (LANES // HD_A), axis=0)
            qg = qg * cos_t + _rope_rows_swap(qg) * sin_t
        qg = (qg * (HD_A ** -0.5 * LOG2_E)).astype(BF16)
        p, g = gi // G_A, gi % G_A
        for tb in range(qt.shape[1] // blk):
            qt_ref[tb, p, :, g * blk:(g + 1) * blk] = qg[:, tb * blk:(tb + 1) * blk]
    for gi in range(KV_W // LANES):
        kg = k[:, gi * LANES:(gi + 1) * LANES]
        if rope:
            kg = (kg * cos_ref[...] + pltpu.roll(kg, LANES - HD_A // 4, axis=1) * sina_ref[...]
                  + pltpu.roll(kg, HD_A // 4, axis=1) * sinb_ref[...])
        k_ref[:, gi * LANES:(gi + 1) * LANES] = kg.astype(BF16)


def _attn_in_call(x, mod, g, wqt, wk, wvt, seq_len, tm, rope_tabs=None, wv=None):
    t = x.shape[0]
    rope = rope_tabs is not None
    emit_f32 = wv is not None
    tiles = seq_len // tm
    tok = lambda width: pl.BlockSpec((tm, width), lambda i: (i, 0))
    in_specs = [tok(D_MODEL), _mod_spec(mod, tiles), _row_spec(D_MODEL),
                _resident((Q_W, D_MODEL)), _resident((D_MODEL, KV_W)), _resident((KV_W, D_MODEL))]
    args = [x, mod, g, wqt, wk, wvt]
    if rope:
        pos_row = lambda i: (i % tiles, 0)
        pos_col = lambda i: (0, i % tiles)
        in_specs += [pl.BlockSpec((tm, LANES), pos_row)] * 3 + [pl.BlockSpec((HD_A, tm), pos_col)] * 2
        args += list(rope_tabs)
    if emit_f32:
        in_specs.append(_resident((D_MODEL, KV_W)))
        args.append(wv)
    nblk = tm // ATTN_BLOCK
    qt_slab = (NKV_A // 2, LANES, G_A * ATTN_BLOCK)
    vt_slab = (KV_W, ATTN_BLOCK)
    out_specs = [pl.BlockSpec((nblk,) + qt_slab, lambda i: (i, 0, 0, 0)), tok(KV_W),
                 pl.BlockSpec((nblk,) + vt_slab, lambda i: (i, 0, 0))]
    out_shape = [jax.ShapeDtypeStruct((t // ATTN_BLOCK,) + qt_slab, BF16), jax.ShapeDtypeStruct((t, KV_W), BF16),
                 jax.ShapeDtypeStruct((t // ATTN_BLOCK,) + vt_slab, BF16)]
    if emit_f32:
        out_specs += [tok(KV_W), tok(KV_W)]
        out_shape += [jax.ShapeDtypeStruct((t, KV_W), F32)] * 2
    resident = 2 * D_MODEL * (Q_W + 3 * KV_W)
    per_step = tm * (D_MODEL * 4 + (Q_W + 2 * KV_W) * 2 + 2 * KV_W * 4 + (3 * LANES + 2 * HD_A) * 4)
    values = tm * (D_MODEL * 2 + (Q_W + 3 * KV_W) * 4 + Q_W * 4)
    return pl.pallas_call(
        functools.partial(_attn_in_kernel, rope=rope, emit_f32=emit_f32),
        grid=(t // tm,),
        in_specs=in_specs,
        out_specs=out_specs,
        out_shape=out_shape,
        compiler_params=_cparams(("parallel",), resident, per_step, values),
        name="attn_in",
    )(*args)


def _block_attention(sink_ref, qt_ref, o_ref, sub, pair_keys, mask_of):
    blk = ATTN_BLOCK
    cols_all = G_A * blk
    col_g = lax.broadcasted_iota(jnp.int32, (1, cols_all), 1) // blk
    staged = []
    for p, keys in enumerate(pair_keys):
        qp = qt_ref[sub, p]
        row = lax.broadcasted_iota(jnp.int32, qp.shape, 0)
        for e in range(2):
            kv = 2 * p + e
            qm = jnp.where((row >= e * HD_A) & (row < (e + 1) * HD_A), qp, jnp.zeros_like(qp))
            sink = jnp.full((1, cols_all), sink_ref[0, kv * G_A], F32)
            for g in range(1, G_A):
                sink = jnp.where(col_g == g, sink_ref[0, kv * G_A + g], sink)
            sink = sink * LOG2_E
            scores = []
            m = sink
            for i, (k, _) in enumerate(keys):
                s = _dot(k, qm)
                blocks = mask_of(i)
                if blocks is not None:
                    s = jnp.concatenate([s[r * blk:(r + 1) * blk] if mk is None
                                         else jnp.where(mk, s[r * blk:(r + 1) * blk], NEG_INF)
                                         for r, mk in enumerate(blocks)], axis=0)
                m = jnp.maximum(m, jnp.max(s, axis=0, keepdims=True))
                scores.append(s)
            staged.append((scores, m, sink))
    outs = []
    for idx, (scores, m, sink) in enumerate(staged):
        acc = None
        for s, (_, vt) in zip(scores, pair_keys[idx // 2]):
            vx = jnp.concatenate([vt, jnp.ones((ONES_ROWS, vt.shape[1]), BF16)], axis=0)
            pv = _dot(vx, jnp.exp2(s - m).astype(BF16))
            acc = pv if acc is None else acc + pv
        outs.append(acc[:LANES] / (acc[LANES:LANES + 1] + jnp.exp2(sink - m)))
    row_o = lax.broadcasted_iota(jnp.int32, outs[0].shape, 0)
    for p in range(len(pair_keys)):
        a_t = jnp.where(row_o < HD_A, outs[2 * p], outs[2 * p + 1])
        for g in range(G_A):
            c0 = (p * G_A + g) * LANES
            o_ref[sub * blk:(sub + 1) * blk, c0:c0 + LANES] = a_t[:, g * blk:(g + 1) * blk].T.astype(BF16)


def _attn_latent_kernel(sink_ref, qt_ref, kp_ref, kc_ref, kn_ref, vtp_ref, vtc_ref, vtn_ref, kx_ref, vtx_ref,
                        o_ref, *, n_tok):
    assert WINDOW == ATTN_BLOCK
    blk = ATTN_BLOCK
    nsub = ATTN_STEP_BLOCKS
    shape = (blk, G_A * blk)
    krow = lax.broadcasted_iota(jnp.int32, shape, 0)
    tcol = lax.broadcasted_iota(jnp.int32, shape, 1) % blk
    for sub in range(nsub):
        j = pl.program_id(1) * nsub + sub
        before, after = krow >= tcol, krow <= tcol
        if sub == 0:
            before = before & (j > 0)
        if sub == nsub - 1:
            after = after & (j < n_tok // blk - 1)
        local = [before, None, after]
        pair_keys = []
        for p in range(NKV_A // 2):
            pl_ = slice(p * LANES, (p + 1) * LANES)
            k_blocks = ([kp_ref[:, pl_]] + [kc_ref[c * blk:(c + 1) * blk, pl_] for c in range(nsub)]
                        + [kn_ref[:, pl_]])[sub:sub + 3]
            vt_blocks = ([vtp_ref[0, pl_, :]] + [vtc_ref[c, pl_, :] for c in range(nsub)]
                         + [vtn_ref[0, pl_, :]])[sub:sub + 3]
            pair_keys.append([(jnp.concatenate(k_blocks, axis=0), jnp.concatenate(vt_blocks, axis=1)),
                              (kx_ref[0, :, pl_], vtx_ref[0, pl_, :])])
        _block_attention(sink_ref, qt_ref, o_ref, sub, pair_keys, lambda i, local=local: local if i == 0 else None)


def _attn_latent_call(qt, k, vt, kx, vtx, sink, bsz, seq_len):
    t = k.shape[0]
    nsub = ATTN_STEP_BLOCKS
    nb = seq_len // ATTN_BLOCK
    ns = nb // nsub
    n_ctx = kx.shape[1]
    prev = lambda b, j: b * nb + jnp.maximum(j * nsub - 1, 0)
    nxt = lambda b, j: b * nb + jnp.minimum(j * nsub + nsub, nb - 1)
    cur = lambda b, j: b * ns + j
    return pl.pallas_call(
        functools.partial(_attn_latent_kernel, n_tok=seq_len),
        grid=(bsz, ns),
        in_specs=[
            pl.BlockSpec(memory_space=pltpu.SMEM),
            pl.BlockSpec((nsub, NKV_A // 2, LANES, G_A * ATTN_BLOCK), lambda b, j: (cur(b, j), 0, 0, 0)),
            pl.BlockSpec((ATTN_BLOCK, KV_W), lambda b, j: (prev(b, j), 0)),
            pl.BlockSpec((nsub * ATTN_BLOCK, KV_W), lambda b, j: (cur(b, j), 0)),
            pl.BlockSpec((ATTN_BLOCK, KV_W), lambda b, j: (nxt(b, j), 0)),
            pl.BlockSpec((1, KV_W, ATTN_BLOCK), lambda b, j: (prev(b, j), 0, 0)),
            pl.BlockSpec((nsub, KV_W, ATTN_BLOCK), lambda b, j: (cur(b, j), 0, 0)),
            pl.BlockSpec((1, KV_W, ATTN_BLOCK), lambda b, j: (nxt(b, j), 0, 0)),
            pl.BlockSpec((1, n_ctx, KV_W), lambda b, j: (b, 0, 0)),
            pl.BlockSpec((1, KV_W, n_ctx), lambda b, j: (b, 0, 0)),
        ],
        out_specs=pl.BlockSpec((nsub * ATTN_BLOCK, Q_W), lambda b, j: (cur(b, j), 0)),
        out_shape=jax.ShapeDtypeStruct((t, Q_W), BF16),
        compiler_params=_cparams(
            ("parallel", "parallel"),
            per_step=(nsub * ATTN_BLOCK * (2 * Q_W + 2 * KV_W) + 2 * (2 * ATTN_BLOCK + n_ctx) * KV_W) * 2,
            values=nsub * NKV_A * (3 * ATTN_BLOCK + n_ctx) * G_A * ATTN_BLOCK * 6),
        name="attn_latent",
    )(sink, qt, k, k, k, vt, vt, vt, kx, vtx)


def _attn_context_kernel(sink_ref, qt_ref, k_ref, vt_ref, o_ref):
    pair_keys = []
    for p in range(NKV_A // 2):
        pl_ = slice(p * LANES, (p + 1) * LANES)
        vt = jnp.concatenate([vt_ref[c, pl_, :] for c in range(vt_ref.shape[0])], axis=1)
        pair_keys.append([(k_ref[:, pl_], vt)])
    for sub in range(qt_ref.shape[0]):
        _block_attention(sink_ref, qt_ref, o_ref, sub, pair_keys, lambda i: None)


def _attn_context_call(qt, k, vt, sink, bsz, seq_len):
    t = k.shape[0]
    nb = seq_len // ATTN_BLOCK
    return pl.pallas_call(
        _attn_context_kernel,
        grid=(bsz,),
        in_specs=[
            pl.BlockSpec(memory_space=pltpu.SMEM),
            pl.BlockSpec((nb, NKV_A // 2, LANES, G_A * ATTN_BLOCK), lambda b: (b, 0, 0, 0)),
            pl.BlockSpec((seq_len, KV_W), lambda b: (b, 0)),
            pl.BlockSpec((nb, KV_W, ATTN_BLOCK), lambda b: (b, 0, 0)),
        ],
        out_specs=pl.BlockSpec((seq_len, Q_W), lambda b: (b, 0)),
        out_shape=jax.ShapeDtypeStruct((t, Q_W), BF16),
        compiler_params=_cparams(("parallel",), per_step=seq_len * (2 * Q_W + 2 * KV_W) * 2,
                                 values=nb * NKV_A * seq_len * G_A * ATTN_BLOCK * 6),
        name="attn_context",
    )(sink, qt, k, vt)


def _rope_tables(n_tok):
    quarter = HD_A // 4
    freqs = ROPE_THETA ** (-jnp.arange(quarter, dtype=F32) / quarter)
    pos = jnp.arange(n_tok)
    row = (pos // GRID_W).astype(F32)
    col = (pos % GRID_W).astype(F32)
    ang_r, ang_c = row[:, None] * freqs, col[:, None] * freqs
    ang = jnp.concatenate([ang_r, ang_r, ang_c, ang_c], axis=-1)
    cos, sin = jnp.cos(ang), jnp.sin(ang)
    first = (jnp.arange(HD_A) % (2 * quarter)) < quarter
    sin_a = jnp.where(first, -sin, 0.0)
    sin_b = jnp.where(first, 0.0, sin)
    sin_t = jnp.where(first, -sin, sin)
    lane_tile = lambda a: jnp.tile(a, (1, LANES // HD_A))
    return lane_tile(cos), lane_tile(sin_a), lane_tile(sin_b), cos.T, sin_t.T


def _gate_lanes(a):
    lead = a.shape[:-2]
    rep = jnp.broadcast_to(a[..., :, None, :], lead + (2, GATE_COPIES, NH_M)).reshape(lead + (2 * GATE_DIR_LANES,))
    return jnp.concatenate([rep, jnp.zeros(lead + (LANES - 2 * GATE_DIR_LANES,), a.dtype)], axis=-1)


def _gate_layout(a):
    a4 = a.reshape(a.shape[:-1] + (2, 2, NH_M))
    return jnp.concatenate([_gate_lanes(a4[..., :, 0, :]), _gate_lanes(a4[..., :, 1, :])], axis=-1)


def _gate_selector():
    r = jnp.arange(LANES)[:, None]
    c = jnp.arange(2 * NH_M * 2 * LANES)[None, :]
    hd, j = c // (2 * LANES), c % (2 * LANES)
    rd, rk, rh = r // GATE_DIR_LANES, (r % GATE_DIR_LANES) // NH_M, r % NH_M
    hit = (r < 2 * GATE_DIR_LANES) & (rd == hd // NH_M) & (rh == hd % NH_M) & ((j < LANES) == (rk < 2))
    return hit.astype(BF16)


def _pair_layout_cols(w):
    d_in = w.shape[0]
    w5 = w.reshape(d_in, NKV_A // 2, 2, G_A, HD_A)
    return jnp.transpose(w5, (0, 1, 3, 2, 4)).reshape(d_in, Q_W)


def _mlstm_init(state_c, state_n, state_m):
    n_rep = jnp.broadcast_to(state_n[..., None].astype(F32), state_n.shape + (DV_M,))
    return state_c.astype(F32), n_rep, _gate_lanes(state_m.astype(F32))[:, None, :]


def kernel(x_prompt, x_sample, state_c, state_n, state_m, cache_k, cache_v, c, c_ctx, w_mod, b_mod, norm_g,
           ffn1_w_gu, ffn1_w_down, ffn2_w_gu, ffn2_w_down, mlstm_w_in, mlstm_b_gate, mlstm_g_head, mlstm_w_out,
           attn_w_in, attn_sink, attn_w_out, final_g):
    bp, n_p, _ = x_prompt.shape
    bs, n_s, _ = x_sample.shape
    xp = x_prompt.reshape(bp * n_p, D_MODEL)
    xs = x_sample.reshape(bs * n_s, D_MODEL)
    tm_ffn1 = tm_in = 1024
    tm_ffn2 = {"mlstm": 512, "attn": 1024}

    cond = jnp.concatenate([c_ctx[None, :], c], axis=0)
    cond = jnp.pad(cond, ((0, COND_ROWS - cond.shape[0]), (0, 0)))
    mod_all = _mod_call(cond, w_mod, b_mod).reshape(DEPTH, COND_ROWS, N_MOD, D_MODEL)

    outs = {}
    for l in range(DEPTH):
        mod_p = mod_all[l, 0:1]
        mod_s = mod_all[l, 1:1 + bs]
        g = norm_g[l]
        xp = _ffn_call(xp, mod_p, g[0:1], ffn1_w_gu, ffn1_w_down, l, 0, n_p, tm_ffn1)
        xs = _ffn_call(xs, mod_s, g[0:1], ffn1_w_gu, ffn1_w_down, l, 0, n_s, tm_ffn1)
        i = l // 2
        if l % 2 == 0:
            w_in = mlstm_w_in[i]
            wq = w_in[:, :QK_W].astype(BF16)
            wkt = w_in[:, QK_W:2 * QK_W].T.astype(BF16)
            wvo = w_in[:, 2 * QK_W:2 * QK_W + 2 * V_W].astype(BF16)
            wg = _gate_layout(w_in[:, 2 * QK_W + 2 * V_W:]).astype(BF16)
            bg = _gate_layout(mlstm_b_gate[i].astype(F32))[None, :]
            gh = mlstm_g_head[i].astype(F32)[None, :]
            w_out = mlstm_w_out[i].astype(BF16)
            sel = _gate_selector()
            streams = [("p", xp, mod_p, n_p, tm_in, bp, None),
                       ("s", xs, mod_s, n_s, tm_in, bs, _mlstm_init(state_c[:, i], state_n[:, i], state_m[:, i]))]
            res = {}
            for tag, x, mod, n_tok, tm, bsz, init in streams:
                q, kt, v, o, scans = _mlstm_in_call(x, mod, g[1:2], wq, wkt, wvo, wg, bg, n_tok, tm)
                hf, hb, c_fin, n_fin, m_fin = _mlstm_core_call(q, kt, v, scans, sel, init, bsz, n_tok)
                res[tag] = ((hf, hb, o, gh, w_out), c_fin, n_fin, m_fin)
            mix_p, c_fin, n_fin, m_fin = res["p"]
            mix_s = res["s"][0]
            mix_key = "mlstm"
            dt = x_prompt.dtype
            m_heads = m_fin[:, 0, :2 * GATE_DIR_LANES].reshape(bp, 2, GATE_COPIES, NH_M)[:, :, 0, :]
            outs.setdefault("c", []).append(c_fin.astype(dt))
            outs.setdefault("n", []).append(n_fin[..., 0].astype(dt))
            outs.setdefault("m", []).append(m_heads.astype(dt))
        else:
            w_in = attn_w_in[i]
            wqt = _pair_layout_cols(w_in[:, :Q_W]).T.astype(BF16)
            wk = w_in[:, Q_W:Q_W + KV_W].astype(BF16)
            wv = w_in[:, Q_W + KV_W:].astype(BF16)
            wvt = w_in[:, Q_W + KV_W:].T.astype(BF16)
            w_out = _pair_layout_cols(attn_w_out[i].T).T.astype(BF16)
            sink = attn_sink[i].astype(F32)[None, :]
            qt, k, vt, kf, vf = _attn_in_call(xp, mod_p, g[1:2], wqt, wk, wvt, n_p, tm_in, wv=wv)
            mix_p = (_attn_context_call(qt, k, vt, sink, bp, n_p), w_out)
            mix_key = "attn"
            outs.setdefault("k", []).append(kf.reshape(bp, n_p, NKV_A, HD_A))
            outs.setdefault("v", []).append(vf.reshape(bp, n_p, NKV_A, HD_A))
            qt, k, vt = _attn_in_call(xs, mod_s, g[1:2], wqt, wk, wvt, n_s, tm_in, rope_tabs=_rope_tables(n_s))
            n_ctx = cache_k.shape[2]
            kx = cache_k[:, i].reshape(bs, n_ctx, KV_W).astype(BF16)
            vtx = jnp.swapaxes(cache_v[:, i].reshape(bs, n_ctx, KV_W), 1, 2).astype(BF16)
            mix_s = (_attn_latent_call(qt, k, vt, kx, vtx, sink, bs, n_s), w_out)
        fg = final_g[None, :] if l == DEPTH - 1 else None
        tm2 = tm_ffn2[mix_key]
        xp = _ffn_call(xp, mod_p, g[2:3], ffn2_w_gu, ffn2_w_down, l, 2, n_p, tm2, final_g=fg, **{mix_key: mix_p})
        xs = _ffn_call(xs, mod_s, g[2:3], ffn2_w_gu, ffn2_w_down, l, 2, n_s, tm2, final_g=fg, **{mix_key: mix_s})

    return (xp.reshape(bp, n_p, D_MODEL), xs.reshape(bs, n_s, D_MODEL),
            jnp.stack(outs["c"], axis=1), jnp.stack(outs["n"], axis=1), jnp.stack(outs["m"], axis=1),
            jnp.stack(outs["k"], axis=1), jnp.stack(outs["v"], axis=1))
```

```python
import functools

import jax
import jax.numpy as jnp
from jax import lax
from jax.experimental import pallas as pl
from jax.experimental.pallas import tpu as pltpu

D_MODEL = 1024
DEPTH = 2
GRID_W = 64
D_FF = 2816
FFN_RES = 0.5
NH_M = 8
DK_M = 64
DV_M = 128
NH_A = 16
NKV_A = 4
G_A = NH_A // NKV_A
HD_A = 64
WINDOW = 128
ATTN_BLOCK = 128
ROPE_THETA = 10000.0
EPS = 1e-6
NEG_INF = -1e30
N_MOD = 9

QK_W = NH_M * DK_M
V_W = NH_M * DV_M
KV_W = NKV_A * HD_A
Q_W = NH_A * HD_A

LANES = 128
MLSTM_L = 128
MLSTM_STEP_CHUNKS = 8
FFN_CHUNK = 256
FFN_W_STEPS = D_FF // FFN_CHUNK
FFN_WGU_COLS = 2 * D_FF // FFN_W_STEPS
FFN_WD_ROWS = D_FF // FFN_W_STEPS
ATTN_STEP_BLOCKS = 8
ONES_ROWS = 16
LOG2_E = 1.4426950408889634
GATE_COPIES = 5
GATE_DIR_LANES = GATE_COPIES * NH_M
COND_ROWS = 16
VMEM_CAP = 60000 * 1024
COMPILER_TEMP_BYTES = 8 << 20

F32 = jnp.float32
BF16 = jnp.bfloat16


def _cparams(sem, resident=0, per_step=0, values=0):
    need = resident + 2 * per_step + values + COMPILER_TEMP_BYTES
    return pltpu.CompilerParams(dimension_semantics=sem, vmem_limit_bytes=min(int(need), VMEM_CAP))


def _dot(a, b):
    return jnp.dot(a, b, preferred_element_type=F32)


def _dot_nt(a, b):
    return lax.dot_general(a, b, (((1,), (1,)), ((), ())), preferred_element_type=F32)


def _rms(x):
    return x * lax.rsqrt(jnp.mean(x * x, axis=-1, keepdims=True) + EPS)


def _sigmoid(x):
    return 1.0 / (1.0 + jnp.exp(-x))


def _mod_norm(x, g, shift, scale):
    return (_rms(x) * g) * (1.0 + scale) + shift


def _mod_kernel(c_ref, w_ref, b_ref, o_ref):
    c = c_ref[...]
    s = (c * _sigmoid(c)).astype(BF16)
    o_ref[0] = _dot(s, w_ref[0].astype(BF16)) + b_ref[0]


def _mod_call(cond, w_mod, b_mod):
    tn = D_MODEL
    n_out = N_MOD * D_MODEL
    return pl.pallas_call(
        _mod_kernel,
        grid=(DEPTH, n_out // tn),
        in_specs=[
            pl.BlockSpec((COND_ROWS, D_MODEL), lambda l, n: (0, 0)),
            pl.BlockSpec((1, D_MODEL, tn), lambda l, n: (l, 0, n)),
            pl.BlockSpec((1, 1, tn), lambda l, n: (l, 0, n)),
        ],
        out_specs=pl.BlockSpec((1, COND_ROWS, tn), lambda l, n: (l, 0, n)),
        out_shape=jax.ShapeDtypeStruct((DEPTH, COND_ROWS, n_out), F32),
        compiler_params=_cparams(("parallel", "parallel"), per_step=D_MODEL * tn * 4, values=D_MODEL * tn * 2),
        name="adaln_mod",
    )(cond, w_mod, b_mod.reshape(DEPTH, 1, n_out))


def _mod_spec(mod, tiles_per_seq, tile=lambda i: i):
    if mod.shape[0] == 1:
        return pl.BlockSpec((1, N_MOD, D_MODEL), lambda i: (0, 0, 0))
    return pl.BlockSpec((1, N_MOD, D_MODEL), lambda i: (tile(i) // tiles_per_seq, 0, 0))


def _row_spec(width):
    return pl.BlockSpec((1, width), lambda i: (0, 0))


def _resident(shape):
    return pl.BlockSpec(shape, lambda i: (0,) * len(shape), pipeline_mode=pl.Buffered(1))


def _mlstm_gated_heads(hf_ref, hb_ref, o_ref, gh_ref):
    hs = hf_ref[...].astype(F32) + hb_ref[...].astype(F32)
    parts = [_rms(hs[:, h * DV_M:(h + 1) * DV_M]) for h in range(NH_M)]
    hn = jnp.concatenate(parts, axis=1) * gh_ref[...]
    return (hn * _sigmoid(o_ref[...].astype(F32))).astype(BF16)


def _ffn_kernel(*refs, j, final, mixer):
    x_ref, mod_ref, g_ref, wgu32_ref, wd32_ref, fg_ref = refs[:6]
    o_ref, wgu_ref, wd_ref = refs[-3:]
    step = pl.program_id(0)

    @pl.when(step < FFN_W_STEPS)
    def _():
        c0 = pl.multiple_of(step * FFN_WGU_COLS, FFN_WGU_COLS)
        r0 = pl.multiple_of(step * FFN_WD_ROWS, FFN_WD_ROWS)
        wgu_ref[:, pl.ds(c0, FFN_WGU_COLS)] = wgu32_ref[0].astype(BF16)
        wd_ref[pl.ds(r0, FFN_WD_ROWS), :] = wd32_ref[0].astype(BF16)

    @pl.when(step >= FFN_W_STEPS)
    def _():
        x = x_ref[...]
        mod = mod_ref[0]
        if mixer == "mlstm":
            hf_ref, hb_ref, og_ref, gh_ref, wo_ref = refs[6:11]
            x = x + mod[5:6] * _dot(_mlstm_gated_heads(hf_ref, hb_ref, og_ref, gh_ref), wo_ref[...])
        elif mixer == "attn":
            a_ref, wo_ref = refs[6:8]
            x = x + mod[5:6] * _dot(a_ref[...], wo_ref[...])
        shift, scale, gate = mod[3 * j:3 * j + 1], mod[3 * j + 1:3 * j + 2], mod[3 * j + 2:3 * j + 3]
        hb = _mod_norm(x, g_ref[...], shift, scale).astype(BF16)
        acc = jnp.zeros(x.shape, F32)
        for c in range(D_FF // FFN_CHUNK):
            lo = c * FFN_CHUNK
            gg = _dot(hb, wgu_ref[:, lo:lo + FFN_CHUNK])
            uu = _dot(hb, wgu_ref[:, D_FF + lo:D_FF + lo + FFN_CHUNK])
            act = (gg * _sigmoid(gg) * uu).astype(BF16)
            acc = acc + _dot(act, wd_ref[lo:lo + FFN_CHUNK, :])
        y = x + (FFN_RES * gate) * acc
        if final:
            y = _rms(y) * fg_ref[...]
        o_ref[...] = y


def _ffn_call(x, mod, g, wgu, wd, layer, j, seq_len, tm, final_g=None, mlstm=None, attn=None):
    t = x.shape[0]
    final = final_g is not None
    fg = final_g if final else g
    tile = lambda i: jnp.maximum(i - FFN_W_STEPS, 0)
    wstep = lambda i: jnp.minimum(i, FFN_W_STEPS - 1)
    tok = lambda width: pl.BlockSpec((tm, width), lambda i: (tile(i), 0))
    in_specs = [tok(D_MODEL), _mod_spec(mod, seq_len // tm, tile), _row_spec(D_MODEL),
                pl.BlockSpec((1, D_MODEL, FFN_WGU_COLS), lambda i: (layer, 0, wstep(i))),
                pl.BlockSpec((1, FFN_WD_ROWS, D_MODEL), lambda i: (layer, wstep(i), 0)), _row_spec(D_MODEL)]
    args = [x, mod, g, wgu, wd, fg]
    resident = 2 * (D_MODEL * 2 * D_FF + D_FF * D_MODEL)
    per_step = 2 * tm * D_MODEL * 4 + 4 * (D_MODEL * FFN_WGU_COLS + FFN_WD_ROWS * D_MODEL)
    values = tm * D_MODEL * (2 + 4 + 4) + 3 * tm * FFN_CHUNK * 4
    mixer = None
    if mlstm is not None:
        mixer = "mlstm"
        in_specs += [tok(V_W), tok(V_W), tok(V_W), _row_spec(V_W), _resident((V_W, D_MODEL))]
        args += list(mlstm)
        resident += 2 * V_W * D_MODEL
        per_step += 3 * tm * V_W * 2
        values += 3 * tm * V_W * 4
    elif attn is not None:
        mixer = "attn"
        in_specs += [tok(Q_W), _resident((Q_W, D_MODEL))]
        args += list(attn)
        resident += 2 * Q_W * D_MODEL
        per_step += tm * Q_W * 2
        values += tm * Q_W * 4
    return pl.pallas_call(
        functools.partial(_ffn_kernel, j=j, final=final, mixer=mixer),
        grid=(FFN_W_STEPS + t // tm,),
        in_specs=in_specs,
        out_specs=tok(D_MODEL),
        out_shape=jax.ShapeDtypeStruct((t, D_MODEL), F32),
        scratch_shapes=[pltpu.VMEM((D_MODEL, 2 * D_FF), BF16), pltpu.VMEM((D_FF, D_MODEL), BF16)],
        compiler_params=_cparams(("arbitrary",), resident, per_step, values),
        name=f"ffn{j // 2 + 1}" + (f"_{mixer}" if mixer else ""),
    )(*args)


def _scan_rows(x, op, fill, reverse):
    n = x.shape[0]
    idx = lax.broadcasted_iota(jnp.int32, x.shape, 0)
    s = 1
    while s < n:
        if reverse:
            x = op(x, jnp.where(idx < n - s, pltpu.roll(x, n - s, axis=0), fill))
        else:
            x = op(x, jnp.where(idx >= s, pltpu.roll(x, s, axis=0), fill))
        s *= 2
    return x


def _mlstm_in_kernel(x_ref, mod_ref, g_ref, wq_ref, wkt_ref, wvo_ref, wg_ref, bg_ref,
                     q_ref, kt_ref, v_ref, o_ref, scan_ref):
    mod = mod_ref[0]
    hb = _mod_norm(x_ref[...], g_ref[...], mod[3:4], mod[4:5]).astype(BF16)
    gt = _dot(hb, wg_ref[...]) + bg_ref[...]
    q_ref[...] = (_dot(hb, wq_ref[...]) * DK_M ** -0.5).astype(BF16)
    kt_ref[...] = _dot_nt(wkt_ref[...], hb).astype(BF16)
    vo = _dot(hb, wvo_ref[...])
    v_ref[...] = vo[:, :V_W].astype(BF16)
    o_ref[...] = vo[:, V_W:].astype(BF16)
    n = MLSTM_L
    fwd = lax.broadcasted_iota(jnp.int32, (n, LANES), 1) < GATE_DIR_LANES
    for c in range(gt.shape[0] // n):
        rows = slice(c * n, (c + 1) * n)
        li = gt[rows, :LANES]
        fx = gt[rows, LANES:]
        lf = jnp.minimum(fx, 0.0) - jnp.log(1.0 + jnp.exp(-jnp.abs(fx)))
        pre = _scan_rows(lf, jnp.add, 0.0, False)
        b = jnp.where(fwd, pre, pre[n - 1:n, :] - pre + lf)
        a = li - b
        scan_ref[rows, 0:LANES] = a
        scan_ref[rows, LANES:2 * LANES] = b
        scan_ref[rows, 2 * LANES:] = jnp.where(fwd, _scan_rows(a, jnp.maximum, -jnp.inf, False),
                                               _scan_rows(a, jnp.maximum, -jnp.inf, True))


def _mlstm_in_call(x, mod, g, wq, wkt, wvo, wg, bg, seq_len, tm):
    t = x.shape[0]
    out_cols = 2 * QK_W + 2 * V_W
    resident = 2 * D_MODEL * (out_cols + 2 * LANES)
    per_step = tm * (D_MODEL * 4 + out_cols * 2 + 3 * LANES * 4)
    values = tm * (D_MODEL * 2 + out_cols * 4 + 5 * LANES * 4)
    return pl.pallas_call(
        _mlstm_in_kernel,
        grid=(t // tm,),
        in_specs=[
            pl.BlockSpec((tm, D_MODEL), lambda i: (i, 0)),
            _mod_spec(mod, seq_len // tm),
            _row_spec(D_MODEL),
            _resident((D_MODEL, QK_W)),
            _resident((QK_W, D_MODEL)),
            _resident((D_MODEL, 2 * V_W)),
            _resident((D_MODEL, 2 * LANES)),
            _row_spec(2 * LANES),
        ],
        out_specs=[
            pl.BlockSpec((tm, QK_W), lambda i: (i, 0)),
            pl.BlockSpec((QK_W, tm), lambda i: (0, i)),
            pl.BlockSpec((tm, V_W), lambda i: (i, 0)),
            pl.BlockSpec((tm, V_W), lambda i: (i, 0)),
            pl.BlockSpec((tm, 3 * LANES), lambda i: (i, 0)),
        ],
        out_shape=[
            jax.ShapeDtypeStruct((t, QK_W), BF16),
            jax.ShapeDtypeStruct((QK_W, t), BF16),
            jax.ShapeDtypeStruct((t, V_W), BF16),
            jax.ShapeDtypeStruct((t, V_W), BF16),
            jax.ShapeDtypeStruct((t, 3 * LANES), F32),
        ],
        compiler_params=_cparams(("parallel",), resident, per_step, values),
        name="mlstm_in",
    )(x, mod, g, wq, wkt, wvo, wg, bg)


def _bf16_part(x):
    return x.astype(BF16).astype(F32)


def _mlstm_gates(d, scan_ref, rows, m):
    n = MLSTM_L
    a, b, cm = scan_ref[rows, 0:LANES], scan_ref[rows, LANES:2 * LANES], scan_ref[rows, 2 * LANES:]
    rev = d == 1
    mt = jnp.maximum(cm, m) * LOG2_E
    u_hi = _bf16_part(mt)
    u_lo = _bf16_part(mt - u_hi)
    z = -(b * LOG2_E + (u_hi + u_lo))
    z_hi = _bf16_part(z)
    z_mid = _bf16_part(z - z_hi)
    z_lo = _bf16_part(z - z_hi - z_mid)
    lane = lax.broadcasted_iota(jnp.int32, mt.shape, 1)
    k = (lane - d * GATE_DIR_LANES) // NH_M
    terms = jnp.where(k == 0, u_hi, jnp.where(k == 1, u_lo, jnp.where(k == 2, z_hi, jnp.where(k == 3, z_mid, z_lo))))
    end = 0 if rev else n - 1
    mx = jnp.maximum(m, cm[end:end + 1, :])
    keep = jnp.exp(m - mx)
    m_new = b[end:end + 1, :] + mx
    return terms, (a * LOG2_E).T, jnp.exp(a - mx).T, keep, m_new


def _mlstm_heads(dirs, c_sc, spread, m2):
    n = MLSTM_L
    row = lax.broadcasted_iota(jnp.int32, (n, n), 0)
    col = lax.broadcasted_iota(jnp.int32, (n, n), 1)
    ones = jnp.ones((n, LANES), BF16)
    heads = [(dd, h) for dd in dirs for h in range(NH_M)]
    kpad = jnp.zeros((DK_M, DK_M), BF16)
    lhs = []
    for (d, q_ref, kt_ref, _, _, rows, a_t, wend_t, _), h in heads:
        c = d * GATE_DIR_LANES + h
        base = (d * NH_M + h) * 2 * LANES
        u = spread[:, base:base + LANES]
        qh = q_ref[rows, h * DK_M:(h + 1) * DK_M]
        kth = kt_ref[h * DK_M:(h + 1) * DK_M, rows]
        e = jnp.exp2(a_t[c:c + 1, :] - u)
        mask = (col >= row) if d == 1 else (col <= row)
        w = jnp.where(mask, _dot(qh, kth) * e, 0.0).astype(BF16)
        inter = jnp.exp2(m2[:, c:c + 1] - u)
        qi = (inter[:, :DK_M] * qh.astype(F32)).astype(BF16)
        kts = (kth.astype(F32) * wend_t[c:c + 1, :]).astype(BF16)
        lhs.append(jnp.concatenate([jnp.concatenate([w, qi], axis=1), jnp.concatenate([kts, kpad], axis=1)], axis=0))
    for ((d, _, _, v_ref, h_ref, rows, _, _, keep), h), wq in zip(heads, lhs):
        c = d * GATE_DIR_LANES + h
        base = (d * NH_M + h) * 2 * LANES
        zz = spread[:, base + LANES:base + 2 * LANES]
        vext = jnp.concatenate([v_ref[rows, h * DV_M:(h + 1) * DV_M], ones], axis=1)
        cext = c_sc[d, h]
        r = _dot(wq, jnp.concatenate([vext, cext.astype(BF16)], axis=0))
        den = jnp.maximum(jnp.abs(r[:n, DV_M:]), jnp.exp2(zz))
        h_ref[rows, h * DV_M:(h + 1) * DV_M] = (r[:n, :DV_M] / den).astype(BF16)
        c_sc[d, h] = keep[:, c:c + 1] * cext + r[n:]


def _mlstm_kernel(*refs, zero_init):
    qf_ref, ktf_ref, vf_ref, scanf_ref, qb_ref, ktb_ref, vb_ref, scanb_ref, sel_ref = refs[:9]
    hf_ref, hb_ref, cout_ref, nout_ref, mout_ref, c_sc, m_sc = refs[-7:]
    i = pl.program_id(1)

    @pl.when(i == 0)
    def _():
        if zero_init:
            c_sc[...] = jnp.zeros(c_sc.shape, F32)
            m_sc[...] = jnp.zeros(m_sc.shape, F32)
        else:
            c0_ref, n0_ref, m0_ref = refs[9:12]
            c_sc[:, :, :, :DV_M] = c0_ref[0]
            c_sc[:, :, :, DV_M:] = n0_ref[0]
            m_sc[...] = m0_ref[0]

    nsub = qf_ref.shape[0] // MLSTM_L
    for sub in range(nsub):
        rows_f = slice(sub * MLSTM_L, (sub + 1) * MLSTM_L)
        rows_b = slice((nsub - 1 - sub) * MLSTM_L, (nsub - sub) * MLSTM_L)
        m = m_sc[...]
        terms_f, at_f, wend_f, keep_f, mnew_f = _mlstm_gates(0, scanf_ref, rows_f, m)
        terms_b, at_b, wend_b, keep_b, mnew_b = _mlstm_gates(1, scanb_ref, rows_b, m)
        lane = lax.broadcasted_iota(jnp.int32, terms_f.shape, 1)
        packed = jnp.where(lane < GATE_DIR_LANES, terms_f, jnp.where(lane < 2 * GATE_DIR_LANES, terms_b, 0.0))
        spread = _dot(packed.astype(BF16), sel_ref[...])
        m2 = m * LOG2_E
        _mlstm_heads([(0, qf_ref, ktf_ref, vf_ref, hf_ref, rows_f, at_f, wend_f, keep_f),
                      (1, qb_ref, ktb_ref, vb_ref, hb_ref, rows_b, at_b, wend_b, keep_b)], c_sc, spread, m2)
        lane_m = lax.broadcasted_iota(jnp.int32, m.shape, 1)
        m_sc[...] = jnp.where(lane_m < GATE_DIR_LANES, mnew_f, jnp.where(lane_m < 2 * GATE_DIR_LANES, mnew_b, m))

    @pl.when(i == pl.num_programs(1) - 1)
    def _():
        cout_ref[0] = c_sc[:, :, :, :DV_M]
        nout_ref[0] = c_sc[:, :, :, DV_M:]
        mout_ref[0] = m_sc[...]


def _mlstm_core_call(q, kt, v, scans, sel, init, bsz, seq_len):
    t = q.shape[0]
    rows = min(MLSTM_STEP_CHUNKS * MLSTM_L, seq_len)
    nc = seq_len // rows
    fwd = lambda b, i: (b * nc + i, 0)
    bwd = lambda b, i: (b * nc + nc - 1 - i, 0)
    fwd_t = lambda b, i: (0, b * nc + i)
    bwd_t = lambda b, i: (0, b * nc + nc - 1 - i)

    def specs(row_map, col_map):
        return [
            pl.BlockSpec((rows, QK_W), row_map),
            pl.BlockSpec((QK_W, rows), col_map),
            pl.BlockSpec((rows, V_W), row_map),
            pl.BlockSpec((rows, 3 * LANES), row_map),
        ]

    half_shape = (2, NH_M, DK_M, DV_M)
    half_spec = pl.BlockSpec((1,) + half_shape, lambda b, i: (b, 0, 0, 0, 0))
    m_spec = pl.BlockSpec((1, 1, LANES), lambda b, i: (b, 0, 0))
    state_bytes = 2 * NH_M * DK_M * 2 * DV_M * 4
    sel_shape = (LANES, 2 * NH_M * 2 * LANES)
    in_specs = specs(fwd, fwd_t) + specs(bwd, bwd_t) + [pl.BlockSpec(sel_shape, lambda b, i: (0, 0))]
    args = [q, kt, v, scans, q, kt, v, scans, sel]
    if init is not None:
        in_specs += [half_spec, half_spec, m_spec]
        args += list(init)
    per_step = 2 * rows * ((2 * QK_W + 2 * V_W) * 2 + 3 * LANES * 4) + 2 * sel_shape[0] * sel_shape[1] + 2 * state_bytes
    values = MLSTM_L * sel_shape[1] * 4 + 2 * NH_M * MLSTM_L * 4 * LANES * 4
    return pl.pallas_call(
        functools.partial(_mlstm_kernel, zero_init=init is None),
        grid=(bsz, nc),
        in_specs=in_specs,
        out_specs=[pl.BlockSpec((rows, V_W), fwd), pl.BlockSpec((rows, V_W), bwd), half_spec, half_spec, m_spec],
        out_shape=[
            jax.ShapeDtypeStruct((t, V_W), BF16),
            jax.ShapeDtypeStruct((t, V_W), BF16),
            jax.ShapeDtypeStruct((bsz,) + half_shape, F32),
            jax.ShapeDtypeStruct((bsz,) + half_shape, F32),
            jax.ShapeDtypeStruct((bsz, 1, LANES), F32),
        ],
        scratch_shapes=[pltpu.VMEM((2, NH_M, DK_M, 2 * DV_M), F32), pltpu.VMEM((1, LANES), F32)],
        compiler_params=_cparams(("parallel", "arbitrary"), state_bytes, per_step, values),
        name="mlstm_core",
    )(*args)


def _rope_rows_swap(x):
    q = HD_A // 4
    parts = []
    for base in range(0, x.shape[0], 2 * q):
        parts += [x[base + q:base + 2 * q], x[base:base + q]]
    return jnp.concatenate(parts, axis=0)


def _attn_in_kernel(*refs, rope, emit_f32):
    x_ref, mod_ref, g_ref, wqt_ref, wk_ref, wvt_ref = refs[:6]
    pos = 6
    if rope:
        cos_ref, sina_ref, sinb_ref, cost_ref, sint_ref = refs[pos:pos + 5]
        pos += 5
    if emit_f32:
        wv_ref = refs[pos]
        pos += 1
    qt_ref, k_ref, vt_ref = refs[pos:pos + 3]
    pos += 3
    mod = mod_ref[0]
    hb = _mod_norm(x_ref[...], g_ref[...], mod[3:4], mod[4:5]).astype(BF16)
    qt = _dot_nt(wqt_ref[...], hb)
    k = _dot(hb, wk_ref[...])
    blk = ATTN_BLOCK
    vt = _dot_nt(wvt_ref[...], hb).astype(BF16)
    for tb in range(vt.shape[1] // blk):
        vt_ref[tb] = vt[:, tb * blk:(tb + 1) * blk]
    if emit_f32:
        kf_ref, vf_ref = refs[pos:pos + 2]
        kf_ref[...] = k
        vf_ref[...] = _dot(hb, wv_ref[...])
    for gi in range(Q_W // LANES):
        qg = qt[gi * LANES:(gi + 1) * LANES, :]
        if rope:
            cos_t = jnp.concatenate([cost_ref[...]] * (LANES // HD_A), axis=0)
            sin_t = jnp.concatenate([sint_ref[...]] * (LANES // HD_A), axis=0)
            qg = qg * cos_t + _rope_rows_swap(qg) * sin_t
        qg = (qg * (HD_A ** -0.5 * LOG2_E)).astype(BF16)
        p, g = gi // G_A, gi % G_A
        for tb in range(qt.shape[1] // blk):
            qt_ref[tb, p, :, g * blk:(g + 1) * blk] = qg[:, tb * blk:(tb + 1) * blk]
    for gi in range(KV_W // LANES):
        kg = k[:, gi * LANES:(gi + 1) * LANES]
        if rope:
            kg = (kg * cos_ref[...] + pltpu.roll(kg, LANES - HD_A // 4, axis=1) * sina_ref[...]
                  + pltpu.roll(kg, HD_A // 4, axis=1) * sinb_ref[...])
        k_ref[:, gi * LANES:(gi + 1) * LANES] = kg.astype(BF16)


def _attn_in_call(x, mod, g, wqt, wk, wvt, seq_len, tm, rope_tabs=None, wv=None):
    t = x.shape[0]
    rope = rope_tabs is not None
    emit_f32 = wv is not None
    tiles = seq_len // tm
    tok = lambda width: pl.BlockSpec((tm, width), lambda i: (i, 0))
    in_specs = [tok(D_MODEL), _mod_spec(mod, tiles), _row_spec(D_MODEL),
                _resident((Q_W, D_MODEL)), _resident((D_MODEL, KV_W)), _resident((KV_W, D_MODEL))]
    args = [x, mod, g, wqt, wk, wvt]
    if rope:
        pos_row = lambda i: (i % tiles, 0)
        pos_col = lambda i: (0, i % tiles)
        in_specs += [pl.BlockSpec((tm, LANES), pos_row)] * 3 + [pl.BlockSpec((HD_A, tm), pos_col)] * 2
        args += list(rope_tabs)
    if emit_f32:
        in_specs.append(_resident((D_MODEL, KV_W)))
        args.append(wv)
    nblk = tm // ATTN_BLOCK
    qt_slab = (NKV_A // 2, LANES, G_A * ATTN_BLOCK)
    vt_slab = (KV_W, ATTN_BLOCK)
    out_specs = [pl.BlockSpec((nblk,) + qt_slab, lambda i: (i, 0, 0, 0)), tok(KV_W),
                 pl.BlockSpec((nblk,) + vt_slab, lambda i: (i, 0, 0))]
    out_shape = [jax.ShapeDtypeStruct((t // ATTN_BLOCK,) + qt_slab, BF16), jax.ShapeDtypeStruct((t, KV_W), BF16),
                 jax.ShapeDtypeStruct((t // ATTN_BLOCK,) + vt_slab, BF16)]
    if emit_f32:
        out_specs += [tok(KV_W), tok(KV_W)]
        out_shape += [jax.ShapeDtypeStruct((t, KV_W), F32)] * 2
    resident = 2 * D_MODEL * (Q_W + 3 * KV_W)
    per_step = tm * (D_MODEL * 4 + (Q_W + 2 * KV_W) * 2 + 2 * KV_W * 4 + (3 * LANES + 2 * HD_A) * 4)
    values = tm * (D_MODEL * 2 + (Q_W + 3 * KV_W) * 4 + Q_W * 4)
    return pl.pallas_call(
        functools.partial(_attn_in_kernel, rope=rope, emit_f32=emit_f32),
        grid=(t // tm,),
        in_specs=in_specs,
        out_specs=out_specs,
        out_shape=out_shape,
        compiler_params=_cparams(("parallel",), resident, per_step, values),
        name="attn_in",
    )(*args)


def _block_attention(sink_ref, qt_ref, o_ref, sub, pair_keys, mask_of):
    blk = ATTN_BLOCK
    cols_all = G_A * blk
    col_g = lax.broadcasted_iota(jnp.int32, (1, cols_all), 1) // blk
    staged = []
    for p, keys in enumerate(pair_keys):
        qp = qt_ref[sub, p]
        row = lax.broadcasted_iota(jnp.int32, qp.shape, 0)
        for e in range(2):
            kv = 2 * p + e
            qm = jnp.where((row >= e * HD_A) & (row < (e + 1) * HD_A), qp, jnp.zeros_like(qp))
            sink = jnp.full((1, cols_all), sink_ref[0, kv * G_A], F32)
            for g in range(1, G_A):
                sink = jnp.where(col_g == g, sink_ref[0, kv * G_A + g], sink)
            sink = sink * LOG2_E
            scores = []
            m = sink
            for i, (k, _) in enumerate(keys):
                s = _dot(k, qm)
                blocks = mask_of(i)
                if blocks is not None:
                    s = jnp.concatenate([s[r * blk:(r + 1) * blk] if mk is None
                                         else jnp.where(mk, s[r * blk:(r + 1) * blk], NEG_INF)
                                         for r, mk in enumerate(blocks)], axis=0)
                m = jnp.maximum(m, jnp.max(s, axis=0, keepdims=True))
                scores.append(s)
            staged.append((scores, m, sink))
    outs = []
    for idx, (scores, m, sink) in enumerate(staged):
        acc = None
        for s, (_, vt) in zip(scores, pair_keys[idx // 2]):
            vx = jnp.concatenate([vt, jnp.ones((ONES_ROWS, vt.shape[1]), BF16)], axis=0)
            pv = _dot(vx, jnp.exp2(s - m).astype(BF16))
            acc = pv if acc is None else acc + pv
        outs.append(acc[:LANES] / (acc[LANES:LANES + 1] + jnp.exp2(sink - m)))
    row_o = lax.broadcasted_iota(jnp.int32, outs[0].shape, 0)
    for p in range(len(pair_keys)):
        a_t = jnp.where(row_o < HD_A, outs[2 * p], outs[2 * p + 1])
        for g in range(G_A):
            c0 = (p * G_A + g) * LANES
            o_ref[sub * blk:(sub + 1) * blk, c0:c0 + LANES] = a_t[:, g * blk:(g + 1) * blk].T.astype(BF16)


def _attn_latent_kernel(sink_ref, qt_ref, kp_ref, kc_ref, kn_ref, vtp_ref, vtc_ref, vtn_ref, kx_ref, vtx_ref,
                        o_ref, *, n_tok):
    assert WINDOW == ATTN_BLOCK
    blk = ATTN_BLOCK
    nsub = ATTN_STEP_BLOCKS
    shape = (blk, G_A * blk)
    krow = lax.broadcasted_iota(jnp.int32, shape, 0)
    tcol = lax.broadcasted_iota(jnp.int32, shape, 1) % blk
    for sub in range(nsub):
        j = pl.program_id(1) * nsub + sub
        before, after = krow >= tcol, krow <= tcol
        if sub == 0:
            before = before & (j > 0)
        if sub == nsub - 1:
            after = after & (j < n_tok // blk - 1)
        local = [before, None, after]
        pair_keys = []
        for p in range(NKV_A // 2):
            pl_ = slice(p * LANES, (p + 1) * LANES)
            k_blocks = ([kp_ref[:, pl_]] + [kc_ref[c * blk:(c + 1) * blk, pl_] for c in range(nsub)]
                        + [kn_ref[:, pl_]])[sub:sub + 3]
            vt_blocks = ([vtp_ref[0, pl_, :]] + [vtc_ref[c, pl_, :] for c in range(nsub)]
                         + [vtn_ref[0, pl_, :]])[sub:sub + 3]
            pair_keys.append([(jnp.concatenate(k_blocks + [kx_ref[0, :, pl_]], axis=0),
                               jnp.concatenate(vt_blocks + [vtx_ref[0, pl_, :]], axis=1))])
        n_ctx_blocks = kx_ref.shape[1] // blk
        _block_attention(sink_ref, qt_ref, o_ref, sub, pair_keys,
                         lambda i, local=local: local + [None] * n_ctx_blocks)


def _attn_latent_call(qt, k, vt, kx, vtx, sink, bsz, seq_len):
    t = k.shape[0]
    nsub = ATTN_STEP_BLOCKS
    nb = seq_len // ATTN_BLOCK
    ns = nb // nsub
    n_ctx = kx.shape[1]
    prev = lambda b, j: b * nb + jnp.maximum(j * nsub - 1, 0)
    nxt = lambda b, j: b * nb + jnp.minimum(j * nsub + nsub, nb - 1)
    cur = lambda b, j: b * ns + j
    return pl.pallas_call(
        functools.partial(_attn_latent_kernel, n_tok=seq_len),
        grid=(bsz, ns),
        in_specs=[
            pl.BlockSpec(memory_space=pltpu.SMEM),
            pl.BlockSpec((nsub, NKV_A // 2, LANES, G_A * ATTN_BLOCK), lambda b, j: (cur(b, j), 0, 0, 0)),
            pl.BlockSpec((ATTN_BLOCK, KV_W), lambda b, j: (prev(b, j), 0)),
            pl.BlockSpec((nsub * ATTN_BLOCK, KV_W), lambda b, j: (cur(b, j), 0)),
            pl.BlockSpec((ATTN_BLOCK, KV_W), lambda b, j: (nxt(b, j), 0)),
            pl.BlockSpec((1, KV_W, ATTN_BLOCK), lambda b, j: (prev(b, j), 0, 0)),
            pl.BlockSpec((nsub, KV_W, ATTN_BLOCK), lambda b, j: (cur(b, j), 0, 0)),
            pl.BlockSpec((1, KV_W, ATTN_BLOCK), lambda b, j: (nxt(b, j), 0, 0)),
            pl.BlockSpec((1, n_ctx, KV_W), lambda b, j: (b, 0, 0)),
            pl.BlockSpec((1, KV_W, n_ctx), lambda b, j: (b, 0, 0)),
        ],
        out_specs=pl.BlockSpec((nsub * ATTN_BLOCK, Q_W), lambda b, j: (cur(b, j), 0)),
        out_shape=jax.ShapeDtypeStruct((t, Q_W), BF16),
        compiler_params=_cparams(
            ("parallel", "parallel"),
            per_step=(nsub * ATTN_BLOCK * (2 * Q_W + 2 * KV_W) + 2 * (2 * ATTN_BLOCK + n_ctx) * KV_W) * 2,
            values=nsub * NKV_A * (3 * ATTN_BLOCK + n_ctx) * G_A * ATTN_BLOCK * 6),
        name="attn_latent",
    )(sink, qt, k, k, k, vt, vt, vt, kx, vtx)


def _attn_context_kernel(sink_ref, qt_ref, k_ref, vt_ref, o_ref):
    pair_keys = []
    for p in range(NKV_A // 2):
        pl_ = slice(p * LANES, (p + 1) * LANES)
        vt = jnp.concatenate([vt_ref[c, pl_, :] for c in range(vt_ref.shape[0])], axis=1)
        pair_keys.append([(k_ref[:, pl_], vt)])
    for sub in range(qt_ref.shape[0]):
        _block_attention(sink_ref, qt_ref, o_ref, sub, pair_keys, lambda i: None)


def _attn_context_call(qt, k, vt, sink, bsz, seq_len):
    t = k.shape[0]
    nb = seq_len // ATTN_BLOCK
    return pl.pallas_call(
        _attn_context_kernel,
        grid=(bsz,),
        in_specs=[
            pl.BlockSpec(memory_space=pltpu.SMEM),
            pl.BlockSpec((nb, NKV_A // 2, LANES, G_A * ATTN_BLOCK), lambda b: (b, 0, 0, 0)),
            pl.BlockSpec((seq_len, KV_W), lambda b: (b, 0)),
            pl.BlockSpec((nb, KV_W, ATTN_BLOCK), lambda b: (b, 0, 0)),
        ],
        out_specs=pl.BlockSpec((seq_len, Q_W), lambda b: (b, 0)),
        out_shape=jax.ShapeDtypeStruct((t, Q_W), BF16),
        compiler_params=_cparams(("parallel",), per_step=seq_len * (2 * Q_W + 2 * KV_W) * 2,
                                 values=nb * NKV_A * seq_len * G_A * ATTN_BLOCK * 6),
        name="attn_context",
    )(sink, qt, k, vt)


def _rope_tables(n_tok):
    quarter = HD_A // 4
    freqs = ROPE_THETA ** (-jnp.arange(quarter, dtype=F32) / quarter)
    pos = jnp.arange(n_tok)
    row = (pos // GRID_W).astype(F32)
    col = (pos % GRID_W).astype(F32)
    ang_r, ang_c = row[:, None] * freqs, col[:, None] * freqs
    ang = jnp.concatenate([ang_r, ang_r, ang_c, ang_c], axis=-1)
    cos, sin = jnp.cos(ang), jnp.sin(ang)
    first = (jnp.arange(HD_A) % (2 * quarter)) < quarter
    sin_a = jnp.where(first, -sin, 0.0)
    sin_b = jnp.where(first, 0.0, sin)
    sin_t = jnp.where(first, -sin, sin)
    lane_tile = lambda a: jnp.tile(a, (1, LANES // HD_A))
    return lane_tile(cos), lane_tile(sin_a), lane_tile(sin_b), cos.T, sin_t.T


def _gate_lanes(a):
    lead = a.shape[:-2]
    rep = jnp.broadcast_to(a[..., :, None, :], lead + (2, GATE_COPIES, NH_M)).reshape(lead + (2 * GATE_DIR_LANES,))
    return jnp.concatenate([rep, jnp.zeros(lead + (LANES - 2 * GATE_DIR_LANES,), a.dtype)], axis=-1)


def _gate_layout(a):
    a4 = a.reshape(a.shape[:-1] + (2, 2, NH_M))
    return jnp.concatenate([_gate_lanes(a4[..., :, 0, :]), _gate_lanes(a4[..., :, 1, :])], axis=-1)


def _gate_selector():
    r = jnp.arange(LANES)[:, None]
    c = jnp.arange(2 * NH_M * 2 * LANES)[None, :]
    hd, j = c // (2 * LANES), c % (2 * LANES)
    rd, rk, rh = r // GATE_DIR_LANES, (r % GATE_DIR_LANES) // NH_M, r % NH_M
    hit = (r < 2 * GATE_DIR_LANES) & (rd == hd // NH_M) & (rh == hd % NH_M) & ((j < LANES) == (rk < 2))
    return hit.astype(BF16)


def _pair_layout_cols(w):
    d_in = w.shape[0]
    w5 = w.reshape(d_in, NKV_A // 2, 2, G_A, HD_A)
    return jnp.transpose(w5, (0, 1, 3, 2, 4)).reshape(d_in, Q_W)


def _mlstm_init(state_c, state_n, state_m):
    n_rep = jnp.broadcast_to(state_n[..., None].astype(F32), state_n.shape + (DV_M,))
    return state_c.astype(F32), n_rep, _gate_lanes(state_m.astype(F32))[:, None, :]


def kernel(x_prompt, x_sample, state_c, state_n, state_m, cache_k, cache_v, c, c_ctx, w_mod, b_mod, norm_g,
           ffn1_w_gu, ffn1_w_down, ffn2_w_gu, ffn2_w_down, mlstm_w_in, mlstm_b_gate, mlstm_g_head, mlstm_w_out,
           attn_w_in, attn_sink, attn_w_out, final_g):
    bp, n_p, _ = x_prompt.shape
    bs, n_s, _ = x_sample.shape
    xp = x_prompt.reshape(bp * n_p, D_MODEL)
    xs = x_sample.reshape(bs * n_s, D_MODEL)
    tm_ffn1 = tm_in = 1024
    tm_ffn2 = {"mlstm": 512, "attn": 1024}

    cond = jnp.concatenate([c_ctx[None, :], c], axis=0)
    cond = jnp.pad(cond, ((0, COND_ROWS - cond.shape[0]), (0, 0)))
    mod_all = _mod_call(cond, w_mod, b_mod).reshape(DEPTH, COND_ROWS, N_MOD, D_MODEL)

    outs = {}
    for l in range(DEPTH):
        mod_p = mod_all[l, 0:1]
        mod_s = mod_all[l, 1:1 + bs]
        g = norm_g[l]
        xp = _ffn_call(xp, mod_p, g[0:1], ffn1_w_gu, ffn1_w_down, l, 0, n_p, tm_ffn1)
        xs = _ffn_call(xs, mod_s, g[0:1], ffn1_w_gu, ffn1_w_down, l, 0, n_s, tm_ffn1)
        i = l // 2
        if l % 2 == 0:
            w_in = mlstm_w_in[i]
            wq = w_in[:, :QK_W].astype(BF16)
            wkt = w_in[:, QK_W:2 * QK_W].T.astype(BF16)
            wvo = w_in[:, 2 * QK_W:2 * QK_W + 2 * V_W].astype(BF16)
            wg = _gate_layout(w_in[:, 2 * QK_W + 2 * V_W:]).astype(BF16)
            bg = _gate_layout(mlstm_b_gate[i].astype(F32))[None, :]
            gh = mlstm_g_head[i].astype(F32)[None, :]
            w_out = mlstm_w_out[i].astype(BF16)
            sel = _gate_selector()
            streams = [("p", xp, mod_p, n_p, tm_in, bp, None),
                       ("s", xs, mod_s, n_s, tm_in, bs, _mlstm_init(state_c[:, i], state_n[:, i], state_m[:, i]))]
            res = {}
            for tag, x, mod, n_tok, tm, bsz, init in streams:
                q, kt, v, o, scans = _mlstm_in_call(x, mod, g[1:2], wq, wkt, wvo, wg, bg, n_tok, tm)
                hf, hb, c_fin, n_fin, m_fin = _mlstm_core_call(q, kt, v, scans, sel, init, bsz, n_tok)
                res[tag] = ((hf, hb, o, gh, w_out), c_fin, n_fin, m_fin)
            mix_p, c_fin, n_fin, m_fin = res["p"]
            mix_s = res["s"][0]
            mix_key = "mlstm"
            dt = x_prompt.dtype
            m_heads = m_fin[:, 0, :2 * GATE_DIR_LANES].reshape(bp, 2, GATE_COPIES, NH_M)[:, :, 0, :]
            outs.setdefault("c", []).append(c_fin.astype(dt))
            outs.setdefault("n", []).append(n_fin[..., 0].astype(dt))
            outs.setdefault("m", []).append(m_heads.astype(dt))
        else:
            w_in = attn_w_in[i]
            wqt = _pair_layout_cols(w_in[:, :Q_W]).T.astype(BF16)
            wk = w_in[:, Q_W:Q_W + KV_W].astype(BF16)
            wv = w_in[:, Q_W + KV_W:].astype(BF16)
            wvt = w_in[:, Q_W + KV_W:].T.astype(BF16)
            w_out = _pair_layout_cols(attn_w_out[i].T).T.astype(BF16)
            sink = attn_sink[i].astype(F32)[None, :]
            qt, k, vt, kf, vf = _attn_in_call(xp, mod_p, g[1:2], wqt, wk, wvt, n_p, tm_in, wv=wv)
            mix_p = (_attn_context_call(qt, k, vt, sink, bp, n_p), w_out)
            mix_key = "attn"
            outs.setdefault("k", []).append(kf.reshape(bp, n_p, NKV_A, HD_A))
            outs.setdefault("v", []).append(vf.reshape(bp, n_p, NKV_A, HD_A))
            qt, k, vt = _attn_in_call(xs, mod_s, g[1:2], wqt, wk, wvt, n_s, tm_in, rope_tabs=_rope_tables(n_s))
            n_ctx = cache_k.shape[2]
            kx = cache_k[:, i].reshape(bs, n_ctx, KV_W).astype(BF16)
            vtx = jnp.swapaxes(cache_v[:, i].reshape(bs, n_ctx, KV_W), 1, 2).astype(BF16)
            mix_s = (_attn_latent_call(qt, k, vt, kx, vtx, sink, bs, n_s), w_out)
        fg = final_g[None, :] if l == DEPTH - 1 else None
        tm2 = tm_ffn2[mix_key]
        xp = _ffn_call(xp, mod_p, g[2:3], ffn2_w_gu, ffn2_w_down, l, 2, n_p, tm2, final_g=fg, **{mix_key: mix_p})
        xs = _ffn_call(xs, mod_s, g[2:3], ffn2_w_gu, ffn2_w_down, l, 2, n_s, tm2, final_g=fg, **{mix_key: mix_s})

    return (xp.reshape(bp, n_p, D_MODEL), xs.reshape(bs, n_s, D_MODEL),
            jnp.stack(outs["c"], axis=1), jnp.stack(outs["n"], axis=1), jnp.stack(outs["m"], axis=1),
            jnp.stack(outs["k"], axis=1), jnp.stack(outs["v"], axis=1))
```
